```python
import math
import jax, jax.numpy as jnp
from jax import lax
import numpy as np

D_MODEL = 2048
BATCH = 4
SEQ = 2048
DEPTH = 1

F32 = jnp.float32
D_MIX = D_MODEL
HG_WIDTH = D_MIX // 2
HG_HEAD_DIM = 128
HG_HEADS = HG_WIDTH // HG_HEAD_DIM
HG_CHUNK = 64
HY_WIDTH = D_MIX - HG_WIDTH
HY_ORDER = 2
HY_SHORT = 3
HY_EMB = 33
HY_BANDS = (HY_EMB - 1) // 2
HY_FILTER_HIDDEN = 64
HY_FAST_DECAY_PCT = 0.3
HY_SLOW_DECAY_PCT = 1.5
HY_DECAY_TARGET = 1e-2
N_EXPERTS = 32
TOP_K = 4
D_FF = D_MODEL
SWIGLU_ALPHA = 1.702
SWIGLU_LIMIT = 7.0
MOE_BLOCK = 128
NORM_EPS = 1e-5
D_IN = 5 * HG_WIDTH + (HY_ORDER + 1) * HY_WIDTH

kernel_name = 'hymba_hgrn2_hyena_moe_encoder'


def rms_norm(x, gain):
    xf = x.astype(F32)
    y = xf * lax.rsqrt(jnp.mean(xf * xf, axis=-1, keepdims=True) + NORM_EPS)
    return (y * gain.astype(F32)).astype(x.dtype)


def hgrn_lower_bounds(lb_logits):
    p = jax.nn.softmax(lb_logits.astype(F32), axis=0)
    return jnp.cumsum(p, axis=0)[:DEPTH]


def gla_scan(q, k, v, log_f):
    B, H, L, DK = q.shape
    DV = v.shape[-1]
    n = L // HG_CHUNK

    def chunks(a):
        return jnp.moveaxis(a.reshape(B, H, n, HG_CHUNK, a.shape[-1]), 2, 0)

    lower = jnp.tril(jnp.ones((HG_CHUNK, HG_CHUNK), bool))[:, :, None]

    def step(state, inp):
        qc, kc, vc, gc = inp
        b = jnp.cumsum(gc, axis=2)
        b_last = b[:, :, -1:, :]
        o_inter = jnp.einsum('bhtk,bhkv->bhtv', qc * jnp.exp(b), state)
        decay = jnp.exp(jnp.where(lower, b[:, :, :, None, :] - b[:, :, None, :, :], -jnp.inf))
        scores = jnp.einsum('bhtsk,bhsk->bhts', qc[:, :, :, None, :] * decay, kc)
        o_intra = jnp.einsum('bhts,bhsv->bhtv', scores, vc)
        new_state = (jnp.exp(b_last[:, :, 0, :])[..., None] * state
                     + jnp.einsum('bhsk,bhsv->bhkv', kc * jnp.exp(b_last - b), vc))
        return new_state, o_inter + o_intra

    s0 = jnp.zeros((B, H, DK, DV), F32)
    _, o = lax.scan(step, s0, (chunks(q), chunks(k), chunks(v), chunks(log_f)))
    return jnp.moveaxis(o, 0, 2).reshape(B, H, L, DV)


def hgrn2_mixer(q_raw, zf_fwd, zf_bwd, i_raw, g_raw, lb, norm_gain):
    B, L, _ = q_raw.shape

    def heads(a):
        return a.astype(F32).reshape(B, L, HG_HEADS, HG_HEAD_DIM).transpose(0, 2, 1, 3)

    q = heads(jax.nn.silu(q_raw.astype(F32)))
    v = heads(i_raw)

    def run(zf, lbd, reverse):
        f = lbd + (1.0 - lbd) * jax.nn.sigmoid(zf.astype(F32))
        k, log_f = heads(1.0 - f), heads(jnp.log(f))
        if reverse:
            o = gla_scan(jnp.flip(q, 2), jnp.flip(k, 2), jnp.flip(v, 2), jnp.flip(log_f, 2))
            return jnp.flip(o, 2)
        return gla_scan(q, k, v, log_f)

    o = run(zf_fwd, lb[0], False) + run(zf_bwd, lb[1], True)
    o = o * lax.rsqrt(jnp.mean(o * o, axis=-1, keepdims=True) + NORM_EPS)
    o = o.transpose(0, 2, 1, 3).reshape(B, L, HG_WIDTH) * norm_gain.astype(F32)
    return (o * jax.nn.silu(g_raw.astype(F32))).astype(q_raw.dtype)


def hyena_filters(L, w1, b1, w2, b2, w3, b3, w4, freq):
    pos = jnp.arange(L, dtype=F32)
    t = pos / max(L - 1, 1)
    bands = jnp.linspace(1e-4, HY_BANDS - 1, HY_BANDS, dtype=F32)
    ang = (2.0 * math.pi / L) * pos[:, None] * bands[None, :]
    z = jnp.concatenate([t[:, None], jnp.cos(ang), -jnp.sin(ang)], axis=-1)
    fr = freq.astype(F32)
    h = jnp.sin(fr * (z @ w1 + b1).astype(F32))
    h = jnp.sin(fr * (h @ w2 + b2).astype(F32))
    h = jnp.sin(fr * (h @ w3 + b3).astype(F32))
    h = (h @ w4).astype(F32)
    min_decay = math.log(HY_DECAY_TARGET) / HY_FAST_DECAY_PCT
    max_decay = math.log(HY_DECAY_TARGET) / HY_SLOW_DECAY_PCT
    deltas = jnp.abs(jnp.linspace(min_decay, max_decay, HY_WIDTH, dtype=F32))
    window = jnp.exp(-t[:, None] * deltas[None, :])
    return h.reshape(L, HY_ORDER, 2, HY_WIDTH) * window[:, None, None, :]


def two_sided_fftconv(u, h_fwd, h_bwd, bias):
    L = u.shape[1]
    C = h_fwd.shape[-1]
    kern = jnp.concatenate([h_fwd, jnp.zeros((1, C), F32), h_bwd[:0:-1]], axis=0)
    uf = u.astype(F32)
    spec = jnp.fft.rfft(uf, n=2 * L, axis=1) * jnp.fft.rfft(kern, n=2 * L, axis=0)[None]
    y = jnp.fft.irfft(spec, n=2 * L, axis=1)[:, :L]
    return y + uf * bias


def hyena_mixer(proj, conv_w, conv_b, filters, hy_bias, norm_gain):
    C3 = proj.shape[-1]
    pad = HY_SHORT // 2
    u = lax.conv_general_dilated(proj, conv_w[:, None, :], window_strides=(1,), padding=((pad, pad),),
                                 dimension_numbers=('NWC', 'WIO', 'NWC'), feature_group_count=C3) + conv_b
    v, x1, x2 = jnp.split(u, HY_ORDER + 1, axis=-1)
    z = v.astype(F32)
    for o, gate in enumerate((x1, x2)):
        z = gate.astype(F32) * two_sided_fftconv(z, filters[:, o, 0], filters[:, o, 1], hy_bias[o].astype(F32))
    return rms_norm(z, norm_gain).astype(proj.dtype)


def swiglu_clamped(h):
    glu, lin = jnp.split(h, 2, axis=-1)
    glu = jnp.minimum(glu, SWIGLU_LIMIT)
    lin = jnp.clip(lin, -SWIGLU_LIMIT, SWIGLU_LIMIT)
    return glu * jax.nn.sigmoid(SWIGLU_ALPHA * glu) * (lin + 1.0)


def moe_ffn(xn, w_router, b_router, w1, b1, w2, b2):
    B, L, D = xn.shape
    N = B * L
    A = N * TOP_K
    xt = xn.reshape(N, D)
    logits = (xt @ w_router + b_router).astype(F32)
    top_logits, top_idx = lax.top_k(logits, TOP_K)
    gates = jax.nn.softmax(top_logits, axis=-1)
    e_flat = top_idx.reshape(A)
    tok_flat = jnp.repeat(jnp.arange(N, dtype=jnp.int32), TOP_K)
    order = jnp.argsort(e_flat)
    e_sorted = e_flat[order]
    counts = jnp.bincount(e_flat, length=N_EXPERTS)
    padded = (counts + MOE_BLOCK - 1) // MOE_BLOCK * MOE_BLOCK
    starts = jnp.cumsum(counts) - counts
    pends = jnp.cumsum(padded)
    dest = (pends - padded)[e_sorted] + jnp.arange(A, dtype=jnp.int32) - starts[e_sorted]
    n_blocks = (A + N_EXPERTS * (MOE_BLOCK - 1) + MOE_BLOCK - 1) // MOE_BLOCK
    P = n_blocks * MOE_BLOCK
    slot_tok = jnp.full((P,), N, jnp.int32).at[dest].set(tok_flat[order])
    slot_gate = jnp.zeros((P,), F32).at[dest].set(gates.reshape(A)[order])
    block_start = jnp.arange(n_blocks, dtype=jnp.int32) * MOE_BLOCK
    block_expert = jnp.minimum(jnp.sum(block_start[:, None] >= pends[None, :], axis=1), N_EXPERTS - 1)
    xpad = jnp.concatenate([xt, jnp.zeros((1, D), xt.dtype)], axis=0)
    x_blocks = xpad[slot_tok].reshape(n_blocks, MOE_BLOCK, D)

    def expert_block(args):
        xb, e = args
        h = swiglu_clamped(xb @ w1[e] + b1[e])
        return h @ w2[e] + b2[e]

    y = lax.map(expert_block, (x_blocks, block_expert)).reshape(P, D)
    y = y.astype(F32) * slot_gate[:, None]
    out = jnp.zeros((N + 1, D), F32).at[slot_tok].add(y)[:N]
    return out.reshape(B, L, D).astype(xn.dtype)


def setup_inputs(seed: int = 0) -> dict:
    key = jax.random.key(seed)
    ks = jax.random.split(key, 26)
    nrm = jax.random.normal
    lb_logits = 0.1 * nrm(ks[3], (DEPTH + 1, 2, HG_WIDTH), F32)
    lb_logits = lb_logits.at[-1].add(2.0)
    return {
        'x': nrm(ks[0], (BATCH, SEQ, D_MODEL), F32),
        'norm_mix_gain': 1.0 + 0.02 * nrm(ks[1], (DEPTH, D_MODEL), F32),
        'w_in': nrm(ks[2], (DEPTH, D_MODEL, D_IN), F32) * D_MODEL ** -0.5,
        'hgrn_lb_logits': lb_logits,
        'hgrn_norm_gain': 1.0 + 0.02 * nrm(ks[4], (DEPTH, HG_WIDTH), F32),
        'hy_conv_w': nrm(ks[5], (DEPTH, HY_SHORT, (HY_ORDER + 1) * HY_WIDTH), F32) * HY_SHORT ** -0.5,
        'hy_conv_b': 0.02 * nrm(ks[6], (DEPTH, (HY_ORDER + 1) * HY_WIDTH), F32),
        'hy_filt_w1': nrm(ks[7], (DEPTH, HY_EMB, HY_FILTER_HIDDEN), F32) * HY_EMB ** -0.5,
        'hy_filt_b1': 0.02 * nrm(ks[8], (DEPTH, HY_FILTER_HIDDEN), F32),
        'hy_filt_w2': nrm(ks[9], (DEPTH, HY_FILTER_HIDDEN, HY_FILTER_HIDDEN), F32) * HY_FILTER_HIDDEN ** -0.5,
        'hy_filt_b2': 0.02 * nrm(ks[10], (DEPTH, HY_FILTER_HIDDEN), F32),
        'hy_filt_w3': nrm(ks[11], (DEPTH, HY_FILTER_HIDDEN, HY_FILTER_HIDDEN), F32) * HY_FILTER_HIDDEN ** -0.5,
        'hy_filt_b3': 0.02 * nrm(ks[12], (DEPTH, HY_FILTER_HIDDEN), F32),
        'hy_filt_w4': nrm(ks[13], (DEPTH, HY_FILTER_HIDDEN, HY_ORDER * 2 * HY_WIDTH), F32) * (0.1 * HY_FILTER_HIDDEN ** -0.5),
        'hy_filt_freq': 1.0 + 0.1 * nrm(ks[14], (DEPTH, HY_FILTER_HIDDEN), F32),
        'hy_bias': 0.5 * nrm(ks[15], (DEPTH, HY_ORDER, HY_WIDTH), F32),
        'hy_norm_gain': 1.0 + 0.02 * nrm(ks[16], (DEPTH, HY_WIDTH), F32),
        'w_out': nrm(ks[17], (DEPTH, D_MIX, D_MODEL), F32) * D_MIX ** -0.5,
        'norm_ffn_gain': 1.0 + 0.02 * nrm(ks[18], (DEPTH, D_MODEL), F32),
        'w_router': nrm(ks[19], (DEPTH, D_MODEL, N_EXPERTS), F32) * D_MODEL ** -0.5,
        'b_router': 0.01 * nrm(ks[20], (DEPTH, N_EXPERTS), F32),
        'moe_w1': nrm(ks[21], (DEPTH, N_EXPERTS, D_MODEL, 2 * D_FF), F32) * D_MODEL ** -0.5,
        'moe_b1': 0.01 * nrm(ks[22], (DEPTH, N_EXPERTS, 2 * D_FF), F32),
        'moe_w2': nrm(ks[23], (DEPTH, N_EXPERTS, D_FF, D_MODEL), F32) * D_FF ** -0.5,
        'moe_b2': 0.01 * nrm(ks[24], (DEPTH, N_EXPERTS, D_MODEL), F32),
        'final_norm_gain': 1.0 + 0.02 * nrm(ks[25], (D_MODEL,), F32),
    }


def reference(x, norm_mix_gain, w_in, hgrn_lb_logits, hgrn_norm_gain, hy_conv_w, hy_conv_b,
              hy_filt_w1, hy_filt_b1, hy_filt_w2, hy_filt_b2, hy_filt_w3, hy_filt_b3, hy_filt_w4,
              hy_filt_freq, hy_bias, hy_norm_gain, w_out, norm_ffn_gain, w_router, b_router,
              moe_w1, moe_b1, moe_w2, moe_b2, final_norm_gain):
    L = x.shape[1]
    lbs = hgrn_lower_bounds(hgrn_lb_logits)
    split_at = [HG_WIDTH * i for i in range(1, 6)]
    for layer in range(DEPTH):
        h = rms_norm(x, norm_mix_gain[layer])
        proj = h @ w_in[layer]
        q_raw, zf_fwd, zf_bwd, i_raw, g_raw, hy_proj = jnp.split(proj, split_at, axis=-1)
        hg = hgrn2_mixer(q_raw, zf_fwd, zf_bwd, i_raw, g_raw, lbs[layer], hgrn_norm_gain[layer])
        filters = hyena_filters(L, hy_filt_w1[layer], hy_filt_b1[layer], hy_filt_w2[layer], hy_filt_b2[layer],
                                hy_filt_w3[layer], hy_filt_b3[layer], hy_filt_w4[layer], hy_filt_freq[layer])
        hy = hyena_mixer(hy_proj, hy_conv_w[layer], hy_conv_b[layer], filters, hy_bias[layer], hy_norm_gain[layer])
        x = x + jnp.concatenate([hg, hy], axis=-1) @ w_out[layer]
        x = x + moe_ffn(rms_norm(x, norm_ffn_gain[layer]), w_router[layer], b_router[layer],
                        moe_w1[layer], moe_b1[layer], moe_w2[layer], moe_b2[layer])
    return rms_norm(x, final_norm_gain)
```

```python
import functools
import math

import numpy as np
import jax
import jax.numpy as jnp
from jax import lax
from jax.experimental import pallas as pl
from jax.experimental.pallas import tpu as pltpu

F32 = jnp.float32
BF16 = jnp.bfloat16

D_MODEL = 2048
HG_WIDTH = 1024
HG_HEAD_DIM = 128
HG_HEADS = HG_WIDTH // HG_HEAD_DIM
HY_WIDTH = 1024
HY_ORDER = 2
HY_EMB = 33
HY_BANDS = (HY_EMB - 1) // 2
HY_FILTER_HIDDEN = 64
HY_FAST_DECAY_PCT = 0.3
HY_SLOW_DECAY_PCT = 1.5
HY_DECAY_TARGET = 1e-2
N_EXPERTS = 32
TOP_K = 4
D_FF = D_MODEL
SWIGLU_ALPHA = 1.702
SWIGLU_LIMIT = 7.0
NORM_EPS = 1e-5
D_IN = 5 * HG_WIDTH + (HY_ORDER + 1) * HY_WIDTH

LANES = 128
GLA_CHUNK = 128
GLA_LEVELS = (64, 32, 16, 8, 4, 2, 1)
VMEM_LIMIT = 56 * 1024 * 1024


def _cparams(sem, vmem=VMEM_LIMIT):
    return pltpu.CompilerParams(dimension_semantics=sem, vmem_limit_bytes=vmem)


def _inproj_kernel(x_ref, g_ref, w_ref, o_ref):
    x = x_ref[...]
    ms = jnp.mean(x * x, axis=-1, keepdims=True)
    h = (x * lax.rsqrt(ms + NORM_EPS) * g_ref[...]).astype(BF16)
    o_ref[...] = jnp.dot(h, w_ref[...], preferred_element_type=F32)


def in_projection(x2d, gain, w_bf16, tm=512, tn=1024):
    n, d = x2d.shape
    dout = w_bf16.shape[1]
    return pl.pallas_call(
        _inproj_kernel,
        grid=(dout // tn, n // tm),
        in_specs=[
            pl.BlockSpec((tm, d), lambda j, i: (i, 0)),
            pl.BlockSpec((1, d), lambda j, i: (0, 0)),
            pl.BlockSpec((d, tn), lambda j, i: (0, j)),
        ],
        out_specs=pl.BlockSpec((tm, tn), lambda j, i: (i, j)),
        out_shape=jax.ShapeDtypeStruct((n, dout), F32),
        compiler_params=_cparams(("arbitrary", "arbitrary")),
        name="in_projection",
    )(x2d, gain.reshape(1, d), w_bf16)


def _gla_constants():
    c = GLA_CHUNK
    t = np.arange(c)[:, None]
    r = np.arange(c)[None, :]
    fwd = [r <= t, r > t]
    bwd = [r >= t, r < t]
    for m in GLA_LEVELS:
        pos = t % (2 * m)
        mid = t - pos + m
        second = pos >= m
        fwd.append(np.where(second, (r >= mid) & (r <= t), (r > t) & (r < mid)))
        bwd.append(np.where(second, (r >= mid) & (r < t), (r >= t) & (r < mid)))
    x = t ^ r
    lv = np.full((c, c), -1, np.int32)
    for j in range(int(math.log2(c))):
        lv = np.where((x >> j) == 1, j, lv)
    mf = jnp.asarray(np.concatenate(fwd, 0).astype(np.float32), BF16)
    mb = jnp.asarray(np.concatenate(bwd, 0).astype(np.float32), BF16)
    return mf, mb, jnp.asarray(lv, jnp.int32)


def _hgrn_kernel(q_ref, ff_ref, fb_ref, i_ref, g_ref, lb_ref, gain_ref, mf_ref, mb_ref, lv_ref,
                 o_ref, acc_ref, qb_ref, kb_ref, db_ref):
    c = GLA_CHUNK
    n_chunks = q_ref.shape[0] // c
    lb = lb_ref[...]
    l0, l1 = lb[0], lb[1]
    mx = jnp.maximum(l0, l1)
    e0 = jnp.exp(l0 - mx)
    e1 = jnp.exp(l1 - mx)
    p0 = e0 / (e0 + e1)
    lb_f = p0[0:1, :]
    lb_b = p0[1:2, :]
    lv = lv_ref[...]
    row = lax.broadcasted_iota(jnp.int32, (c, LANES), 0)
    nt = (((1,), (1,)), ((), ()))

    def gates(z, lower):
        f = lower + (1.0 - lower) * jax.nn.sigmoid(z)
        return 1.0 - f, jnp.log(f)

    def exponents(m_ref, g):
        g_hi = g.astype(BF16)
        g_lo = (g - g_hi.astype(F32)).astype(BF16)
        m = m_ref[...]
        return (jnp.dot(m, g_hi, preferred_element_type=F32)
                + jnp.dot(m, g_lo, preferred_element_type=F32))

    def fwd_body(ci, st):
        rows = pl.ds(pl.multiple_of(ci * c, c), c)
        qr = q_ref[rows, :]
        q = qr * jax.nn.sigmoid(qr)
        v = i_ref[rows, :]
        kf, gf = gates(ff_ref[rows, :], lb_f)
        kb, gb = gates(fb_ref[rows, :], lb_b)
        ef_all = exponents(mf_ref, gf)
        eb_all = exponents(mb_ref, gb)
        scores = jnp.zeros((c, c), F32)
        for li, m in enumerate(GLA_LEVELS):
            ef = jnp.exp(ef_all[(2 + li) * c:(3 + li) * c])
            eb = jnp.exp(eb_all[(2 + li) * c:(3 + li) * c])
            second = (row & m) != 0
            a = jnp.concatenate([jnp.where(second, q * ef, 0.0), jnp.where(second, 0.0, q * eb)], axis=1)
            b = jnp.concatenate([jnp.where(second, 0.0, kf * ef), jnp.where(second, kb * eb, 0.0)], axis=1)
            s = lax.dot_general(a.astype(BF16), b.astype(BF16), nt, preferred_element_type=F32)
            scores = jnp.where(lv == int(math.log2(m)), s, scores)
        v_bf = v.astype(BF16)
        o = jnp.dot(scores.astype(BF16), v_bf, preferred_element_type=F32)
        o = o + jnp.sum(q * (kf + kb), axis=-1, keepdims=True) * v
        b_inc = ef_all[0:c]
        q_dec = (q * jnp.exp(b_inc)).astype(BF16)
        o = o + lax.dot_general(q_dec, st.astype(BF16), nt, preferred_element_type=F32)
        k_dec = (kf * jnp.exp(ef_all[c:2 * c])).astype(BF16)
        vt_bf = v.T.astype(BF16)
        st = st * jnp.exp(b_inc[c - 1:c, :]) + jnp.dot(vt_bf, k_dec, preferred_element_type=F32)
        acc_ref[rows, :] = o
        bb = eb_all[0:c]
        qb_ref[rows, :] = (q * jnp.exp(bb)).astype(BF16)
        kb_ref[rows, :] = (kb * jnp.exp(eb_all[c:2 * c])).astype(BF16)
        db_ref[ci] = jnp.broadcast_to(jnp.exp(bb[0:1, :]), (8, LANES))
        return st

    lax.fori_loop(0, n_chunks, fwd_body, jnp.zeros((c, c), F32))

    gain = gain_ref[...]

    def bwd_body(i, st):
        ci = n_chunks - 1 - i
        rows = pl.ds(pl.multiple_of(ci * c, c), c)
        o = acc_ref[rows, :] + lax.dot_general(qb_ref[rows, :], st.astype(BF16), nt,
                                               preferred_element_type=F32)
        vt_bf = i_ref[rows, :].T.astype(BF16)
        st = st * db_ref[ci][0:1, :] + jnp.dot(vt_bf, kb_ref[rows, :], preferred_element_type=F32)
        o = o * lax.rsqrt(jnp.mean(o * o, axis=-1, keepdims=True) + NORM_EPS) * gain
        gr = g_ref[rows, :]
        o_ref[rows, :] = (o * (gr * jax.nn.sigmoid(gr))).astype(o_ref.dtype)
        return st

    lax.fori_loop(0, n_chunks, bwd_body, jnp.zeros((c, c), F32))


def hgrn2_mixer(proj3, lb_logits, norm_gain):
    b, l, _ = proj3.shape
    h, dh = HG_HEADS, HG_HEAD_DIM
    mf, mb, lv = _gla_constants()
    nblk = mf.shape[0]

    def col(off):
        return pl.BlockSpec((None, l, dh), lambda bi, hi, off=off: (bi, 0, off + hi))

    const2 = lambda bi, hi: (0, 0)
    return pl.pallas_call(
        _hgrn_kernel,
        grid=(b, h),
        in_specs=[col(0), col(h), col(2 * h), col(3 * h), col(4 * h),
                  pl.BlockSpec((2, 2, dh), lambda bi, hi: (0, 0, hi)),
                  pl.BlockSpec((1, dh), lambda bi, hi: (0, hi)),
                  pl.BlockSpec((nblk, GLA_CHUNK), const2),
                  pl.BlockSpec((nblk, GLA_CHUNK), const2),
                  pl.BlockSpec((GLA_CHUNK, GLA_CHUNK), const2)],
        out_specs=pl.BlockSpec((None, l, dh), lambda bi, hi: (bi, 0, hi)),
        out_shape=jax.ShapeDtypeStruct((b, l, HG_WIDTH), BF16),
        scratch_shapes=[pltpu.VMEM((l, dh), F32), pltpu.VMEM((l, dh), BF16), pltpu.VMEM((l, dh), BF16),
                        pltpu.VMEM((l // GLA_CHUNK, 8, LANES), F32)],
        compiler_params=_cparams(("arbitrary", "arbitrary")),
        name="hgrn2_mixer",
    )(proj3, proj3, proj3, proj3, proj3, lb_logits, norm_gain.reshape(1, HG_WIDTH), mf, mb, lv)


def _dft_tables(l):
    n = 2 * l
    k2 = 2 * lax.broadcasted_iota(jnp.int32, (l, l), 0) + 1
    m1 = lax.broadcasted_iota(jnp.int32, (l, l), 1)
    ang_s = ((k2 * (2 * m1 + 1)) % (4 * n)).astype(F32) * (2.0 * math.pi / (4 * n))
    ang_f = ((k2 * m1) % (2 * n)).astype(F32) * (2.0 * math.pi / (2 * n))
    sym = jnp.stack([jnp.cos(ang_s), jnp.sin(ang_s)]).astype(BF16)
    flt = jnp.stack([jnp.cos(ang_f), jnp.sin(ang_f)]).astype(BF16)
    return sym, flt


def _filter_features(l):
    pos = jnp.arange(l, dtype=F32)
    t = pos / max(l - 1, 1)
    bands = jnp.linspace(1e-4, HY_BANDS - 1, HY_BANDS, dtype=F32)
    ang = (2.0 * math.pi / l) * pos[:, None] * bands[None, :]
    z = jnp.concatenate([t[:, None], jnp.cos(ang), -jnp.sin(ang)], axis=-1)
    z = jnp.pad(z, ((0, 0), (0, LANES - HY_EMB)))
    min_decay = math.log(HY_DECAY_TARGET) / HY_FAST_DECAY_PCT
    max_decay = math.log(HY_DECAY_TARGET) / HY_SLOW_DECAY_PCT
    deltas = jnp.abs(jnp.linspace(min_decay, max_decay, HY_WIDTH, dtype=F32))
    window = jnp.exp(-t[:, None] * deltas[None, :])
    return z, window


def _filter_kernel(z_ref, win_ref, w1_ref, b1_ref, w2_ref, b2_ref, w3_ref, b3_ref, w4_ref, fr_ref,
                   sum_ref, diff_ref):
    hp = lax.Precision.HIGHEST
    fr = fr_ref[...]
    h = jnp.sin(fr * (jnp.dot(z_ref[...], w1_ref[...], precision=hp, preferred_element_type=F32) + b1_ref[...]))
    h = jnp.sin(fr * (jnp.dot(h, w2_ref[...], precision=hp, preferred_element_type=F32) + b2_ref[...]))
    h = jnp.sin(fr * (jnp.dot(h, w3_ref[...], precision=hp, preferred_element_type=F32) + b3_ref[...]))
    h = jnp.dot(h, w4_ref[...], precision=hp, preferred_element_type=F32)
    win = win_ref[...]
    tl = h.shape[0]
    lag = pl.program_id(0) * tl + lax.broadcasted_iota(jnp.int32, (tl, HY_WIDTH), 0)
    for o in range(HY_ORDER):
        hf = h[:, (2 * o) * HY_WIDTH:(2 * o + 1) * HY_WIDTH] * win
        hb = h[:, (2 * o + 1) * HY_WIDTH:(2 * o + 2) * HY_WIDTH] * win
        hb = jnp.where(lag == 0, 0.0, hb)
        sum_ref[:, o * HY_WIDTH:(o + 1) * HY_WIDTH] = (hf + hb).astype(sum_ref.dtype)
        diff_ref[:, o * HY_WIDTH:(o + 1) * HY_WIDTH] = (hf - hb).astype(diff_ref.dtype)


def hyena_filter_taps(l, w1, b1, w2, b2, w3, b3, w4, freq, tl=256):
    z, window = _filter_features(l)
    w1p = jnp.pad(w1, ((0, LANES - HY_EMB), (0, 0)))
    hid = HY_FILTER_HIDDEN
    full = lambda shape: pl.BlockSpec(shape, lambda i: (0,) * len(shape))
    nout = HY_ORDER * HY_WIDTH
    return pl.pallas_call(
        _filter_kernel,
        grid=(l // tl,),
        in_specs=[pl.BlockSpec((tl, LANES), lambda i: (i, 0)),
                  pl.BlockSpec((tl, HY_WIDTH), lambda i: (i, 0)),
                  full((LANES, hid)), full((1, hid)), full((hid, hid)), full((1, hid)),
                  full((hid, hid)), full((1, hid)), full((hid, 2 * nout)), full((1, hid))],
        out_specs=[pl.BlockSpec((tl, nout), lambda i: (i, 0)), pl.BlockSpec((tl, nout), lambda i: (i, 0))],
        out_shape=[jax.ShapeDtypeStruct((l, nout), BF16), jax.ShapeDtypeStruct((l, nout), BF16)],
        compiler_params=_cparams(("arbitrary",)),
        name="hyena_filter_taps",
    )(z, window, w1p, b1.reshape(1, hid), w2, b2.reshape(1, hid), w3, b3.reshape(1, hid), w4,
      freq.reshape(1, hid))


def _spectrum_kernel(f_ref, h_ref, o_ref):
    o_ref[...] = jnp.dot(f_ref[...], h_ref[...], preferred_element_type=F32).astype(o_ref.dtype)


def hyena_filter_spectra(flt, taps, tn=512):
    _, l, nout = taps.shape
    return pl.pallas_call(
        _spectrum_kernel,
        grid=(2, nout // tn),
        in_specs=[pl.BlockSpec((None, l, l), lambda p, j: (p, 0, 0)),
                  pl.BlockSpec((None, l, tn), lambda p, j: (p, 0, j))],
        out_specs=pl.BlockSpec((None, l, tn), lambda p, j: (p, 0, j)),
        out_shape=jax.ShapeDtypeStruct((2, l, nout), BF16),
        compiler_params=_cparams(("arbitrary", "arbitrary")),
        name="hyena_filter_spectra",
    )(flt, taps)


def _hyena_kernel(xv_ref, x1_ref, x2_ref, cw_ref, cb_ref, sym_ref, sp0_ref, sp1_ref, bias_ref, o_ref,
                  z_s, g_s, zb_s, pa_s, pb_s, *, rb):
    l, tc = xv_ref.shape
    scale = 2.0 / (2 * l)
    first = lax.broadcasted_iota(jnp.int32, (l, tc), 0) == 0
    last = lax.broadcasted_iota(jnp.int32, (l, tc), 0) == l - 1

    def short_conv(x_ref, part):
        x = x_ref[...]
        w = cw_ref[part]
        prev = jnp.where(first, 0.0, pltpu.roll(x, 1, 0))
        nxt = jnp.where(last, 0.0, pltpu.roll(x, l - 1, 0))
        return prev * w[0:1, :] + x * w[1:2, :] + nxt * w[2:3, :] + cb_ref[part]

    z_s[...] = short_conv(xv_ref, 0)
    for o, (gate_ref, part, sp_ref) in enumerate(((x1_ref, 1, sp0_ref), (x2_ref, 2, sp1_ref))):
        zb_s[...] = z_s[...].astype(BF16)
        g_s[...] = short_conv(gate_ref, part)
        for r0 in range(0, l, rb):
            a = jnp.dot(sym_ref[0, r0:r0 + rb, :], zb_s[...], preferred_element_type=F32)
            b = jnp.dot(sym_ref[1, r0:r0 + rb, :], zb_s[...], preferred_element_type=F32)
            hr = sp_ref[0, r0:r0 + rb, :].astype(F32)
            hs = sp_ref[1, r0:r0 + rb, :].astype(F32)
            pa_s[r0:r0 + rb, :] = (a * hr - b * hs).astype(BF16)
            pb_s[r0:r0 + rb, :] = (a * hs + b * hr).astype(BF16)
        for r0 in range(0, l, rb):
            y = (jnp.dot(sym_ref[0, r0:r0 + rb, :], pa_s[...], preferred_element_type=F32)
                 + jnp.dot(sym_ref[1, r0:r0 + rb, :], pb_s[...], preferred_element_type=F32)) * scale
            y = y + z_s[r0:r0 + rb, :] * bias_ref[o]
            z_s[r0:r0 + rb, :] = g_s[r0:r0 + rb, :] * y
    o_ref[...] = z_s[...].astype(o_ref.dtype)


def hyena_mixer(proj3, conv_w, conv_b, sym, spectra, hy_bias, tc=256, rb=1024):
    b, l, _ = proj3.shape
    nct = HY_WIDTH // tc
    base = (5 * HG_WIDTH) // tc

    def col(part):
        return pl.BlockSpec((None, l, tc), lambda ci, bi, part=part: (bi, 0, base + part * nct + ci))

    def spec_cols(o):
        return pl.BlockSpec((2, l, tc), lambda ci, bi, o=o: (0, 0, o * nct + ci))

    cw = conv_w.reshape(3, HY_ORDER + 1, HY_WIDTH).transpose(1, 0, 2)
    cb = conv_b.reshape(HY_ORDER + 1, 1, HY_WIDTH)
    return pl.pallas_call(
        functools.partial(_hyena_kernel, rb=rb),
        grid=(nct, b),
        in_specs=[col(0), col(1), col(2),
                  pl.BlockSpec((HY_ORDER + 1, 3, tc), lambda ci, bi: (0, 0, ci)),
                  pl.BlockSpec((HY_ORDER + 1, 1, tc), lambda ci, bi: (0, 0, ci)),
                  pl.BlockSpec((2, l, l), lambda ci, bi: (0, 0, 0), pipeline_mode=pl.Buffered(1)),
                  spec_cols(0), spec_cols(1),
                  pl.BlockSpec((HY_ORDER, 1, tc), lambda ci, bi: (0, 0, ci))],
        out_specs=pl.BlockSpec((None, l, tc), lambda ci, bi: (bi, 0, ci)),
        out_shape=jax.ShapeDtypeStruct((b, l, HY_WIDTH), BF16),
        scratch_shapes=[pltpu.VMEM((l, tc), F32), pltpu.VMEM((l, tc), F32), pltpu.VMEM((l, tc), BF16),
                        pltpu.VMEM((l, tc), BF16), pltpu.VMEM((l, tc), BF16)],
        compiler_params=_cparams(("arbitrary", "arbitrary")),
        name="hyena_mixer",
    )(proj3, proj3, proj3, cw, cb, sym, spectra, spectra, hy_bias.reshape(HY_ORDER, 1, HY_WIDTH))


def _outproj_router_kernel(hg_ref, hy_ref, x_ref, w_ref, hyg_ref, fg_ref, wr_ref, br_ref, tri_ref,
                           x2_ref, xn_ref, meta_ref, cnt_ref, carry_ref):
    @pl.when(pl.program_id(0) == 0)
    def _():
        carry_ref[...] = jnp.zeros_like(carry_ref)

    hy = hy_ref[...].astype(F32)
    hy = hy * lax.rsqrt(jnp.mean(hy * hy, axis=-1, keepdims=True) + NORM_EPS) * hyg_ref[...]
    m = (jnp.dot(hg_ref[...], w_ref[0:HG_WIDTH, :], preferred_element_type=F32)
         + jnp.dot(hy.astype(BF16), w_ref[HG_WIDTH:, :], preferred_element_type=F32))
    x2 = x_ref[...] + m
    x2_ref[...] = x2
    xn = x2 * lax.rsqrt(jnp.mean(x2 * x2, axis=-1, keepdims=True) + NORM_EPS) * fg_ref[...]
    xn_ref[...] = xn
    logits = jnp.dot(xn, wr_ref[...], precision=lax.Precision.HIGHEST,
                     preferred_element_type=F32) + br_ref[...]
    tm = logits.shape[0]
    lane = lax.broadcasted_iota(jnp.int32, (tm, LANES), 1).astype(F32)
    neg = jnp.float32(-jnp.inf)
    logits = jnp.where(lane < N_EXPERTS, logits, neg)
    tops, idxs = [], []
    for _ in range(TOP_K):
        mx = jnp.max(logits, axis=-1, keepdims=True)
        idx = jnp.min(jnp.where(logits == mx, lane, float(LANES)), axis=-1, keepdims=True)
        tops.append(mx)
        idxs.append(idx)
        logits = jnp.where(lane == idx, neg, logits)
    exps = [jnp.exp(t - tops[0]) for t in tops]
    denom = exps[0] + exps[1] + exps[2] + exps[3]
    onehot = jnp.zeros((tm, LANES), F32)
    for idx in idxs:
        onehot = onehot + jnp.where(lane == idx, 1.0, 0.0)
    cum = jnp.dot(tri_ref[...], onehot.astype(BF16), preferred_element_type=F32) + carry_ref[0:1, :]
    meta = jnp.zeros((tm, LANES), F32)
    for k in range(TOP_K):
        rank = jnp.sum(jnp.where(lane == idxs[k], cum, 0.0), axis=-1, keepdims=True)
        meta = jnp.where(lane == k, idxs[k], meta)
        meta = jnp.where(lane == TOP_K + k, rank, meta)
        meta = jnp.where(lane == 2 * TOP_K + k, exps[k] / denom, meta)
    meta_ref[...] = meta
    carry = carry_ref[...] + jnp.sum(onehot, axis=0, keepdims=True)
    carry_ref[...] = carry
    cnt_ref[...] = carry


def outproj_router(hg2d, hy2d, x2d, w_out_bf16, hy_gain, ffn_gain, w_router, b_router, tm=256):
    n, d = x2d.shape
    wr = jnp.pad(w_router, ((0, 0), (0, LANES - N_EXPERTS)))
    br = jnp.pad(b_router, (0, LANES - N_EXPERTS)).reshape(1, LANES)
    tri = jnp.asarray(np.tril(np.ones((tm, tm), np.float32), -1), BF16)
    row = lambda w: pl.BlockSpec((tm, w), lambda i: (i, 0))
    full = lambda shape, **kw: pl.BlockSpec(shape, lambda i: (0,) * len(shape), **kw)
    return pl.pallas_call(
        _outproj_router_kernel,
        grid=(n // tm,),
        in_specs=[row(HG_WIDTH), row(HY_WIDTH), row(d),
                  full((d, d), pipeline_mode=pl.Buffered(1)),
                  full((1, HY_WIDTH)), full((1, d)), full((d, LANES)), full((1, LANES)), full((tm, tm))],
        out_specs=[row(d), row(d), row(LANES), full((8, LANES))],
        out_shape=[jax.ShapeDtypeStruct((n, d), F32), jax.ShapeDtypeStruct((n, d), F32),
                   jax.ShapeDtypeStruct((n, LANES), F32), jax.ShapeDtypeStruct((8, LANES), F32)],
        scratch_shapes=[pltpu.VMEM((8, LANES), F32)],
        compiler_params=_cparams(("arbitrary",)),
        name="outproj_router",
    )(hg2d, hy2d, x2d, w_out_bf16, hy_gain.reshape(1, HY_WIDTH), ffn_gain.reshape(1, d), wr, br, tri)


def _slot_table_kernel(dest_ref, slot_ref):
    n_slots = slot_ref.shape[0]
    n_pairs = dest_ref.shape[0]

    def fill(i, carry):
        slot_ref[i] = 0
        return carry

    lax.fori_loop(0, n_slots, fill, 0)

    def scatter(a, carry):
        slot_ref[dest_ref[a]] = a // TOP_K
        return carry

    lax.fori_loop(0, n_pairs, scatter, 0)


def build_slot_table(dest_flat, n_slots):
    return pl.pallas_call(
        _slot_table_kernel,
        in_specs=[pl.BlockSpec(memory_space=pltpu.SMEM)],
        out_specs=pl.BlockSpec(memory_space=pltpu.SMEM),
        out_shape=jax.ShapeDtypeStruct((n_slots,), jnp.int32),
        name="build_slot_table",
    )(dest_flat)


def _row_gather_kernel(slot_ref, used_ref, x_hbm, o_ref, buf_ref, sem):
    t = pl.program_id(0)
    tr = buf_ref.shape[0]

    @pl.when(t < used_ref[0])
    def _():
        base = t * tr

        def row_copy(r):
            return pltpu.make_async_copy(x_hbm.at[pl.ds(slot_ref[base + r], 1)], buf_ref.at[pl.ds(r, 1)], sem)

        def start(r, carry):
            row_copy(r).start()
            return carry

        lax.fori_loop(0, tr, start, 0)

        def wait(r, carry):
            row_copy(r).wait()
            return carry

        lax.fori_loop(0, tr, wait, 0)
        o_ref[...] = buf_ref[...].astype(o_ref.dtype)

    @pl.when(t >= used_ref[0])
    def _():
        o_ref[...] = jnp.zeros(o_ref.shape, o_ref.dtype)


def gather_rows(slot_tok, used_rows_tiles, xn, n_slots, tr=256):
    n, d = xn.shape
    grid_spec = pltpu.PrefetchScalarGridSpec(
        num_scalar_prefetch=2,
        grid=(n_slots // tr,),
        in_specs=[pl.BlockSpec(memory_space=pl.ANY)],
        out_specs=pl.BlockSpec((tr, d), lambda t, slot, used: (t, 0)),
        scratch_shapes=[pltpu.VMEM((tr, d), F32), pltpu.SemaphoreType.DMA(())],
    )
    return pl.pallas_call(
        _row_gather_kernel,
        grid_spec=grid_spec,
        out_shape=jax.ShapeDtypeStruct((n_slots, d), BF16),
        compiler_params=_cparams(("arbitrary",)),
        name="moe_gather_rows",
    )(slot_tok, used_rows_tiles, xn)


def _expert_kernel(te_ref, tv_ref, tb_ref, x_ref, wg_ref, wl_ref, bg_ref, bl_ref, w2_ref, b2_ref, o_ref,
                   wg_s, wl_s, w2_s, *, sub):
    j = pl.program_id(0)
    f = pl.program_id(1)
    valid = tv_ref[j]
    tm = x_ref.shape[0]

    @pl.when(jnp.logical_and(valid == 0, f == 0))
    def _():
        o_ref[...] = jnp.zeros(o_ref.shape, o_ref.dtype)

    @pl.when(valid > 0)
    def _():
        wg_s[...] = wg_ref[...].astype(BF16)
        wl_s[...] = wl_ref[...].astype(BF16)
        w2_s[...] = w2_ref[...].astype(BF16)
        for sb in range(tm // sub):
            @pl.when(sb * sub < valid)
            def _(sb=sb):
                rows = pl.ds(sb * sub, sub)
                x = x_ref[rows, :]
                glu = jnp.dot(x, wg_s[...], preferred_element_type=F32) + bg_ref[...]
                lin = jnp.dot(x, wl_s[...], preferred_element_type=F32) + bl_ref[...]
                glu = jnp.minimum(glu, SWIGLU_LIMIT)
                lin = jnp.clip(lin, -SWIGLU_LIMIT, SWIGLU_LIMIT)
                act = glu * jax.nn.sigmoid(SWIGLU_ALPHA * glu) * (lin + 1.0)
                y = jnp.dot(act.astype(BF16), w2_s[...], preferred_element_type=F32)

                @pl.when(f == 0)
                def _():
                    o_ref[rows, :] = y + b2_ref[...]

                @pl.when(f != 0)
                def _():
                    o_ref[rows, :] += y

            @pl.when(jnp.logical_and(sb * sub >= valid, f == 0))
            def _(sb=sb):
                o_ref[pl.ds(sb * sub, sub), :] = jnp.zeros((sub, o_ref.shape[1]), o_ref.dtype)


def expert_ffn(tile_expert, tile_valid, tile_block, xs, w1, b1, w2, b2, tm, tf=256, sub=512):
    n_slots, d = xs.shape
    n_tiles = n_slots // tm
    nf = D_FF // tf
    b1r = b1.reshape(N_EXPERTS, 1, 2 * D_FF)
    b2r = b2.reshape(N_EXPERTS, 1, d)

    def feff(j, f, tv):
        return jnp.where(tv[j] > 0, f, nf - 1)

    grid_spec = pltpu.PrefetchScalarGridSpec(
        num_scalar_prefetch=3,
        grid=(n_tiles, nf),
        in_specs=[
            pl.BlockSpec((tm, d), lambda j, f, te, tv, tb: (tb[j], 0)),
            pl.BlockSpec((None, d, tf), lambda j, f, te, tv, tb: (te[j], 0, feff(j, f, tv))),
            pl.BlockSpec((None, d, tf), lambda j, f, te, tv, tb: (te[j], 0, nf + feff(j, f, tv))),
            pl.BlockSpec((None, 1, tf), lambda j, f, te, tv, tb: (te[j], 0, feff(j, f, tv))),
            pl.BlockSpec((None, 1, tf), lambda j, f, te, tv, tb: (te[j], 0, nf + feff(j, f, tv))),
            pl.BlockSpec((None, tf, d), lambda j, f, te, tv, tb: (te[j], feff(j, f, tv), 0)),
            pl.BlockSpec((None, 1, d), lambda j, f, te, tv, tb: (te[j], 0, 0)),
        ],
        out_specs=pl.BlockSpec((tm, d), lambda j, f, te, tv, tb: (j, 0)),
        scratch_shapes=[pltpu.VMEM((d, tf), BF16), pltpu.VMEM((d, tf), BF16), pltpu.VMEM((tf, d), BF16)],
    )
    return pl.pallas_call(
        functools.partial(_expert_kernel, sub=sub),
        grid_spec=grid_spec,
        out_shape=jax.ShapeDtypeStruct((n_slots, d), F32),
        compiler_params=_cparams(("arbitrary", "arbitrary")),
        name="moe_expert_ffn",
    )(tile_expert, tile_valid, tile_block, xs, w1, w1, b1r, b1r, w2, b2r)


def _combine_kernel(dest_ref, ys_hbm, x2_ref, meta_ref, gain_ref, o_ref, buf_ref, sem):
    t = pl.program_id(0)
    tt = x2_ref.shape[0]
    base = t * tt * TOP_K

    def row_copy(r, k):
        return pltpu.make_async_copy(ys_hbm.at[pl.ds(dest_ref[base + r * TOP_K + k], 1)],
                                     buf_ref.at[k, pl.ds(r, 1)], sem)

    def start(r, carry):
        for k in range(TOP_K):
            row_copy(r, k).start()
        return carry

    lax.fori_loop(0, tt, start, 0)

    def wait(r, carry):
        for k in range(TOP_K):
            row_copy(r, k).wait()
        return carry

    lax.fori_loop(0, tt, wait, 0)
    meta = meta_ref[...]
    y = x2_ref[...]
    for k in range(TOP_K):
        y = y + meta[:, 2 * TOP_K + k:2 * TOP_K + k + 1] * buf_ref[k]
    o_ref[...] = y * lax.rsqrt(jnp.mean(y * y, axis=-1, keepdims=True) + NORM_EPS) * gain_ref[...]


def combine_final(dest_flat, ys, x2, meta, final_gain, tt=128):
    n, d = x2.shape
    grid_spec = pltpu.PrefetchScalarGridSpec(
        num_scalar_prefetch=1,
        grid=(n // tt,),
        in_specs=[pl.BlockSpec(memory_space=pl.ANY),
                  pl.BlockSpec((tt, d), lambda t, dest: (t, 0)),
                  pl.BlockSpec((tt, LANES), lambda t, dest: (t, 0)),
                  pl.BlockSpec((1, d), lambda t, dest: (0, 0))],
        out_specs=pl.BlockSpec((tt, d), lambda t, dest: (t, 0)),
        scratch_shapes=[pltpu.VMEM((TOP_K, tt, d), F32), pltpu.SemaphoreType.DMA(())],
    )
    return pl.pallas_call(
        _combine_kernel,
        grid_spec=grid_spec,
        out_shape=jax.ShapeDtypeStruct((n, d), F32),
        compiler_params=_cparams(("arbitrary",)),
        name="moe_combine_final",
    )(dest_flat, ys, x2, meta, final_gain.reshape(1, d))


def moe_ffn_final(x2, xn, meta, counts, w1, b1, w2, b2, final_gain, tm=1024):
    n, d = x2.shape
    a = n * TOP_K
    n_tiles = a // tm + N_EXPERTS
    n_slots = n_tiles * tm
    idx = meta[:, 0:TOP_K].astype(jnp.int32)
    rank = meta[:, TOP_K:2 * TOP_K].astype(jnp.int32)
    cnt = counts[0, :N_EXPERTS].astype(jnp.int32)
    tiles_e = (cnt + tm - 1) // tm
    tile_end = jnp.cumsum(tiles_e)
    tile_start = tile_end - tiles_e
    used = tile_end[-1]
    dest = (tile_start * tm)[idx] + rank
    dest_flat = dest.reshape(a)
    tj = jnp.arange(n_tiles, dtype=jnp.int32)
    te = jnp.minimum(jnp.sum(tj[:, None] >= tile_end[None, :], axis=1), N_EXPERTS - 1).astype(jnp.int32)
    last_e = te[jnp.maximum(used - 1, 0)]
    tile_expert = jnp.where(tj < used, te, last_e).astype(jnp.int32)
    tile_valid = jnp.where(tj < used, jnp.clip(cnt[te] - (tj - tile_start[te]) * tm, 0, tm), 0).astype(jnp.int32)
    tile_block = jnp.minimum(tj, used - 1).astype(jnp.int32)

    slot_tok = build_slot_table(dest_flat, n_slots)
    tr = 256
    used_rows = (used * (tm // tr)).astype(jnp.int32).reshape(1)
    xs = gather_rows(slot_tok, used_rows, xn, n_slots, tr=tr)
    ys = expert_ffn(tile_expert, tile_valid, tile_block, xs, w1, b1, w2, b2, tm)
    return combine_final(dest_flat, ys, x2, meta, final_gain)


def kernel(x, norm_mix_gain, w_in, hgrn_lb_logits, hgrn_norm_gain, hy_conv_w, hy_conv_b, hy_filt_w1,
           hy_filt_b1, hy_filt_w2, hy_filt_b2, hy_filt_w3, hy_filt_b3, hy_filt_w4, hy_filt_freq, hy_bias,
           hy_norm_gain, w_out, norm_ffn_gain, w_router, b_router, moe_w1, moe_b1, moe_w2, moe_b2,
           final_norm_gain):
    b, l, d = x.shape
    n = b * l
    x2d = x.reshape(n, d)
    proj = in_projection(x2d, norm_mix_gain[0], w_in[0].astype(BF16))
    proj3 = proj.reshape(b, l, D_IN)
    hg = hgrn2_mixer(proj3, hgrn_lb_logits, hgrn_norm_gain[0])

    sym, flt = _dft_tables(l)
    tap_sum, tap_diff = hyena_filter_taps(l, hy_filt_w1[0], hy_filt_b1[0], hy_filt_w2[0], hy_filt_b2[0],
                                          hy_filt_w3[0], hy_filt_b3[0], hy_filt_w4[0], hy_filt_freq[0])
    spectra = hyena_filter_spectra(flt, jnp.stack([tap_sum, tap_diff]))
    hy = hyena_mixer(proj3, hy_conv_w[0], hy_conv_b[0], sym, spectra, hy_bias[0])

    x2, xn, meta, counts = outproj_router(hg.reshape(n, HG_WIDTH), hy.reshape(n, HY_WIDTH), x2d,
                                          w_out[0].astype(BF16), hy_norm_gain[0], norm_ffn_gain[0],
                                          w_router[0], b_router[0])
    out = moe_ffn_final(x2, xn, meta, counts, moe_w1[0], moe_b1[0], moe_w2[0], moe_b2[0], final_norm_gain)
    return out.reshape(b, l, d)
```

```python
import functools
import math

import numpy as np
import jax
import jax.numpy as jnp
from jax import lax
from jax.experimental import pallas as pl
from jax.experimental.pallas import tpu as pltpu

F32 = jnp.float32
BF16 = jnp.bfloat16

D_MODEL = 2048
HG_WIDTH = 1024
HG_HEAD_DIM = 128
HG_HEADS = HG_WIDTH // HG_HEAD_DIM
HY_WIDTH = 1024
HY_ORDER = 2
HY_EMB = 33
HY_BANDS = (HY_EMB - 1) // 2
HY_FILTER_HIDDEN = 64
HY_FAST_DECAY_PCT = 0.3
HY_SLOW_DECAY_PCT = 1.5
HY_DECAY_TARGET = 1e-2
N_EXPERTS = 32
TOP_K = 4
D_FF = D_MODEL
SWIGLU_ALPHA = 1.702
SWIGLU_LIMIT = 7.0
NORM_EPS = 1e-5
D_IN = 5 * HG_WIDTH + (HY_ORDER + 1) * HY_WIDTH

LANES = 128
GLA_CHUNK = 128
GLA_LEVELS = (64, 32, 16, 8, 4, 2, 1)
VMEM_LIMIT = 56 * 1024 * 1024


def _cparams(sem, vmem=VMEM_LIMIT):
    return pltpu.CompilerParams(dimension_semantics=sem, vmem_limit_bytes=vmem)


def _inproj_kernel(x_ref, g_ref, w_ref, o_ref):
    x = x_ref[...]
    ms = jnp.mean(x * x, axis=-1, keepdims=True)
    h = (x * lax.rsqrt(ms + NORM_EPS) * g_ref[...]).astype(BF16)
    o_ref[...] = jnp.dot(h, w_ref[...], preferred_element_type=F32)


def in_projection(x2d, gain, w_bf16, tm=512, tn=1024):
    n, d = x2d.shape
    dout = w_bf16.shape[1]
    return pl.pallas_call(
        _inproj_kernel,
        grid=(dout // tn, n // tm),
        in_specs=[
            pl.BlockSpec((tm, d), lambda j, i: (i, 0)),
            pl.BlockSpec((1, d), lambda j, i: (0, 0)),
            pl.BlockSpec((d, tn), lambda j, i: (0, j)),
        ],
        out_specs=pl.BlockSpec((tm, tn), lambda j, i: (i, j)),
        out_shape=jax.ShapeDtypeStruct((n, dout), F32),
        compiler_params=_cparams(("arbitrary", "arbitrary")),
        name="in_projection",
    )(x2d, gain.reshape(1, d), w_bf16)


def _gla_constants():
    c = GLA_CHUNK
    t = np.arange(c)[:, None]
    r = np.arange(c)[None, :]
    fwd = [r <= t, r > t]
    bwd = [r >= t, r < t]
    for m in GLA_LEVELS:
        pos = t % (2 * m)
        mid = t - pos + m
        second = pos >= m
        fwd.append(np.where(second, (r >= mid) & (r <= t), (r > t) & (r < mid)))
        bwd.append(np.where(second, (r >= mid) & (r < t), (r >= t) & (r < mid)))
    x = t ^ r
    lv = np.full((c, c), -1, np.int32)
    for j in range(int(math.log2(c))):
        lv = np.where((x >> j) == 1, j, lv)
    mf = jnp.asarray(np.concatenate(fwd, 0).astype(np.float32), BF16)
    mb = jnp.asarray(np.concatenate(bwd, 0).astype(np.float32), BF16)
    return mf, mb, jnp.asarray(lv, jnp.int32)


def _hgrn_kernel(q_ref, ff_ref, fb_ref, i_ref, g_ref, lb_ref, gain_ref, mf_ref, mb_ref, lv_ref,
                 o_ref, acc_ref, qb_ref, kb_ref, db_ref):
    c = GLA_CHUNK
    n_chunks = q_ref.shape[0] // c
    lb = lb_ref[...]
    l0, l1 = lb[0], lb[1]
    mx = jnp.maximum(l0, l1)
    e0 = jnp.exp(l0 - mx)
    e1 = jnp.exp(l1 - mx)
    p0 = e0 / (e0 + e1)
    lb_f = p0[0:1, :]
    lb_b = p0[1:2, :]
    lv = lv_ref[...]
    row = lax.broadcasted_iota(jnp.int32, (c, LANES), 0)
    nt = (((1,), (1,)), ((), ()))

    def gates(z, lower):
        f = lower + (1.0 - lower) * jax.nn.sigmoid(z)
        return 1.0 - f, jnp.log(f)

    def exponents(m_ref, g):
        g_hi = g.astype(BF16)
        g_lo = (g - g_hi.astype(F32)).astype(BF16)
        m = m_ref[...]
        return (jnp.dot(m, g_hi, preferred_element_type=F32)
                + jnp.dot(m, g_lo, preferred_element_type=F32))

    def fwd_body(ci, st):
        rows = pl.ds(pl.multiple_of(ci * c, c), c)
        qr = q_ref[rows, :]
        q = qr * jax.nn.sigmoid(qr)
        v = i_ref[rows, :]
        kf, gf = gates(ff_ref[rows, :], lb_f)
        kb, gb = gates(fb_ref[rows, :], lb_b)
        ef_all = exponents(mf_ref, gf)
        eb_all = exponents(mb_ref, gb)
        scores = jnp.zeros((c, c), F32)
        for li, m in enumerate(GLA_LEVELS):
            ef = jnp.exp(ef_all[(2 + li) * c:(3 + li) * c])
            eb = jnp.exp(eb_all[(2 + li) * c:(3 + li) * c])
            second = (row & m) != 0
            a = jnp.concatenate([jnp.where(second, q * ef, 0.0), jnp.where(second, 0.0, q * eb)], axis=1)
            b = jnp.concatenate([jnp.where(second, 0.0, kf * ef), jnp.where(second, kb * eb, 0.0)], axis=1)
            s = lax.dot_general(a.astype(BF16), b.astype(BF16), nt, preferred_element_type=F32)
            scores = jnp.where(lv == int(math.log2(m)), s, scores)
        v_bf = v.astype(BF16)
        o = jnp.dot(scores.astype(BF16), v_bf, preferred_element_type=F32)
        o = o + jnp.sum(q * (kf + kb), axis=-1, keepdims=True) * v
        b_inc = ef_all[0:c]
        q_dec = (q * jnp.exp(b_inc)).astype(BF16)
        o = o + lax.dot_general(q_dec, st.astype(BF16), nt, preferred_element_type=F32)
        k_dec = (kf * jnp.exp(ef_all[c:2 * c])).astype(BF16)
        vt_bf = v.T.astype(BF16)
        st = st * jnp.exp(b_inc[c - 1:c, :]) + jnp.dot(vt_bf, k_dec, preferred_element_type=F32)
        acc_ref[rows, :] = o
        bb = eb_all[0:c]
        qb_ref[rows, :] = (q * jnp.exp(bb)).astype(BF16)
        kb_ref[rows, :] = (kb * jnp.exp(eb_all[c:2 * c])).astype(BF16)
        db_ref[ci] = jnp.broadcast_to(jnp.exp(bb[0:1, :]), (8, LANES))
        return st

    lax.fori_loop(0, n_chunks, fwd_body, jnp.zeros((c, c), F32), unroll=2)

    gain = gain_ref[...]

    def bwd_body(i, st):
        ci = n_chunks - 1 - i
        rows = pl.ds(pl.multiple_of(ci * c, c), c)
        o = acc_ref[rows, :] + lax.dot_general(qb_ref[rows, :], st.astype(BF16), nt,
                                               preferred_element_type=F32)
        vt_bf = i_ref[rows, :].T.astype(BF16)
        st = st * db_ref[ci][0:1, :] + jnp.dot(vt_bf, kb_ref[rows, :], preferred_element_type=F32)
        o = o * lax.rsqrt(jnp.mean(o * o, axis=-1, keepdims=True) + NORM_EPS) * gain
        gr = g_ref[rows, :]
        o_ref[rows, :] = (o * (gr * jax.nn.sigmoid(gr))).astype(o_ref.dtype)
        return st

    lax.fori_loop(0, n_chunks, bwd_body, jnp.zeros((c, c), F32))


def hgrn2_mixer(proj3, lb_logits, norm_gain):
    b, l, _ = proj3.shape
    h, dh = HG_HEADS, HG_HEAD_DIM
    mf, mb, lv = _gla_constants()
    nblk = mf.shape[0]

    def col(off):
        return pl.BlockSpec((None, l, dh), lambda bi, hi, off=off: (bi, 0, off + hi))

    const2 = lambda bi, hi: (0, 0)
    return pl.pallas_call(
        _hgrn_kernel,
        grid=(b, h),
        in_specs=[col(0), col(h), col(2 * h), col(3 * h), col(4 * h),
                  pl.BlockSpec((2, 2, dh), lambda bi, hi: (0, 0, hi)),
                  pl.BlockSpec((1, dh), lambda bi, hi: (0, hi)),
                  pl.BlockSpec((nblk, GLA_CHUNK), const2),
                  pl.BlockSpec((nblk, GLA_CHUNK), const2),
                  pl.BlockSpec((GLA_CHUNK, GLA_CHUNK), const2)],
        out_specs=pl.BlockSpec((None, l, dh), lambda bi, hi: (bi, 0, hi)),
        out_shape=jax.ShapeDtypeStruct((b, l, HG_WIDTH), BF16),
        scratch_shapes=[pltpu.VMEM((l, dh), F32), pltpu.VMEM((l, dh), BF16), pltpu.VMEM((l, dh), BF16),
                        pltpu.VMEM((l // GLA_CHUNK, 8, LANES), F32)],
        compiler_params=_cparams(("arbitrary", "arbitrary")),
        name="hgrn2_mixer",
    )(proj3, proj3, proj3, proj3, proj3, lb_logits, norm_gain.reshape(1, HG_WIDTH), mf, mb, lv)


@functools.lru_cache(maxsize=None)
def _dft_tables(l):
    n = 2 * l
    k2 = 2 * np.arange(l, dtype=np.int64)[:, None] + 1
    m1 = np.arange(l, dtype=np.int64)[None, :]
    ang_s = ((k2 * (2 * m1 + 1)) % (4 * n)).astype(np.float64) * (2.0 * math.pi / (4 * n))
    ang_f = ((k2 * m1) % (2 * n)).astype(np.float64) * (2.0 * math.pi / (2 * n))
    sym = np.stack([np.cos(ang_s), np.sin(ang_s)]).astype(np.float32)
    flt = np.stack([np.cos(ang_f), np.sin(ang_f)]).astype(np.float32)
    return sym, flt


def _filter_features(l):
    pos = jnp.arange(l, dtype=F32)
    t = pos / max(l - 1, 1)
    bands = jnp.linspace(1e-4, HY_BANDS - 1, HY_BANDS, dtype=F32)
    ang = (2.0 * math.pi / l) * pos[:, None] * bands[None, :]
    z = jnp.concatenate([t[:, None], jnp.cos(ang), -jnp.sin(ang)], axis=-1)
    z = jnp.pad(z, ((0, 0), (0, LANES - HY_EMB)))
    min_decay = math.log(HY_DECAY_TARGET) / HY_FAST_DECAY_PCT
    max_decay = math.log(HY_DECAY_TARGET) / HY_SLOW_DECAY_PCT
    deltas = jnp.abs(jnp.linspace(min_decay, max_decay, HY_WIDTH, dtype=F32))
    window = jnp.exp(-t[:, None] * deltas[None, :])
    return z, window


def _filter_kernel(z_ref, win_ref, w1_ref, b1_ref, w2_ref, b2_ref, w3_ref, b3_ref, w4_ref, fr_ref,
                   sum_ref, diff_ref):
    hp = lax.Precision.HIGHEST
    fr = fr_ref[...]
    h = jnp.sin(fr * (jnp.dot(z_ref[...], w1_ref[...], precision=hp, preferred_element_type=F32) + b1_ref[...]))
    h = jnp.sin(fr * (jnp.dot(h, w2_ref[...], precision=hp, preferred_element_type=F32) + b2_ref[...]))
    h = jnp.sin(fr * (jnp.dot(h, w3_ref[...], precision=hp, preferred_element_type=F32) + b3_ref[...]))
    h = jnp.dot(h, w4_ref[...], precision=hp, preferred_element_type=F32)
    win = win_ref[...]
    tl = h.shape[0]
    lag = pl.program_id(0) * tl + lax.broadcasted_iota(jnp.int32, (tl, HY_WIDTH), 0)
    for o in range(HY_ORDER):
        hf = h[:, (2 * o) * HY_WIDTH:(2 * o + 1) * HY_WIDTH] * win
        hb = h[:, (2 * o + 1) * HY_WIDTH:(2 * o + 2) * HY_WIDTH] * win
        hb = jnp.where(lag == 0, 0.0, hb)
        sum_ref[:, o * HY_WIDTH:(o + 1) * HY_WIDTH] = (hf + hb).astype(sum_ref.dtype)
        diff_ref[:, o * HY_WIDTH:(o + 1) * HY_WIDTH] = (hf - hb).astype(diff_ref.dtype)


def hyena_filter_taps(l, w1, b1, w2, b2, w3, b3, w4, freq, tl=256):
    z, window = _filter_features(l)
    w1p = jnp.pad(w1, ((0, LANES - HY_EMB), (0, 0)))
    hid = HY_FILTER_HIDDEN
    full = lambda shape: pl.BlockSpec(shape, lambda i: (0,) * len(shape))
    nout = HY_ORDER * HY_WIDTH
    return pl.pallas_call(
        _filter_kernel,
        grid=(l // tl,),
        in_specs=[pl.BlockSpec((tl, LANES), lambda i: (i, 0)),
                  pl.BlockSpec((tl, HY_WIDTH), lambda i: (i, 0)),
                  full((LANES, hid)), full((1, hid)), full((hid, hid)), full((1, hid)),
                  full((hid, hid)), full((1, hid)), full((hid, 2 * nout)), full((1, hid))],
        out_specs=[pl.BlockSpec((tl, nout), lambda i: (i, 0)), pl.BlockSpec((tl, nout), lambda i: (i, 0))],
        out_shape=[jax.ShapeDtypeStruct((l, nout), BF16), jax.ShapeDtypeStruct((l, nout), BF16)],
        compiler_params=_cparams(("arbitrary",)),
        name="hyena_filter_taps",
    )(z, window, w1p, b1.reshape(1, hid), w2, b2.reshape(1, hid), w3, b3.reshape(1, hid), w4,
      freq.reshape(1, hid))


def _spectrum_kernel(f_ref, h_ref, o_ref):
    o_ref[...] = jnp.dot(f_ref[...], h_ref[...], preferred_element_type=F32).astype(o_ref.dtype)


def hyena_filter_spectra(flt, taps, tn=512):
    _, l, nout = taps.shape
    return pl.pallas_call(
        _spectrum_kernel,
        grid=(2, nout // tn),
        in_specs=[pl.BlockSpec((None, l, l), lambda p, j: (p, 0, 0)),
                  pl.BlockSpec((None, l, tn), lambda p, j: (p, 0, j))],
        out_specs=pl.BlockSpec((None, l, tn), lambda p, j: (p, 0, j)),
        out_shape=jax.ShapeDtypeStruct((2, l, nout), BF16),
        compiler_params=_cparams(("arbitrary", "arbitrary")),
        name="hyena_filter_spectra",
    )(flt, taps)


def _hyena_kernel(xv_ref, x1_ref, x2_ref, cw_ref, cb_ref, sym_ref, sp0_ref, sp1_ref, bias_ref, o_ref,
                  z_s, g_s, zb_s, pa_s, pb_s, *, rb):
    l, tc = xv_ref.shape
    scale = 2.0 / (2 * l)
    first = lax.broadcasted_iota(jnp.int32, (l, tc), 0) == 0
    last = lax.broadcasted_iota(jnp.int32, (l, tc), 0) == l - 1

    def short_conv(x_ref, part):
        x = x_ref[...]
        w = cw_ref[part]
        prev = jnp.where(first, 0.0, pltpu.roll(x, 1, 0))
        nxt = jnp.where(last, 0.0, pltpu.roll(x, l - 1, 0))
        return prev * w[0:1, :] + x * w[1:2, :] + nxt * w[2:3, :] + cb_ref[part]

    z_s[...] = short_conv(xv_ref, 0)
    for o, (gate_ref, part, sp_ref) in enumerate(((x1_ref, 1, sp0_ref), (x2_ref, 2, sp1_ref))):
        zb_s[...] = z_s[...].astype(BF16)
        g_s[...] = short_conv(gate_ref, part)
        for r0 in range(0, l, rb):
            a = jnp.dot(sym_ref[0, r0:r0 + rb, :], zb_s[...], preferred_element_type=F32)
            b = jnp.dot(sym_ref[1, r0:r0 + rb, :], zb_s[...], preferred_element_type=F32)
            hr = sp_ref[0, r0:r0 + rb, :].astype(F32)
            hs = sp_ref[1, r0:r0 + rb, :].astype(F32)
            pa_s[r0:r0 + rb, :] = (a * hr - b * hs).astype(BF16)
            pb_s[r0:r0 + rb, :] = (a * hs + b * hr).astype(BF16)
        for r0 in range(0, l, rb):
            y = (jnp.dot(sym_ref[0, r0:r0 + rb, :], pa_s[...], preferred_element_type=F32)
                 + jnp.dot(sym_ref[1, r0:r0 + rb, :], pb_s[...], preferred_element_type=F32)) * scale
            y = y + z_s[r0:r0 + rb, :] * bias_ref[o]
            z_s[r0:r0 + rb, :] = g_s[r0:r0 + rb, :] * y
    o_ref[...] = z_s[...].astype(o_ref.dtype)


def hyena_mixer(proj3, conv_w, conv_b, sym, spectra, hy_bias, tc=256, rb=1024):
    b, l, _ = proj3.shape
    nct = HY_WIDTH // tc
    base = (5 * HG_WIDTH) // tc

    def col(part):
        return pl.BlockSpec((None, l, tc), lambda ci, bi, part=part: (bi, 0, base + part * nct + ci))

    def spec_cols(o):
        return pl.BlockSpec((2, l, tc), lambda ci, bi, o=o: (0, 0, o * nct + ci))

    cw = conv_w.reshape(3, HY_ORDER + 1, HY_WIDTH).transpose(1, 0, 2)
    cb = conv_b.reshape(HY_ORDER + 1, 1, HY_WIDTH)
    return pl.pallas_call(
        functools.partial(_hyena_kernel, rb=rb),
        grid=(nct, b),
        in_specs=[col(0), col(1), col(2),
                  pl.BlockSpec((HY_ORDER + 1, 3, tc), lambda ci, bi: (0, 0, ci)),
                  pl.BlockSpec((HY_ORDER + 1, 1, tc), lambda ci, bi: (0, 0, ci)),
                  pl.BlockSpec((2, l, l), lambda ci, bi: (0, 0, 0), pipeline_mode=pl.Buffered(1)),
                  spec_cols(0), spec_cols(1),
                  pl.BlockSpec((HY_ORDER, 1, tc), lambda ci, bi: (0, 0, ci))],
        out_specs=pl.BlockSpec((None, l, tc), lambda ci, bi: (bi, 0, ci)),
        out_shape=jax.ShapeDtypeStruct((b, l, HY_WIDTH), BF16),
        scratch_shapes=[pltpu.VMEM((l, tc), F32), pltpu.VMEM((l, tc), F32), pltpu.VMEM((l, tc), BF16),
                        pltpu.VMEM((l, tc), BF16), pltpu.VMEM((l, tc), BF16)],
        compiler_params=_cparams(("arbitrary", "arbitrary")),
        name="hyena_mixer",
    )(proj3, proj3, proj3, cw, cb, sym, spectra, spectra, hy_bias.reshape(HY_ORDER, 1, HY_WIDTH))


def _outproj_router_kernel(hg_ref, hy_ref, x_ref, w_ref, hyg_ref, fg_ref, wr_ref, br_ref, tri_ref,
                           x2_ref, xn_ref, meta_ref, cnt_ref, carry_ref):
    @pl.when(pl.program_id(0) == 0)
    def _():
        carry_ref[...] = jnp.zeros_like(carry_ref)

    hy = hy_ref[...].astype(F32)
    hy = hy * lax.rsqrt(jnp.mean(hy * hy, axis=-1, keepdims=True) + NORM_EPS) * hyg_ref[...]
    m = (jnp.dot(hg_ref[...], w_ref[0:HG_WIDTH, :], preferred_element_type=F32)
         + jnp.dot(hy.astype(BF16), w_ref[HG_WIDTH:, :], preferred_element_type=F32))
    x2 = x_ref[...] + m
    x2_ref[...] = x2
    xn = x2 * lax.rsqrt(jnp.mean(x2 * x2, axis=-1, keepdims=True) + NORM_EPS) * fg_ref[...]
    xn_ref[...] = xn
    logits = jnp.dot(xn, wr_ref[...], precision=lax.Precision.HIGHEST,
                     preferred_element_type=F32) + br_ref[...]
    tm = logits.shape[0]
    lane = lax.broadcasted_iota(jnp.int32, (tm, LANES), 1).astype(F32)
    neg = jnp.float32(-jnp.inf)
    logits = jnp.where(lane < N_EXPERTS, logits, neg)
    tops, idxs = [], []
    for _ in range(TOP_K):
        mx = jnp.max(logits, axis=-1, keepdims=True)
        idx = jnp.min(jnp.where(logits == mx, lane, float(LANES)), axis=-1, keepdims=True)
        tops.append(mx)
        idxs.append(idx)
        logits = jnp.where(lane == idx, neg, logits)
    exps = [jnp.exp(t - tops[0]) for t in tops]
    denom = exps[0] + exps[1] + exps[2] + exps[3]
    onehot = jnp.zeros((tm, LANES), F32)
    for idx in idxs:
        onehot = onehot + jnp.where(lane == idx, 1.0, 0.0)
    cum = jnp.dot(tri_ref[...], onehot.astype(BF16), preferred_element_type=F32) + carry_ref[0:1, :]
    meta = jnp.zeros((tm, LANES), F32)
    for k in range(TOP_K):
        rank = jnp.sum(jnp.where(lane == idxs[k], cum, 0.0), axis=-1, keepdims=True)
        meta = jnp.where(lane == k, idxs[k], meta)
        meta = jnp.where(lane == TOP_K + k, rank, meta)
        meta = jnp.where(lane == 2 * TOP_K + k, exps[k] / denom, meta)
    meta_ref[...] = meta
    carry = carry_ref[...] + jnp.sum(onehot, axis=0, keepdims=True)
    carry_ref[...] = carry
    cnt_ref[...] = carry


def outproj_router(hg2d, hy2d, x2d, w_out_bf16, hy_gain, ffn_gain, w_router, b_router, tm=256):
    n, d = x2d.shape
    wr = jnp.pad(w_router, ((0, 0), (0, LANES - N_EXPERTS)))
    br = jnp.pad(b_router, (0, LANES - N_EXPERTS)).reshape(1, LANES)
    tri = jnp.asarray(np.tril(np.ones((tm, tm), np.float32), -1), BF16)
    row = lambda w: pl.BlockSpec((tm, w), lambda i: (i, 0))
    full = lambda shape, **kw: pl.BlockSpec(shape, lambda i: (0,) * len(shape), **kw)
    return pl.pallas_call(
        _outproj_router_kernel,
        grid=(n // tm,),
        in_specs=[row(HG_WIDTH), row(HY_WIDTH), row(d),
                  full((d, d), pipeline_mode=pl.Buffered(1)),
                  full((1, HY_WIDTH)), full((1, d)), full((d, LANES)), full((1, LANES)), full((tm, tm))],
        out_specs=[row(d), row(d), row(LANES), full((8, LANES))],
        out_shape=[jax.ShapeDtypeStruct((n, d), F32), jax.ShapeDtypeStruct((n, d), F32),
                   jax.ShapeDtypeStruct((n, LANES), F32), jax.ShapeDtypeStruct((8, LANES), F32)],
        scratch_shapes=[pltpu.VMEM((8, LANES), F32)],
        compiler_params=_cparams(("arbitrary",)),
        name="outproj_router",
    )(hg2d, hy2d, x2d, w_out_bf16, hy_gain.reshape(1, HY_WIDTH), ffn_gain.reshape(1, d), wr, br, tri)


def _slot_table_kernel(dest_ref, slot_ref):
    n_slots = slot_ref.shape[0]
    n_pairs = dest_ref.shape[0]

    def fill(i, carry):
        slot_ref[i] = 0
        return carry

    lax.fori_loop(0, n_slots, fill, 0, unroll=16)

    def scatter(a, carry):
        slot_ref[dest_ref[a]] = lax.shift_right_logical(a, int(math.log2(TOP_K)))
        return carry

    lax.fori_loop(0, n_pairs, scatter, 0, unroll=16)


def build_slot_table(dest_flat, n_slots):
    return pl.pallas_call(
        _slot_table_kernel,
        in_specs=[pl.BlockSpec(memory_space=pltpu.SMEM)],
        out_specs=pl.BlockSpec(memory_space=pltpu.SMEM),
        out_shape=jax.ShapeDtypeStruct((n_slots,), jnp.int32),
        name="build_slot_table",
    )(dest_flat)


GATHER_UNROLL = 8


def _row_gather_kernel(slot_ref, rows_ref, used_ref, x_hbm, o_ref, buf_ref, sems):
    t = pl.program_id(0)
    tr = buf_ref.shape[1]
    used = used_ref[0]

    def n_copies(tile):
        return ((rows_ref[tile] + GATHER_UNROLL - 1) // GATHER_UNROLL) * GATHER_UNROLL

    def issue(tile):
        b = tile % 2
        base = tile * tr

        def body(i, carry):
            for u in range(GATHER_UNROLL):
                r = i * GATHER_UNROLL + u
                pltpu.make_async_copy(x_hbm.at[pl.ds(slot_ref[base + r], 1)], buf_ref.at[b, pl.ds(r, 1)],
                                      sems.at[b]).start()
            return carry

        lax.fori_loop(0, n_copies(tile) // GATHER_UNROLL, body, 0)

    @pl.when(t == 0)
    def _():
        buf_ref[...] = jnp.zeros(buf_ref.shape, buf_ref.dtype)
        issue(t)

    @pl.when(t + 1 < used)
    def _():
        issue(t + 1)

    @pl.when(t < used)
    def _():
        b = t % 2
        nc = n_copies(t)

        @pl.when(nc > 0)
        def _():
            pltpu.make_async_copy(x_hbm.at[pl.ds(0, nc)], buf_ref.at[b, pl.ds(0, nc)], sems.at[b]).wait()

        row = lax.broadcasted_iota(jnp.int32, o_ref.shape, 0)
        o_ref[...] = jnp.where(row < rows_ref[t], buf_ref[b], 0.0).astype(o_ref.dtype)


def gather_rows(slot_tok, tile_rows, used_tiles, xn, n_slots, tr):
    n, d = xn.shape
    grid_spec = pltpu.PrefetchScalarGridSpec(
        num_scalar_prefetch=3,
        grid=(n_slots // tr,),
        in_specs=[pl.BlockSpec(memory_space=pl.ANY)],
        out_specs=pl.BlockSpec((tr, d), lambda t, slot, rows, used: (jnp.minimum(t, used[0] - 1), 0)),
        scratch_shapes=[pltpu.VMEM((2, tr, d), F32), pltpu.SemaphoreType.DMA((2,))],
    )
    return pl.pallas_call(
        _row_gather_kernel,
        grid_spec=grid_spec,
        out_shape=jax.ShapeDtypeStruct((n_slots, d), BF16),
        compiler_params=pltpu.CompilerParams(dimension_semantics=("arbitrary",), vmem_limit_bytes=VMEM_LIMIT,
                                             disable_bounds_checks=True),
        name="moe_gather_rows",
    )(slot_tok, tile_rows, used_tiles, xn)


def _expert_kernel(te_ref, tv_ref, tb_ref, x_ref, wg_ref, wl_ref, bg_ref, bl_ref, w2_ref, b2_ref, o_ref,
                   act_s, wg_s, wl_s, w2_s, *, nf, blocks):
    j = pl.program_id(0)
    s = pl.program_id(1)
    valid = tv_ref[j]
    tf = wg_ref.shape[1]

    @pl.when(jnp.logical_and(valid > 0, s < nf))
    def _():
        wg_s[...] = wg_ref[...].astype(BF16)
        wl_s[...] = wl_ref[...].astype(BF16)
        for r0, rn in blocks:
            @pl.when(r0 < valid)
            def _(r0=r0, rn=rn):
                x = x_ref[r0:r0 + rn, :]
                glu = jnp.dot(x, wg_s[...], preferred_element_type=F32) + bg_ref[...]
                lin = jnp.dot(x, wl_s[...], preferred_element_type=F32) + bl_ref[...]
                glu = jnp.minimum(glu, SWIGLU_LIMIT)
                lin = jnp.clip(lin, -SWIGLU_LIMIT, SWIGLU_LIMIT)
                act = glu * jax.nn.sigmoid(SWIGLU_ALPHA * glu) * (lin + 1.0)
                act_s[s, r0:r0 + rn, :] = act.astype(BF16)

    @pl.when(jnp.logical_and(valid > 0, s >= nf))
    def _():
        w2_s[...] = w2_ref[...].astype(BF16)
        for r0, rn in blocks:
            @pl.when(r0 < valid)
            def _(r0=r0, rn=rn):
                y = b2_ref[...] + jnp.dot(act_s[0, r0:r0 + rn, :], w2_s[0:tf, :], preferred_element_type=F32)
                for f in range(1, nf):
                    y = y + jnp.dot(act_s[f, r0:r0 + rn, :], w2_s[f * tf:(f + 1) * tf, :],
                                    preferred_element_type=F32)
                o_ref[r0:r0 + rn, :] = y

            @pl.when(r0 >= valid)
            def _(r0=r0, rn=rn):
                o_ref[r0:r0 + rn, :] = jnp.zeros((rn, o_ref.shape[1]), o_ref.dtype)


def expert_ffn(tile_expert, tile_valid, tile_block, xs, w1, b1, w2, b2, tm, blocks, tf=256):
    n_slots, d = xs.shape
    n_tiles = n_slots // tm
    nf = D_FF // tf
    nn = d // tf
    assert nn == nf
    b1r = b1.reshape(N_EXPERTS, 1, 2 * D_FF)
    b2r = b2.reshape(N_EXPERTS, 1, d)

    def w1_blk(j, s, tv):
        return jnp.where(tv[j] > 0, jnp.minimum(s, nf - 1), nf - 1)

    def w2_blk(j, s, tv):
        return jnp.where(tv[j] > 0, jnp.maximum(s - nf, 0), nn - 1)

    grid_spec = pltpu.PrefetchScalarGridSpec(
        num_scalar_prefetch=3,
        grid=(n_tiles, nf + nn),
        in_specs=[
            pl.BlockSpec((tm, d), lambda j, s, te, tv, tb: (tb[j], 0)),
            pl.BlockSpec((None, d, tf), lambda j, s, te, tv, tb: (te[j], 0, w1_blk(j, s, tv))),
            pl.BlockSpec((None, d, tf), lambda j, s, te, tv, tb: (te[j], 0, nf + w1_blk(j, s, tv))),
            pl.BlockSpec((None, 1, tf), lambda j, s, te, tv, tb: (te[j], 0, w1_blk(j, s, tv))),
            pl.BlockSpec((None, 1, tf), lambda j, s, te, tv, tb: (te[j], 0, nf + w1_blk(j, s, tv))),
            pl.BlockSpec((None, D_FF, tf), lambda j, s, te, tv, tb: (te[j], 0, w2_blk(j, s, tv))),
            pl.BlockSpec((None, 1, tf), lambda j, s, te, tv, tb: (te[j], 0, w2_blk(j, s, tv))),
        ],
        out_specs=pl.BlockSpec((tm, tf), lambda j, s, te, tv, tb: (tb[j], w2_blk(j, s, tv))),
        scratch_shapes=[pltpu.VMEM((nf, tm, tf), BF16), pltpu.VMEM((d, tf), BF16), pltpu.VMEM((d, tf), BF16),
                        pltpu.VMEM((D_FF, tf), BF16)],
    )
    return pl.pallas_call(
        functools.partial(_expert_kernel, nf=nf, blocks=blocks),
        grid_spec=grid_spec,
        out_shape=jax.ShapeDtypeStruct((n_slots, d), F32),
        compiler_params=_cparams(("arbitrary", "arbitrary")),
        name="moe_expert_ffn",
    )(tile_expert, tile_valid, tile_block, xs, w1, w1, b1r, b1r, w2, b2r)


def _combine_kernel(dest_ref, ys_hbm, x2_ref, meta_ref, gain_ref, o_ref, buf_ref, sems):
    t = pl.program_id(0)
    nt = pl.num_programs(0)
    tt = x2_ref.shape[0]

    def issue(tile):
        b = tile % 2
        base = tile * (tt * TOP_K)

        def body(r, carry):
            for k in range(TOP_K):
                pltpu.make_async_copy(ys_hbm.at[pl.ds(dest_ref[base + r * TOP_K + k], 1)],
                                      buf_ref.at[b, k, pl.ds(r, 1)], sems.at[b]).start()
            return carry

        lax.fori_loop(0, tt, body, 0, unroll=4)

    @pl.when(t == 0)
    def _():
        issue(t)

    @pl.when(t + 1 < nt)
    def _():
        issue(t + 1)

    b = t % 2
    for k in range(TOP_K):
        pltpu.make_async_copy(ys_hbm.at[pl.ds(0, tt)], buf_ref.at[b, k], sems.at[b]).wait()
    meta = meta_ref[...]
    y = x2_ref[...]
    for k in range(TOP_K):
        y = y + meta[:, 2 * TOP_K + k:2 * TOP_K + k + 1] * buf_ref[b, k]
    o_ref[...] = y * lax.rsqrt(jnp.mean(y * y, axis=-1, keepdims=True) + NORM_EPS) * gain_ref[...]


def combine_final(dest_flat, ys, x2, meta, final_gain, tt=128):
    n, d = x2.shape
    grid_spec = pltpu.PrefetchScalarGridSpec(
        num_scalar_prefetch=1,
        grid=(n // tt,),
        in_specs=[pl.BlockSpec(memory_space=pl.ANY),
                  pl.BlockSpec((tt, d), lambda t, dest: (t, 0)),
                  pl.BlockSpec((tt, LANES), lambda t, dest: (t, 0)),
                  pl.BlockSpec((1, d), lambda t, dest: (0, 0))],
        out_specs=pl.BlockSpec((tt, d), lambda t, dest: (t, 0)),
        scratch_shapes=[pltpu.VMEM((2, TOP_K, tt, d), F32), pltpu.SemaphoreType.DMA((2,))],
    )
    return pl.pallas_call(
        _combine_kernel,
        grid_spec=grid_spec,
        out_shape=jax.ShapeDtypeStruct((n, d), F32),
        compiler_params=pltpu.CompilerParams(dimension_semantics=("arbitrary",), vmem_limit_bytes=VMEM_LIMIT,
                                             disable_bounds_checks=True),
        name="moe_combine_final",
    )(dest_flat, ys, x2, meta, final_gain.reshape(1, d))


MOE_TILE = 1280
MOE_ROW_BLOCKS = ((0, 512), (512, 512), (1024, 256))
MOE_GATHER_TILE = 256


def moe_ffn_final(x2, xn, meta, counts, w1, b1, w2, b2, final_gain, tm=MOE_TILE):
    n, d = x2.shape
    a = n * TOP_K
    n_tiles = -(-a // tm) + N_EXPERTS
    n_slots = n_tiles * tm
    idx = meta[:, 0:TOP_K].astype(jnp.int32)
    rank = meta[:, TOP_K:2 * TOP_K].astype(jnp.int32)
    cnt = counts[0, :N_EXPERTS].astype(jnp.int32)
    tiles_e = (cnt + tm - 1) // tm
    tile_end = jnp.cumsum(tiles_e)
    tile_start = tile_end - tiles_e
    used = tile_end[-1]
    dest = (tile_start * tm)[idx] + rank
    dest_flat = dest.reshape(a)
    tj = jnp.arange(n_tiles, dtype=jnp.int32)
    te = jnp.minimum(jnp.sum(tj[:, None] >= tile_end[None, :], axis=1), N_EXPERTS - 1).astype(jnp.int32)
    last_e = te[jnp.maximum(used - 1, 0)]
    tile_expert = jnp.where(tj < used, te, last_e).astype(jnp.int32)
    tile_valid = jnp.where(tj < used, jnp.clip(cnt[te] - (tj - tile_start[te]) * tm, 0, tm), 0).astype(jnp.int32)
    tile_block = jnp.maximum(jnp.minimum(tj, used - 1), 0).astype(jnp.int32)

    slot_tok = build_slot_table(dest_flat, n_slots)
    tr = MOE_GATHER_TILE
    per = tm // tr
    gj = jnp.arange(n_tiles * per, dtype=jnp.int32)
    gather_rows_valid = jnp.clip(tile_valid[gj // per] - (gj % per) * tr, 0, tr).astype(jnp.int32)
    used_gather = jnp.maximum(used * per, 1).astype(jnp.int32).reshape(1)
    xs = gather_rows(slot_tok, gather_rows_valid, used_gather, xn, n_slots, tr)
    ys = expert_ffn(tile_expert, tile_valid, tile_block, xs, w1, b1, w2, b2, tm, MOE_ROW_BLOCKS)
    return combine_final(dest_flat, ys, x2, meta, final_gain)


def kernel(x, norm_mix_gain, w_in, hgrn_lb_logits, hgrn_norm_gain, hy_conv_w, hy_conv_b, hy_filt_w1,
           hy_filt_b1, hy_filt_w2, hy_filt_b2, hy_filt_w3, hy_filt_b3, hy_filt_w4, hy_filt_freq, hy_bias,
           hy_norm_gain, w_out, norm_ffn_gain, w_router, b_router, moe_w1, moe_b1, moe_w2, moe_b2,
           final_norm_gain):
    b, l, d = x.shape
    n = b * l
    x2d = x.reshape(n, d)
    proj = in_projection(x2d, norm_mix_gain[0], w_in[0].astype(BF16))
    proj3 = proj.reshape(b, l, D_IN)
    hg = hgrn2_mixer(proj3, hgrn_lb_logits, hgrn_norm_gain[0])

    sym, flt = (jnp.asarray(tab).astype(BF16) for tab in _dft_tables(l))
    tap_sum, tap_diff = hyena_filter_taps(l, hy_filt_w1[0], hy_filt_b1[0], hy_filt_w2[0], hy_filt_b2[0],
                                          hy_filt_w3[0], hy_filt_b3[0], hy_filt_w4[0], hy_filt_freq[0])
    spectra = hyena_filter_spectra(flt, jnp.stack([tap_sum, tap_diff]))
    hy = hyena_mixer(proj3, hy_conv_w[0], hy_conv_b[0], sym, spectra, hy_bias[0])

    x2, xn, meta, counts = outproj_router(hg.reshape(n, HG_WIDTH), hy.reshape(n, HY_WIDTH), x2d,
                                          w_out[0].astype(BF16), hy_norm_gain[0], norm_ffn_gain[0],
                                          w_router[0], b_router[0])
    out = moe_ffn_final(x2, xn, meta, counts, moe_w1[0], moe_b1[0], moe_w2[0], moe_b2[0], final_norm_gain)
    return out.reshape(b, l, d)
```

```python
import functools
import math

import numpy as np
import jax
import jax.numpy as jnp
from jax import lax
from jax.experimental import pallas as pl
from jax.experimental.pallas import tpu as pltpu

F32 = jnp.float32
BF16 = jnp.bfloat16

D_MODEL = 2048
HG_WIDTH = 1024
HG_HEAD_DIM = 128
HG_HEADS = HG_WIDTH // HG_HEAD_DIM
HY_WIDTH = 1024
HY_ORDER = 2
HY_EMB = 33
HY_BANDS = (HY_EMB - 1) // 2
HY_FILTER_HIDDEN = 64
HY_FAST_DECAY_PCT = 0.3
HY_SLOW_DECAY_PCT = 1.5
HY_DECAY_TARGET = 1e-2
N_EXPERTS = 32
TOP_K = 4
D_FF = D_MODEL
SWIGLU_ALPHA = 1.702
SWIGLU_LIMIT = 7.0
NORM_EPS = 1e-5
D_IN = 5 * HG_WIDTH + (HY_ORDER + 1) * HY_WIDTH

LANES = 128
GLA_CHUNK = 128
GLA_LEVELS = (64, 32, 16, 8, 4, 2, 1)
VMEM_LIMIT = 56 * 1024 * 1024


def _cparams(sem, vmem=VMEM_LIMIT):
    return pltpu.CompilerParams(dimension_semantics=sem, vmem_limit_bytes=vmem)


def _inproj_kernel(x_ref, g_ref, w_ref, o_ref, h_s):
    @pl.when(pl.program_id(1) == 0)
    def _():
        x = x_ref[...]
        ms = jnp.mean(x * x, axis=-1, keepdims=True)
        h_s[...] = (x * lax.rsqrt(ms + NORM_EPS) * g_ref[...]).astype(BF16)

    o_ref[...] = jnp.dot(h_s[...], w_ref[...], preferred_element_type=F32)


def in_projection(x2d, gain, w_bf16, tm=1024, tn=1024):
    n, d = x2d.shape
    dout = w_bf16.shape[1]
    return pl.pallas_call(
        _inproj_kernel,
        grid=(n // tm, dout // tn),
        in_specs=[
            pl.BlockSpec((tm, d), lambda i, j: (i, 0)),
            pl.BlockSpec((1, d), lambda i, j: (0, 0)),
            pl.BlockSpec((d, tn), lambda i, j: (0, j)),
        ],
        out_specs=pl.BlockSpec((tm, tn), lambda i, j: (i, j)),
        out_shape=jax.ShapeDtypeStruct((n, dout), F32),
        scratch_shapes=[pltpu.VMEM((tm, d), BF16)],
        compiler_params=_cparams(("arbitrary", "arbitrary")),
        name="in_projection",
    )(x2d, gain.reshape(1, d), w_bf16)


def _gla_constants():
    c = GLA_CHUNK
    t = np.arange(c)[:, None]
    r = np.arange(c)[None, :]
    fwd = [r <= t, r > t]
    bwd = [r >= t, r < t]
    for m in GLA_LEVELS:
        pos = t % (2 * m)
        mid = t - pos + m
        second = pos >= m
        fwd.append(np.where(second, (r >= mid) & (r <= t), (r > t) & (r < mid)))
        bwd.append(np.where(second, (r >= mid) & (r < t), (r >= t) & (r < mid)))
    x = t ^ r
    lv = np.full((c, c), -1, np.int32)
    for j in range(int(math.log2(c))):
        lv = np.where((x >> j) == 1, j, lv)
    mf = jnp.asarray(np.concatenate(fwd, 0).astype(np.float32), BF16)
    mb = jnp.asarray(np.concatenate(bwd, 0).astype(np.float32), BF16)
    return mf, mb, jnp.asarray(lv, jnp.int32)


def _hgrn_kernel(q_ref, ff_ref, fb_ref, i_ref, g_ref, lb_ref, gain_ref, mf_ref, mb_ref, lv_ref,
                 o_ref, acc_ref, qb_ref, kb_ref, db_ref):
    c = GLA_CHUNK
    n_chunks = q_ref.shape[0] // c
    lb = lb_ref[...]
    l0, l1 = lb[0], lb[1]
    mx = jnp.maximum(l0, l1)
    e0 = jnp.exp(l0 - mx)
    e1 = jnp.exp(l1 - mx)
    p0 = e0 / (e0 + e1)
    lb_f = p0[0:1, :]
    lb_b = p0[1:2, :]
    lv = lv_ref[...]
    row = lax.broadcasted_iota(jnp.int32, (c, LANES), 0)
    nt = (((1,), (1,)), ((), ()))

    def gates(z, lower):
        f = lower + (1.0 - lower) * jax.nn.sigmoid(z)
        return 1.0 - f, jnp.log(f)

    def exponents(m_ref, g):
        g_hi = g.astype(BF16)
        g_lo = (g - g_hi.astype(F32)).astype(BF16)
        m = m_ref[...]
        return (jnp.dot(m, g_hi, preferred_element_type=F32)
                + jnp.dot(m, g_lo, preferred_element_type=F32))

    def fwd_body(ci, st):
        rows = pl.ds(pl.multiple_of(ci * c, c), c)
        qr = q_ref[rows, :]
        q = qr * jax.nn.sigmoid(qr)
        v = i_ref[rows, :]
        kf, gf = gates(ff_ref[rows, :], lb_f)
        kb, gb = gates(fb_ref[rows, :], lb_b)
        ef_all = exponents(mf_ref, gf)
        eb_all = exponents(mb_ref, gb)
        scores = jnp.zeros((c, c), F32)
        for li, m in enumerate(GLA_LEVELS):
            ef = ef_all[(2 + li) * c:(3 + li) * c]
            eb = eb_all[(2 + li) * c:(3 + li) * c]
            second = (row & m) != 0
            a = q * jnp.exp(jnp.where(second, ef, eb))
            b = jnp.where(second, kb, kf) * jnp.exp(jnp.where(second, eb, ef))
            s = lax.dot_general(a.astype(BF16), b.astype(BF16), nt, preferred_element_type=F32)
            scores = jnp.where(lv == int(math.log2(m)), s, scores)
        v_bf = v.astype(BF16)
        o = jnp.dot(scores.astype(BF16), v_bf, preferred_element_type=F32)
        o = o + jnp.sum(q * (kf + kb), axis=-1, keepdims=True) * v
        b_inc = ef_all[0:c]
        q_dec = (q * jnp.exp(b_inc)).astype(BF16)
        o = o + lax.dot_general(q_dec, st.astype(BF16), nt, preferred_element_type=F32)
        k_dec = (kf * jnp.exp(ef_all[c:2 * c])).astype(BF16)
        vt_bf = v.T.astype(BF16)
        st = st * jnp.exp(b_inc[c - 1:c, :]) + jnp.dot(vt_bf, k_dec, preferred_element_type=F32)
        acc_ref[rows, :] = o
        bb = eb_all[0:c]
        qb_ref[rows, :] = (q * jnp.exp(bb)).astype(BF16)
        kb_ref[rows, :] = (kb * jnp.exp(eb_all[c:2 * c])).astype(BF16)
        db_ref[ci] = jnp.broadcast_to(jnp.exp(bb[0:1, :]), (8, LANES))
        return st

    lax.fori_loop(0, n_chunks, fwd_body, jnp.zeros((c, c), F32), unroll=2)

    gain = gain_ref[...]

    def bwd_body(i, st):
        ci = n_chunks - 1 - i
        rows = pl.ds(pl.multiple_of(ci * c, c), c)
        o = acc_ref[rows, :] + lax.dot_general(qb_ref[rows, :], st.astype(BF16), nt,
                                               preferred_element_type=F32)
        vt_bf = i_ref[rows, :].T.astype(BF16)
        st = st * db_ref[ci][0:1, :] + jnp.dot(vt_bf, kb_ref[rows, :], preferred_element_type=F32)
        o = o * lax.rsqrt(jnp.mean(o * o, axis=-1, keepdims=True) + NORM_EPS) * gain
        gr = g_ref[rows, :]
        o_ref[rows, :] = (o * (gr * jax.nn.sigmoid(gr))).astype(o_ref.dtype)
        return st

    lax.fori_loop(0, n_chunks, bwd_body, jnp.zeros((c, c), F32))


def hgrn2_mixer(proj3, lb_logits, norm_gain):
    b, l, _ = proj3.shape
    h, dh = HG_HEADS, HG_HEAD_DIM
    mf, mb, lv = _gla_constants()
    nblk = mf.shape[0]

    def col(off):
        return pl.BlockSpec((None, l, dh), lambda bi, hi, off=off: (bi, 0, off + hi))

    const2 = lambda bi, hi: (0, 0)
    return pl.pallas_call(
        _hgrn_kernel,
        grid=(b, h),
        in_specs=[col(0), col(h), col(2 * h), col(3 * h), col(4 * h),
                  pl.BlockSpec((2, 2, dh), lambda bi, hi: (0, 0, hi)),
                  pl.BlockSpec((1, dh), lambda bi, hi: (0, hi)),
                  pl.BlockSpec((nblk, GLA_CHUNK), const2),
                  pl.BlockSpec((nblk, GLA_CHUNK), const2),
                  pl.BlockSpec((GLA_CHUNK, GLA_CHUNK), const2)],
        out_specs=pl.BlockSpec((None, l, dh), lambda bi, hi: (bi, 0, hi)),
        out_shape=jax.ShapeDtypeStruct((b, l, HG_WIDTH), BF16),
        scratch_shapes=[pltpu.VMEM((l, dh), F32), pltpu.VMEM((l, dh), BF16), pltpu.VMEM((l, dh), BF16),
                        pltpu.VMEM((l // GLA_CHUNK, 8, LANES), F32)],
        compiler_params=_cparams(("arbitrary", "arbitrary")),
        name="hgrn2_mixer",
    )(proj3, proj3, proj3, proj3, proj3, lb_logits, norm_gain.reshape(1, HG_WIDTH), mf, mb, lv)


@functools.lru_cache(maxsize=None)
def _dft_tables(l):
    n = 2 * l
    k2 = 2 * np.arange(l, dtype=np.int64)[:, None] + 1
    m1 = np.arange(l, dtype=np.int64)[None, :]
    ang_s = ((k2 * (2 * m1 + 1)) % (4 * n)).astype(np.float64) * (2.0 * math.pi / (4 * n))
    ang_f = ((k2 * m1) % (2 * n)).astype(np.float64) * (2.0 * math.pi / (2 * n))
    sym = np.stack([np.cos(ang_s), np.sin(ang_s)]).astype(np.float32)
    flt = np.stack([np.cos(ang_f), np.sin(ang_f)]).astype(np.float32)
    return sym, flt


def _filter_features(l):
    pos = jnp.arange(l, dtype=F32)
    t = pos / max(l - 1, 1)
    bands = jnp.linspace(1e-4, HY_BANDS - 1, HY_BANDS, dtype=F32)
    ang = (2.0 * math.pi / l) * pos[:, None] * bands[None, :]
    z = jnp.concatenate([t[:, None], jnp.cos(ang), -jnp.sin(ang)], axis=-1)
    z = jnp.pad(z, ((0, 0), (0, LANES - HY_EMB)))
    min_decay = math.log(HY_DECAY_TARGET) / HY_FAST_DECAY_PCT
    max_decay = math.log(HY_DECAY_TARGET) / HY_SLOW_DECAY_PCT
    deltas = jnp.abs(jnp.linspace(min_decay, max_decay, HY_WIDTH, dtype=F32))
    window = jnp.exp(-t[:, None] * deltas[None, :])
    return z, window


def _filter_kernel(z_ref, win_ref, w1_ref, b1_ref, w2_ref, b2_ref, w3_ref, b3_ref, w4_ref, fr_ref,
                   sum_ref, diff_ref):
    hp = lax.Precision.HIGHEST
    fr = fr_ref[...]
    h = jnp.sin(fr * (jnp.dot(z_ref[...], w1_ref[...], precision=hp, preferred_element_type=F32) + b1_ref[...]))
    h = jnp.sin(fr * (jnp.dot(h, w2_ref[...], precision=hp, preferred_element_type=F32) + b2_ref[...]))
    h = jnp.sin(fr * (jnp.dot(h, w3_ref[...], precision=hp, preferred_element_type=F32) + b3_ref[...]))
    h = jnp.dot(h, w4_ref[...], precision=hp, preferred_element_type=F32)
    win = win_ref[...]
    tl = h.shape[0]
    lag = pl.program_id(0) * tl + lax.broadcasted_iota(jnp.int32, (tl, HY_WIDTH), 0)
    for o in range(HY_ORDER):
        hf = h[:, (2 * o) * HY_WIDTH:(2 * o + 1) * HY_WIDTH] * win
        hb = h[:, (2 * o + 1) * HY_WIDTH:(2 * o + 2) * HY_WIDTH] * win
        hb = jnp.where(lag == 0, 0.0, hb)
        sum_ref[:, o * HY_WIDTH:(o + 1) * HY_WIDTH] = (hf + hb).astype(sum_ref.dtype)
        diff_ref[:, o * HY_WIDTH:(o + 1) * HY_WIDTH] = (hf - hb).astype(diff_ref.dtype)


def hyena_filter_taps(l, w1, b1, w2, b2, w3, b3, w4, freq, tl=256):
    z, window = _filter_features(l)
    w1p = jnp.pad(w1, ((0, LANES - HY_EMB), (0, 0)))
    hid = HY_FILTER_HIDDEN
    full = lambda shape: pl.BlockSpec(shape, lambda i: (0,) * len(shape))
    nout = HY_ORDER * HY_WIDTH
    return pl.pallas_call(
        _filter_kernel,
        grid=(l // tl,),
        in_specs=[pl.BlockSpec((tl, LANES), lambda i: (i, 0)),
                  pl.BlockSpec((tl, HY_WIDTH), lambda i: (i, 0)),
                  full((LANES, hid)), full((1, hid)), full((hid, hid)), full((1, hid)),
                  full((hid, hid)), full((1, hid)), full((hid, 2 * nout)), full((1, hid))],
        out_specs=[pl.BlockSpec((tl, nout), lambda i: (i, 0)), pl.BlockSpec((tl, nout), lambda i: (i, 0))],
        out_shape=[jax.ShapeDtypeStruct((l, nout), BF16), jax.ShapeDtypeStruct((l, nout), BF16)],
        compiler_params=_cparams(("arbitrary",)),
        name="hyena_filter_taps",
    )(z, window, w1p, b1.reshape(1, hid), w2, b2.reshape(1, hid), w3, b3.reshape(1, hid), w4,
      freq.reshape(1, hid))


def _spectrum_kernel(f_ref, h_ref, o_ref):
    o_ref[...] = jnp.dot(f_ref[...], h_ref[...], preferred_element_type=F32).astype(o_ref.dtype)


def hyena_filter_spectra(flt, taps, tn=512):
    _, l, nout = taps.shape
    return pl.pallas_call(
        _spectrum_kernel,
        grid=(2, nout // tn),
        in_specs=[pl.BlockSpec((None, l, l), lambda p, j: (p, 0, 0)),
                  pl.BlockSpec((None, l, tn), lambda p, j: (p, 0, j))],
        out_specs=pl.BlockSpec((None, l, tn), lambda p, j: (p, 0, j)),
        out_shape=jax.ShapeDtypeStruct((2, l, nout), BF16),
        compiler_params=_cparams(("arbitrary", "arbitrary")),
        name="hyena_filter_spectra",
    )(flt, taps)


def _hyena_kernel(xv_ref, x1_ref, x2_ref, cw_ref, cb_ref, sym_ref, sp0_ref, sp1_ref, bias_ref, o_ref,
                  z_s, g_s, zb_s, pa_s, pb_s, *, rb):
    l, tc = xv_ref.shape
    scale = 2.0 / (2 * l)
    first = lax.broadcasted_iota(jnp.int32, (l, tc), 0) == 0
    last = lax.broadcasted_iota(jnp.int32, (l, tc), 0) == l - 1

    def short_conv(x_ref, part):
        x = x_ref[...]
        w = cw_ref[part]
        prev = jnp.where(first, 0.0, pltpu.roll(x, 1, 0))
        nxt = jnp.where(last, 0.0, pltpu.roll(x, l - 1, 0))
        return prev * w[0:1, :] + x * w[1:2, :] + nxt * w[2:3, :] + cb_ref[part]

    z_s[...] = short_conv(xv_ref, 0)
    for o, (gate_ref, part, sp_ref) in enumerate(((x1_ref, 1, sp0_ref), (x2_ref, 2, sp1_ref))):
        zb_s[...] = z_s[...].astype(BF16)
        g_s[...] = short_conv(gate_ref, part)
        for r0 in range(0, l, rb):
            a = jnp.dot(sym_ref[0, r0:r0 + rb, :], zb_s[...], preferred_element_type=F32)
            b = jnp.dot(sym_ref[1, r0:r0 + rb, :], zb_s[...], preferred_element_type=F32)
            hr = sp_ref[0, r0:r0 + rb, :].astype(F32)
            hs = sp_ref[1, r0:r0 + rb, :].astype(F32)
            pa_s[r0:r0 + rb, :] = (a * hr - b * hs).astype(BF16)
            pb_s[r0:r0 + rb, :] = (a * hs + b * hr).astype(BF16)
        for r0 in range(0, l, rb):
            y = (jnp.dot(sym_ref[0, r0:r0 + rb, :], pa_s[...], preferred_element_type=F32)
                 + jnp.dot(sym_ref[1, r0:r0 + rb, :], pb_s[...], preferred_element_type=F32)) * scale
            y = y + z_s[r0:r0 + rb, :] * bias_ref[o]
            z_s[r0:r0 + rb, :] = g_s[r0:r0 + rb, :] * y
    o_ref[...] = z_s[...].astype(o_ref.dtype)


def hyena_mixer(proj3, conv_w, conv_b, sym, spectra, hy_bias, tc=256, rb=1024):
    b, l, _ = proj3.shape
    nct = HY_WIDTH // tc
    base = (5 * HG_WIDTH) // tc

    def col(part):
        return pl.BlockSpec((None, l, tc), lambda ci, bi, part=part: (bi, 0, base + part * nct + ci))

    def spec_cols(o):
        return pl.BlockSpec((2, l, tc), lambda ci, bi, o=o: (0, 0, o * nct + ci))

    cw = conv_w.reshape(3, HY_ORDER + 1, HY_WIDTH).transpose(1, 0, 2)
    cb = conv_b.reshape(HY_ORDER + 1, 1, HY_WIDTH)
    return pl.pallas_call(
        functools.partial(_hyena_kernel, rb=rb),
        grid=(nct, b),
        in_specs=[col(0), col(1), col(2),
                  pl.BlockSpec((HY_ORDER + 1, 3, tc), lambda ci, bi: (0, 0, ci)),
                  pl.BlockSpec((HY_ORDER + 1, 1, tc), lambda ci, bi: (0, 0, ci)),
                  pl.BlockSpec((2, l, l), lambda ci, bi: (0, 0, 0), pipeline_mode=pl.Buffered(1)),
                  spec_cols(0), spec_cols(1),
                  pl.BlockSpec((HY_ORDER, 1, tc), lambda ci, bi: (0, 0, ci))],
        out_specs=pl.BlockSpec((None, l, tc), lambda ci, bi: (bi, 0, ci)),
        out_shape=jax.ShapeDtypeStruct((b, l, HY_WIDTH), BF16),
        scratch_shapes=[pltpu.VMEM((l, tc), F32), pltpu.VMEM((l, tc), F32), pltpu.VMEM((l, tc), BF16),
                        pltpu.VMEM((l, tc), BF16), pltpu.VMEM((l, tc), BF16)],
        compiler_params=_cparams(("arbitrary", "arbitrary")),
        name="hyena_mixer",
    )(proj3, proj3, proj3, cw, cb, sym, spectra, spectra, hy_bias.reshape(HY_ORDER, 1, HY_WIDTH))


def _outproj_router_kernel(hg_ref, hy_ref, x_ref, w_ref, hyg_ref, fg_ref, wr_ref, br_ref, tri_ref,
                           x2_ref, xn_ref, meta_ref, cnt_ref, carry_ref):
    @pl.when(pl.program_id(0) == 0)
    def _():
        carry_ref[...] = jnp.zeros_like(carry_ref)

    hy = hy_ref[...].astype(F32)
    hy = hy * lax.rsqrt(jnp.mean(hy * hy, axis=-1, keepdims=True) + NORM_EPS) * hyg_ref[...]
    m = (jnp.dot(hg_ref[...], w_ref[0:HG_WIDTH, :], preferred_element_type=F32)
         + jnp.dot(hy.astype(BF16), w_ref[HG_WIDTH:, :], preferred_element_type=F32))
    x2 = x_ref[...] + m
    x2_ref[...] = x2
    xn = x2 * lax.rsqrt(jnp.mean(x2 * x2, axis=-1, keepdims=True) + NORM_EPS) * fg_ref[...]
    xn_ref[...] = xn
    xn_hi = xn.astype(BF16)
    xn_lo = (xn - xn_hi.astype(F32)).astype(BF16)
    wr = wr_ref[...]
    wr_hi = wr.astype(BF16)
    wr_lo = (wr - wr_hi.astype(F32)).astype(BF16)
    logits = (jnp.dot(xn_hi, wr_hi, preferred_element_type=F32)
              + jnp.dot(xn_hi, wr_lo, preferred_element_type=F32)
              + jnp.dot(xn_lo, wr_hi, preferred_element_type=F32)) + br_ref[...]
    tm = logits.shape[0]
    lane = lax.broadcasted_iota(jnp.int32, (tm, LANES), 1).astype(F32)
    neg = jnp.float32(-jnp.inf)
    logits = jnp.where(lane < N_EXPERTS, logits, neg)
    tops, idxs = [], []
    for _ in range(TOP_K):
        mx = jnp.max(logits, axis=-1, keepdims=True)
        idx = jnp.min(jnp.where(logits == mx, lane, float(LANES)), axis=-1, keepdims=True)
        tops.append(mx)
        idxs.append(idx)
        logits = jnp.where(lane == idx, neg, logits)
    exps = [jnp.exp(t - tops[0]) for t in tops]
    denom = exps[0] + exps[1] + exps[2] + exps[3]
    onehot = jnp.zeros((tm, LANES), F32)
    for idx in idxs:
        onehot = onehot + jnp.where(lane == idx, 1.0, 0.0)
    cum = jnp.dot(tri_ref[...], onehot.astype(BF16), preferred_element_type=F32) + carry_ref[0:1, :]
    meta = jnp.zeros((tm, LANES), F32)
    for k in range(TOP_K):
        rank = jnp.sum(jnp.where(lane == idxs[k], cum, 0.0), axis=-1, keepdims=True)
        meta = jnp.where(lane == k, idxs[k], meta)
        meta = jnp.where(lane == TOP_K + k, rank, meta)
        meta = jnp.where(lane == 2 * TOP_K + k, exps[k] / denom, meta)
    meta_ref[...] = meta
    carry = carry_ref[...] + jnp.sum(onehot, axis=0, keepdims=True)
    carry_ref[...] = carry
    cnt_ref[...] = carry


def outproj_router(hg2d, hy2d, x2d, w_out_bf16, hy_gain, ffn_gain, w_router, b_router, tm=512):
    n, d = x2d.shape
    wr = jnp.pad(w_router, ((0, 0), (0, LANES - N_EXPERTS)))
    br = jnp.pad(b_router, (0, LANES - N_EXPERTS)).reshape(1, LANES)
    tri = jnp.asarray(np.tril(np.ones((tm, tm), np.float32), -1), BF16)
    row = lambda w: pl.BlockSpec((tm, w), lambda i: (i, 0))
    full = lambda shape, **kw: pl.BlockSpec(shape, lambda i: (0,) * len(shape), **kw)
    return pl.pallas_call(
        _outproj_router_kernel,
        grid=(n // tm,),
        in_specs=[row(HG_WIDTH), row(HY_WIDTH), row(d),
                  full((d, d), pipeline_mode=pl.Buffered(1)),
                  full((1, HY_WIDTH)), full((1, d)), full((d, LANES)), full((1, LANES)), full((tm, tm))],
        out_specs=[row(d), row(d), row(LANES), full((8, LANES))],
        out_shape=[jax.ShapeDtypeStruct((n, d), F32), jax.ShapeDtypeStruct((n, d), F32),
                   jax.ShapeDtypeStruct((n, LANES), F32), jax.ShapeDtypeStruct((8, LANES), F32)],
        scratch_shapes=[pltpu.VMEM((8, LANES), F32)],
        compiler_params=_cparams(("arbitrary",)),
        name="outproj_router",
    )(hg2d, hy2d, x2d, w_out_bf16, hy_gain.reshape(1, HY_WIDTH), ffn_gain.reshape(1, d), wr, br, tri)


def _slot_table_kernel(dest_ref, slot_ref):
    n_slots = slot_ref.shape[0]
    n_pairs = dest_ref.shape[0]

    def fill(i, carry):
        slot_ref[i] = 0
        return carry

    lax.fori_loop(0, n_slots, fill, 0, unroll=16)

    def scatter(a, carry):
        slot_ref[dest_ref[a]] = lax.shift_right_logical(a, int(math.log2(TOP_K)))
        return carry

    lax.fori_loop(0, n_pairs, scatter, 0, unroll=16)


def build_slot_table(dest_flat, n_slots):
    return pl.pallas_call(
        _slot_table_kernel,
        in_specs=[pl.BlockSpec(memory_space=pltpu.SMEM)],
        out_specs=pl.BlockSpec(memory_space=pltpu.SMEM),
        out_shape=jax.ShapeDtypeStruct((n_slots,), jnp.int32),
        name="build_slot_table",
    )(dest_flat)


GATHER_UNROLL = 8


def _row_gather_kernel(slot_ref, rows_ref, used_ref, x_hbm, o_ref, buf_ref, sems):
    t = pl.program_id(0)
    tr = buf_ref.shape[1]
    used = used_ref[0]

    def n_copies(tile):
        return ((rows_ref[tile] + GATHER_UNROLL - 1) // GATHER_UNROLL) * GATHER_UNROLL

    def issue(tile):
        b = tile % 2
        base = tile * tr

        def body(i, carry):
            for u in range(GATHER_UNROLL):
                r = i * GATHER_UNROLL + u
                pltpu.make_async_copy(x_hbm.at[pl.ds(slot_ref[base + r], 1)], buf_ref.at[b, pl.ds(r, 1)],
                                      sems.at[b]).start()
            return carry

        lax.fori_loop(0, n_copies(tile) // GATHER_UNROLL, body, 0)

    @pl.when(t == 0)
    def _():
        buf_ref[...] = jnp.zeros(buf_ref.shape, buf_ref.dtype)
        issue(t)

    @pl.when(t + 1 < used)
    def _():
        issue(t + 1)

    @pl.when(t < used)
    def _():
        b = t % 2
        nc = n_copies(t)

        @pl.when(nc > 0)
        def _():
            pltpu.make_async_copy(x_hbm.at[pl.ds(0, nc)], buf_ref.at[b, pl.ds(0, nc)], sems.at[b]).wait()

        row = lax.broadcasted_iota(jnp.int32, o_ref.shape, 0)
        o_ref[...] = jnp.where(row < rows_ref[t], buf_ref[b], 0.0).astype(o_ref.dtype)


def gather_rows(slot_tok, tile_rows, used_tiles, xn, n_slots, tr):
    n, d = xn.shape
    grid_spec = pltpu.PrefetchScalarGridSpec(
        num_scalar_prefetch=3,
        grid=(used_tiles[0],),
        in_specs=[pl.BlockSpec(memory_space=pl.ANY)],
        out_specs=pl.BlockSpec((tr, d), lambda t, slot, rows, used: (jnp.minimum(t, used[0] - 1), 0)),
        scratch_shapes=[pltpu.VMEM((2, tr, d), F32), pltpu.SemaphoreType.DMA((2,))],
    )
    return pl.pallas_call(
        _row_gather_kernel,
        grid_spec=grid_spec,
        out_shape=jax.ShapeDtypeStruct((n_slots, d), BF16),
        compiler_params=pltpu.CompilerParams(dimension_semantics=("arbitrary",), vmem_limit_bytes=VMEM_LIMIT,
                                             disable_bounds_checks=True),
        name="moe_gather_rows",
    )(slot_tok, tile_rows, used_tiles, xn)


def _expert_kernel(te_ref, tv_ref, tb_ref, x_ref, wg_ref, wl_ref, bg_ref, bl_ref, w2_ref, b2_ref, o_ref,
                   act_s, wg_s, wl_s, w2_s, *, nf, blocks):
    j = pl.program_id(0)
    s = pl.program_id(1)
    valid = tv_ref[j]
    tf = wg_ref.shape[1]

    @pl.when(jnp.logical_and(valid > 0, s < nf))
    def _():
        wg_s[...] = wg_ref[...].astype(BF16)
        wl_s[...] = wl_ref[...].astype(BF16)
        for r0, rn in blocks:
            @pl.when(r0 < valid)
            def _(r0=r0, rn=rn):
                x = x_ref[r0:r0 + rn, :]
                glu = jnp.dot(x, wg_s[...], preferred_element_type=F32) + bg_ref[...]
                lin = jnp.dot(x, wl_s[...], preferred_element_type=F32) + bl_ref[...]
                glu = jnp.minimum(glu, SWIGLU_LIMIT)
                lin = jnp.clip(lin, -SWIGLU_LIMIT, SWIGLU_LIMIT)
                act = glu * jax.nn.sigmoid(SWIGLU_ALPHA * glu) * (lin + 1.0)
                act_s[s, r0:r0 + rn, :] = act.astype(BF16)

    @pl.when(jnp.logical_and(valid > 0, s >= nf))
    def _():
        w2_s[...] = w2_ref[...].astype(BF16)
        for r0, rn in blocks:
            @pl.when(r0 < valid)
            def _(r0=r0, rn=rn):
                y = b2_ref[...] + jnp.dot(act_s[0, r0:r0 + rn, :], w2_s[0:tf, :], preferred_element_type=F32)
                for f in range(1, nf):
                    y = y + jnp.dot(act_s[f, r0:r0 + rn, :], w2_s[f * tf:(f + 1) * tf, :],
                                    preferred_element_type=F32)
                o_ref[r0:r0 + rn, :] = y

            @pl.when(r0 >= valid)
            def _(r0=r0, rn=rn):
                o_ref[r0:r0 + rn, :] = jnp.zeros((rn, o_ref.shape[1]), o_ref.dtype)


def expert_ffn(n_used_tiles, tile_expert, tile_valid, tile_block, xs, w1, b1, w2, b2, tm, blocks, tf=512):
    n_slots, d = xs.shape
    nf = D_FF // tf
    nn = d // tf
    assert nn == nf
    b1r = b1.reshape(N_EXPERTS, 1, 2 * D_FF)
    b2r = b2.reshape(N_EXPERTS, 1, d)

    def w1_blk(j, s, tv):
        return jnp.where(tv[j] > 0, jnp.minimum(s, nf - 1), nf - 1)

    def w2_blk(j, s, tv):
        return jnp.where(tv[j] > 0, jnp.maximum(s - nf, 0), nn - 1)

    grid_spec = pltpu.PrefetchScalarGridSpec(
        num_scalar_prefetch=3,
        grid=(n_used_tiles, nf + nn),
        in_specs=[
            pl.BlockSpec((tm, d), lambda j, s, te, tv, tb: (tb[j], 0)),
            pl.BlockSpec((None, d, tf), lambda j, s, te, tv, tb: (te[j], 0, w1_blk(j, s, tv))),
            pl.BlockSpec((None, d, tf), lambda j, s, te, tv, tb: (te[j], 0, nf + w1_blk(j, s, tv))),
            pl.BlockSpec((None, 1, tf), lambda j, s, te, tv, tb: (te[j], 0, w1_blk(j, s, tv))),
            pl.BlockSpec((None, 1, tf), lambda j, s, te, tv, tb: (te[j], 0, nf + w1_blk(j, s, tv))),
            pl.BlockSpec((None, D_FF, tf), lambda j, s, te, tv, tb: (te[j], 0, w2_blk(j, s, tv))),
            pl.BlockSpec((None, 1, tf), lambda j, s, te, tv, tb: (te[j], 0, w2_blk(j, s, tv))),
        ],
        out_specs=pl.BlockSpec((tm, tf), lambda j, s, te, tv, tb: (tb[j], w2_blk(j, s, tv))),
        scratch_shapes=[pltpu.VMEM((nf, tm, tf), BF16), pltpu.VMEM((d, tf), BF16), pltpu.VMEM((d, tf), BF16),
                        pltpu.VMEM((D_FF, tf), BF16)],
    )
    return pl.pallas_call(
        functools.partial(_expert_kernel, nf=nf, blocks=blocks),
        grid_spec=grid_spec,
        out_shape=jax.ShapeDtypeStruct((n_slots, d), F32),
        compiler_params=_cparams(("arbitrary", "arbitrary")),
        name="moe_expert_ffn",
    )(tile_expert, tile_valid, tile_block, xs, w1, w1, b1r, b1r, w2, b2r)


def _combine_kernel(dest_ref, ys_hbm, x2_ref, meta_ref, gain_ref, o_ref, buf_ref, sems):
    t = pl.program_id(0)
    nt = pl.num_programs(0)
    tt = x2_ref.shape[0]

    def issue(tile):
        b = tile % 2
        base = tile * (tt * TOP_K)

        def body(r, carry):
            for k in range(TOP_K):
                pltpu.make_async_copy(ys_hbm.at[pl.ds(dest_ref[base + r * TOP_K + k], 1)],
                                      buf_ref.at[b, k, pl.ds(r, 1)], sems.at[b]).start()
            return carry

        lax.fori_loop(0, tt, body, 0, unroll=4)

    @pl.when(t == 0)
    def _():
        issue(t)

    @pl.when(t + 1 < nt)
    def _():
        issue(t + 1)

    b = t % 2
    for k in range(TOP_K):
        pltpu.make_async_copy(ys_hbm.at[pl.ds(0, tt)], buf_ref.at[b, k], sems.at[b]).wait()
    meta = meta_ref[...]
    y = x2_ref[...]
    for k in range(TOP_K):
        y = y + meta[:, 2 * TOP_K + k:2 * TOP_K + k + 1] * buf_ref[b, k]
    o_ref[...] = y * lax.rsqrt(jnp.mean(y * y, axis=-1, keepdims=True) + NORM_EPS) * gain_ref[...]


def combine_final(dest_flat, ys, x2, meta, final_gain, tt=128):
    n, d = x2.shape
    grid_spec = pltpu.PrefetchScalarGridSpec(
        num_scalar_prefetch=1,
        grid=(n // tt,),
        in_specs=[pl.BlockSpec(memory_space=pl.ANY),
                  pl.BlockSpec((tt, d), lambda t, dest: (t, 0)),
                  pl.BlockSpec((tt, LANES), lambda t, dest: (t, 0)),
                  pl.BlockSpec((1, d), lambda t, dest: (0, 0))],
        out_specs=pl.BlockSpec((tt, d), lambda t, dest: (t, 0)),
        scratch_shapes=[pltpu.VMEM((2, TOP_K, tt, d), F32), pltpu.SemaphoreType.DMA((2,))],
    )
    return pl.pallas_call(
        _combine_kernel,
        grid_spec=grid_spec,
        out_shape=jax.ShapeDtypeStruct((n, d), F32),
        compiler_params=pltpu.CompilerParams(dimension_semantics=("arbitrary",), vmem_limit_bytes=VMEM_LIMIT,
                                             disable_bounds_checks=True),
        name="moe_combine_final",
    )(dest_flat, ys, x2, meta, final_gain.reshape(1, d))


MOE_TILE = 1280
MOE_ROW_BLOCKS = ((0, 512), (512, 512), (1024, 256))
MOE_GATHER_TILE = 256


def moe_ffn_final(x2, xn, meta, counts, w1, b1, w2, b2, final_gain, tm=MOE_TILE):
    n, d = x2.shape
    a = n * TOP_K
    n_tiles = -(-a // tm) + N_EXPERTS
    n_slots = n_tiles * tm
    idx = meta[:, 0:TOP_K].astype(jnp.int32)
    rank = meta[:, TOP_K:2 * TOP_K].astype(jnp.int32)
    cnt = counts[0, :N_EXPERTS].astype(jnp.int32)
    tiles_e = (cnt + tm - 1) // tm
    tile_end = jnp.cumsum(tiles_e)
    tile_start = tile_end - tiles_e
    used = tile_end[-1]
    dest = (tile_start * tm)[idx] + rank
    dest_flat = dest.reshape(a)
    tj = jnp.arange(n_tiles, dtype=jnp.int32)
    te = jnp.minimum(jnp.sum(tj[:, None] >= tile_end[None, :], axis=1), N_EXPERTS - 1).astype(jnp.int32)
    last_e = te[jnp.maximum(used - 1, 0)]
    tile_expert = jnp.where(tj < used, te, last_e).astype(jnp.int32)
    tile_valid = jnp.where(tj < used, jnp.clip(cnt[te] - (tj - tile_start[te]) * tm, 0, tm), 0).astype(jnp.int32)
    tile_block = jnp.maximum(jnp.minimum(tj, used - 1), 0).astype(jnp.int32)

    slot_tok = build_slot_table(dest_flat, n_slots)
    tr = MOE_GATHER_TILE
    per = tm // tr
    gj = jnp.arange(n_tiles * per, dtype=jnp.int32)
    gather_rows_valid = jnp.clip(tile_valid[gj // per] - (gj % per) * tr, 0, tr).astype(jnp.int32)
    used_gather = jnp.maximum(used * per, 1).astype(jnp.int32).reshape(1)
    xs = gather_rows(slot_tok, gather_rows_valid, used_gather, xn, n_slots, tr)
    ys = expert_ffn(jnp.maximum(used, 1), tile_expert, tile_valid, tile_block, xs, w1, b1, w2, b2, tm,
                    MOE_ROW_BLOCKS)
    return combine_final(dest_flat, ys, x2, meta, final_gain)


def kernel(x, norm_mix_gain, w_in, hgrn_lb_logits, hgrn_norm_gain, hy_conv_w, hy_conv_b, hy_filt_w1,
           hy_filt_b1, hy_filt_w2, hy_filt_b2, hy_filt_w3, hy_filt_b3, hy_filt_w4, hy_filt_freq, hy_bias,
           hy_norm_gain, w_out, norm_ffn_gain, w_router, b_router, moe_w1, moe_b1, moe_w2, moe_b2,
           final_norm_gain):
    b, l, d = x.shape
    n = b * l
    x2d = x.reshape(n, d)
    proj = in_projection(x2d, norm_mix_gain[0], w_in[0].astype(BF16))
    proj3 = proj.reshape(b, l, D_IN)
    hg = hgrn2_mixer(proj3, hgrn_lb_logits, hgrn_norm_gain[0])

    sym, flt = (jnp.asarray(tab).astype(BF16) for tab in _dft_tables(l))
    tap_sum, tap_diff = hyena_filter_taps(l, hy_filt_w1[0], hy_filt_b1[0], hy_filt_w2[0], hy_filt_b2[0],
                                          hy_filt_w3[0], hy_filt_b3[0], hy_filt_w4[0], hy_filt_freq[0])
    spectra = hyena_filter_spectra(flt, jnp.stack([tap_sum, tap_diff]))
    hy = hyena_mixer(proj3, hy_conv_w[0], hy_conv_b[0], sym, spectra, hy_bias[0])

    x2, xn, meta, counts = outproj_router(hg.reshape(n, HG_WIDTH), hy.reshape(n, HY_WIDTH), x2d,
                                          w_out[0].astype(BF16), hy_norm_gain[0], norm_ffn_gain[0],
                                          w_router[0], b_router[0])
    out = moe_ffn_final(x2, xn, meta, counts, moe_w1[0], moe_b1[0], moe_w2[0], moe_b2[0], final_norm_gain)
    return out.reshape(b, l, d)
```

```python
import functools
import math

import numpy as np
import jax
import jax.numpy as jnp
from jax import lax
from jax.experimental import pallas as pl
from jax.experimental.pallas import tpu as pltpu

F32 = jnp.float32
BF16 = jnp.bfloat16

D_MODEL = 2048
HG_WIDTH = 1024
HG_HEAD_DIM = 128
HG_HEADS = HG_WIDTH // HG_HEAD_DIM
HY_WIDTH = 1024
HY_ORDER = 2
HY_EMB = 33
HY_BANDS = (HY_EMB - 1) // 2
HY_FILTER_HIDDEN = 64
HY_FAST_DECAY_PCT = 0.3
HY_SLOW_DECAY_PCT = 1.5
HY_DECAY_TARGET = 1e-2
N_EXPERTS = 32
TOP_K = 4
D_FF = D_MODEL
SWIGLU_ALPHA = 1.702
SWIGLU_LIMIT = 7.0
NORM_EPS = 1e-5
D_IN = 5 * HG_WIDTH + (HY_ORDER + 1) * HY_WIDTH

LANES = 128
GLA_CHUNK = 128
GLA_LEVELS = (64, 32, 16, 8, 4, 2, 1)
VMEM_LIMIT = 56 * 1024 * 1024


def _cparams(sem, vmem=VMEM_LIMIT):
    return pltpu.CompilerParams(dimension_semantics=sem, vmem_limit_bytes=vmem)


def _inproj_kernel(x_ref, g_ref, w_ref, o_ref, h_s):
    @pl.when(pl.program_id(1) == 0)
    def _():
        x = x_ref[...]
        ms = jnp.mean(x * x, axis=-1, keepdims=True)
        h_s[...] = (x * lax.rsqrt(ms + NORM_EPS) * g_ref[...]).astype(BF16)

    o_ref[...] = jnp.dot(h_s[...], w_ref[...], preferred_element_type=F32)


def in_projection(x2d, gain, w_bf16, tm=1024, tn=1024):
    n, d = x2d.shape
    dout = w_bf16.shape[1]
    return pl.pallas_call(
        _inproj_kernel,
        grid=(n // tm, dout // tn),
        in_specs=[
            pl.BlockSpec((tm, d), lambda i, j: (i, 0)),
            pl.BlockSpec((1, d), lambda i, j: (0, 0)),
            pl.BlockSpec((d, tn), lambda i, j: (0, j)),
        ],
        out_specs=pl.BlockSpec((tm, tn), lambda i, j: (i, j)),
        out_shape=jax.ShapeDtypeStruct((n, dout), F32),
        scratch_shapes=[pltpu.VMEM((tm, d), BF16)],
        compiler_params=_cparams(("arbitrary", "arbitrary")),
        name="in_projection",
    )(x2d, gain.reshape(1, d), w_bf16)


def _gla_constants():
    c = GLA_CHUNK
    t = np.arange(c)[:, None]
    r = np.arange(c)[None, :]
    fwd = [r <= t, r > t]
    bwd = [r >= t, r < t]
    for m in GLA_LEVELS:
        pos = t % (2 * m)
        mid = t - pos + m
        second = pos >= m
        fwd.append(np.where(second, (r >= mid) & (r <= t), (r > t) & (r < mid)))
        bwd.append(np.where(second, (r >= mid) & (r < t), (r >= t) & (r < mid)))
    x = t ^ r
    lv = np.full((c, c), -1, np.int32)
    for j in range(int(math.log2(c))):
        lv = np.where((x >> j) == 1, j, lv)
    mf = jnp.asarray(np.concatenate(fwd, 0).astype(np.float32), BF16)
    mb = jnp.asarray(np.concatenate(bwd, 0).astype(np.float32), BF16)
    return mf, mb, jnp.asarray(lv, jnp.int32)


def _hgrn_kernel(q_ref, ff_ref, fb_ref, i_ref, g_ref, lb_ref, gain_ref, mf_ref, mb_ref, lv_ref,
                 o_ref, acc_ref, qb_ref, kb_ref, db_ref):
    c = GLA_CHUNK
    n_chunks = q_ref.shape[0] // c
    lb = lb_ref[...]
    l0, l1 = lb[0], lb[1]
    mx = jnp.maximum(l0, l1)
    e0 = jnp.exp(l0 - mx)
    e1 = jnp.exp(l1 - mx)
    p0 = e0 / (e0 + e1)
    lb_f = p0[0:1, :]
    lb_b = p0[1:2, :]
    lv = lv_ref[...]
    row = lax.broadcasted_iota(jnp.int32, (c, LANES), 0)
    nt = (((1,), (1,)), ((), ()))

    def gates(z, lower):
        f = lower + (1.0 - lower) * jax.nn.sigmoid(z)
        return 1.0 - f, jnp.log(f)

    def exponents(m_ref, g):
        g_hi = g.astype(BF16)
        g_lo = (g - g_hi.astype(F32)).astype(BF16)
        m = m_ref[...]
        return (jnp.dot(m, g_hi, preferred_element_type=F32)
                + jnp.dot(m, g_lo, preferred_element_type=F32))

    def fwd_body(ci, st):
        rows = pl.ds(pl.multiple_of(ci * c, c), c)
        qr = q_ref[rows, :]
        q = qr * jax.nn.sigmoid(qr)
        v = i_ref[rows, :]
        kf, gf = gates(ff_ref[rows, :], lb_f)
        kb, gb = gates(fb_ref[rows, :], lb_b)
        ef_all = exponents(mf_ref, gf)
        eb_all = exponents(mb_ref, gb)
        scores = jnp.zeros((c, c), F32)
        for li, m in enumerate(GLA_LEVELS):
            ef = ef_all[(2 + li) * c:(3 + li) * c]
            eb = eb_all[(2 + li) * c:(3 + li) * c]
            second = (row & m) != 0
            a = q * jnp.exp(jnp.where(second, ef, eb))
            b = jnp.where(second, kb, kf) * jnp.exp(jnp.where(second, eb, ef))
            s = lax.dot_general(a.astype(BF16), b.astype(BF16), nt, preferred_element_type=F32)
            scores = jnp.where(lv == int(math.log2(m)), s, scores)
        v_bf = v.astype(BF16)
        o = jnp.dot(scores.astype(BF16), v_bf, preferred_element_type=F32)
        o = o + jnp.sum(q * (kf + kb), axis=-1, keepdims=True) * v
        b_inc = ef_all[0:c]
        q_dec = (q * jnp.exp(b_inc)).astype(BF16)
        o = o + lax.dot_general(q_dec, st.astype(BF16), nt, preferred_element_type=F32)
        k_dec = (kf * jnp.exp(ef_all[c:2 * c])).astype(BF16)
        vt_bf = v.T.astype(BF16)
        st = st * jnp.exp(b_inc[c - 1:c, :]) + jnp.dot(vt_bf, k_dec, preferred_element_type=F32)
        acc_ref[rows, :] = o
        bb = eb_all[0:c]
        qb_ref[rows, :] = (q * jnp.exp(bb)).astype(BF16)
        kb_ref[rows, :] = (kb * jnp.exp(eb_all[c:2 * c])).astype(BF16)
        db_ref[ci] = jnp.broadcast_to(jnp.exp(bb[0:1, :]), (8, LANES))
        return st

    lax.fori_loop(0, n_chunks, fwd_body, jnp.zeros((c, c), F32), unroll=4)

    gain = gain_ref[...]

    def bwd_body(i, st):
        ci = n_chunks - 1 - i
        rows = pl.ds(pl.multiple_of(ci * c, c), c)
        o = acc_ref[rows, :] + lax.dot_general(qb_ref[rows, :], st.astype(BF16), nt,
                                               preferred_element_type=F32)
        vt_bf = i_ref[rows, :].T.astype(BF16)
        st = st * db_ref[ci][0:1, :] + jnp.dot(vt_bf, kb_ref[rows, :], preferred_element_type=F32)
        o = o * lax.rsqrt(jnp.mean(o * o, axis=-1, keepdims=True) + NORM_EPS) * gain
        gr = g_ref[rows, :]
        o_ref[rows, :] = (o * (gr * jax.nn.sigmoid(gr))).astype(o_ref.dtype)
        return st

    lax.fori_loop(0, n_chunks, bwd_body, jnp.zeros((c, c), F32), unroll=4)


def hgrn2_mixer(proj3, lb_logits, norm_gain):
    b, l, _ = proj3.shape
    h, dh = HG_HEADS, HG_HEAD_DIM
    mf, mb, lv = _gla_constants()
    nblk = mf.shape[0]

    def col(off):
        return pl.BlockSpec((None, l, dh), lambda bi, hi, off=off: (bi, 0, off + hi))

    const2 = lambda bi, hi: (0, 0)
    return pl.pallas_call(
        _hgrn_kernel,
        grid=(b, h),
        in_specs=[col(0), col(h), col(2 * h), col(3 * h), col(4 * h),
                  pl.BlockSpec((2, 2, dh), lambda bi, hi: (0, 0, hi)),
                  pl.BlockSpec((1, dh), lambda bi, hi: (0, hi)),
                  pl.BlockSpec((nblk, GLA_CHUNK), const2),
                  pl.BlockSpec((nblk, GLA_CHUNK), const2),
                  pl.BlockSpec((GLA_CHUNK, GLA_CHUNK), const2)],
        out_specs=pl.BlockSpec((None, l, dh), lambda bi, hi: (bi, 0, hi)),
        out_shape=jax.ShapeDtypeStruct((b, l, HG_WIDTH), BF16),
        scratch_shapes=[pltpu.VMEM((l, dh), F32), pltpu.VMEM((l, dh), BF16), pltpu.VMEM((l, dh), BF16),
                        pltpu.VMEM((l // GLA_CHUNK, 8, LANES), F32)],
        compiler_params=_cparams(("arbitrary", "arbitrary")),
        name="hgrn2_mixer",
    )(proj3, proj3, proj3, proj3, proj3, lb_logits, norm_gain.reshape(1, HG_WIDTH), mf, mb, lv)


@functools.lru_cache(maxsize=None)
def _dft_tables(l):
    n = 2 * l
    k2 = 2 * np.arange(l, dtype=np.int64)[:, None] + 1
    m1 = np.arange(l, dtype=np.int64)[None, :]
    ang_s = ((k2 * (2 * m1 + 1)) % (4 * n)).astype(np.float64) * (2.0 * math.pi / (4 * n))
    ang_f = ((k2 * m1) % (2 * n)).astype(np.float64) * (2.0 * math.pi / (2 * n))
    sym = np.stack([np.cos(ang_s), np.sin(ang_s)]).astype(np.float32)
    flt = np.stack([np.cos(ang_f), np.sin(ang_f)]).astype(np.float32)
    return sym, flt


def _filter_features(l):
    pos = jnp.arange(l, dtype=F32)
    t = pos / max(l - 1, 1)
    bands = jnp.linspace(1e-4, HY_BANDS - 1, HY_BANDS, dtype=F32)
    ang = (2.0 * math.pi / l) * pos[:, None] * bands[None, :]
    z = jnp.concatenate([t[:, None], jnp.cos(ang), -jnp.sin(ang)], axis=-1)
    z = jnp.pad(z, ((0, 0), (0, LANES - HY_EMB)))
    min_decay = math.log(HY_DECAY_TARGET) / HY_FAST_DECAY_PCT
    max_decay = math.log(HY_DECAY_TARGET) / HY_SLOW_DECAY_PCT
    deltas = jnp.abs(jnp.linspace(min_decay, max_decay, HY_WIDTH, dtype=F32))
    window = jnp.exp(-t[:, None] * deltas[None, :])
    return z, window


def _filter_kernel(z_ref, win_ref, w1_ref, b1_ref, w2_ref, b2_ref, w3_ref, b3_ref, w4_ref, fr_ref,
                   sum_ref, diff_ref):
    hp = lax.Precision.HIGHEST
    fr = fr_ref[...]
    h = jnp.sin(fr * (jnp.dot(z_ref[...], w1_ref[...], precision=hp, preferred_element_type=F32) + b1_ref[...]))
    h = jnp.sin(fr * (jnp.dot(h, w2_ref[...], precision=hp, preferred_element_type=F32) + b2_ref[...]))
    h = jnp.sin(fr * (jnp.dot(h, w3_ref[...], precision=hp, preferred_element_type=F32) + b3_ref[...]))
    h = jnp.dot(h, w4_ref[...], precision=hp, preferred_element_type=F32)
    win = win_ref[...]
    tl = h.shape[0]
    lag = pl.program_id(0) * tl + lax.broadcasted_iota(jnp.int32, (tl, HY_WIDTH), 0)
    for o in range(HY_ORDER):
        hf = h[:, (2 * o) * HY_WIDTH:(2 * o + 1) * HY_WIDTH] * win
        hb = h[:, (2 * o + 1) * HY_WIDTH:(2 * o + 2) * HY_WIDTH] * win
        hb = jnp.where(lag == 0, 0.0, hb)
        sum_ref[:, o * HY_WIDTH:(o + 1) * HY_WIDTH] = (hf + hb).astype(sum_ref.dtype)
        diff_ref[:, o * HY_WIDTH:(o + 1) * HY_WIDTH] = (hf - hb).astype(diff_ref.dtype)


def hyena_filter_taps(l, w1, b1, w2, b2, w3, b3, w4, freq, tl=256):
    z, window = _filter_features(l)
    w1p = jnp.pad(w1, ((0, LANES - HY_EMB), (0, 0)))
    hid = HY_FILTER_HIDDEN
    full = lambda shape: pl.BlockSpec(shape, lambda i: (0,) * len(shape))
    nout = HY_ORDER * HY_WIDTH
    return pl.pallas_call(
        _filter_kernel,
        grid=(l // tl,),
        in_specs=[pl.BlockSpec((tl, LANES), lambda i: (i, 0)),
                  pl.BlockSpec((tl, HY_WIDTH), lambda i: (i, 0)),
                  full((LANES, hid)), full((1, hid)), full((hid, hid)), full((1, hid)),
                  full((hid, hid)), full((1, hid)), full((hid, 2 * nout)), full((1, hid))],
        out_specs=[pl.BlockSpec((tl, nout), lambda i: (i, 0)), pl.BlockSpec((tl, nout), lambda i: (i, 0))],
        out_shape=[jax.ShapeDtypeStruct((l, nout), BF16), jax.ShapeDtypeStruct((l, nout), BF16)],
        compiler_params=_cparams(("arbitrary",)),
        name="hyena_filter_taps",
    )(z, window, w1p, b1.reshape(1, hid), w2, b2.reshape(1, hid), w3, b3.reshape(1, hid), w4,
      freq.reshape(1, hid))


def _spectrum_kernel(f_ref, h_ref, o_ref):
    o_ref[...] = jnp.dot(f_ref[...], h_ref[...], preferred_element_type=F32).astype(o_ref.dtype)


def hyena_filter_spectra(flt, taps, tn=512):
    _, l, nout = taps.shape
    return pl.pallas_call(
        _spectrum_kernel,
        grid=(2, nout // tn),
        in_specs=[pl.BlockSpec((None, l, l), lambda p, j: (p, 0, 0)),
                  pl.BlockSpec((None, l, tn), lambda p, j: (p, 0, j))],
        out_specs=pl.BlockSpec((None, l, tn), lambda p, j: (p, 0, j)),
        out_shape=jax.ShapeDtypeStruct((2, l, nout), BF16),
        compiler_params=_cparams(("arbitrary", "arbitrary")),
        name="hyena_filter_spectra",
    )(flt, taps)


def _hyena_kernel(xv_ref, x1_ref, x2_ref, cw_ref, cb_ref, sym_ref, sp0_ref, sp1_ref, bias_ref, o_ref,
                  z_s, g_s, zb_s, pa_s, pb_s, *, rb):
    l, tc = xv_ref.shape
    scale = 2.0 / (2 * l)
    first = lax.broadcasted_iota(jnp.int32, (l, tc), 0) == 0
    last = lax.broadcasted_iota(jnp.int32, (l, tc), 0) == l - 1

    def short_conv(x_ref, part):
        x = x_ref[...]
        w = cw_ref[part]
        prev = jnp.where(first, 0.0, pltpu.roll(x, 1, 0))
        nxt = jnp.where(last, 0.0, pltpu.roll(x, l - 1, 0))
        return prev * w[0:1, :] + x * w[1:2, :] + nxt * w[2:3, :] + cb_ref[part]

    z_s[...] = short_conv(xv_ref, 0)
    for o, (gate_ref, part, sp_ref) in enumerate(((x1_ref, 1, sp0_ref), (x2_ref, 2, sp1_ref))):
        zb_s[...] = z_s[...].astype(BF16)
        g_s[...] = short_conv(gate_ref, part)
        for r0 in range(0, l, rb):
            a = jnp.dot(sym_ref[0, r0:r0 + rb, :], zb_s[...], preferred_element_type=F32)
            b = jnp.dot(sym_ref[1, r0:r0 + rb, :], zb_s[...], preferred_element_type=F32)
            hr = sp_ref[0, r0:r0 + rb, :].astype(F32)
            hs = sp_ref[1, r0:r0 + rb, :].astype(F32)
            pa_s[r0:r0 + rb, :] = (a * hr - b * hs).astype(BF16)
            pb_s[r0:r0 + rb, :] = (a * hs + b * hr).astype(BF16)
        for r0 in range(0, l, rb):
            y = (jnp.dot(sym_ref[0, r0:r0 + rb, :], pa_s[...], preferred_element_type=F32)
                 + jnp.dot(sym_ref[1, r0:r0 + rb, :], pb_s[...], preferred_element_type=F32)) * scale
            y = y + z_s[r0:r0 + rb, :] * bias_ref[o]
            z_s[r0:r0 + rb, :] = g_s[r0:r0 + rb, :] * y
    o_ref[...] = z_s[...].astype(o_ref.dtype)


def hyena_mixer(proj3, conv_w, conv_b, sym, spectra, hy_bias, tc=256, rb=1024):
    b, l, _ = proj3.shape
    nct = HY_WIDTH // tc
    base = (5 * HG_WIDTH) // tc

    def col(part):
        return pl.BlockSpec((None, l, tc), lambda ci, bi, part=part: (bi, 0, base + part * nct + ci))

    def spec_cols(o):
        return pl.BlockSpec((2, l, tc), lambda ci, bi, o=o: (0, 0, o * nct + ci))

    cw = conv_w.reshape(3, HY_ORDER + 1, HY_WIDTH).transpose(1, 0, 2)
    cb = conv_b.reshape(HY_ORDER + 1, 1, HY_WIDTH)
    return pl.pallas_call(
        functools.partial(_hyena_kernel, rb=rb),
        grid=(nct, b),
        in_specs=[col(0), col(1), col(2),
                  pl.BlockSpec((HY_ORDER + 1, 3, tc), lambda ci, bi: (0, 0, ci)),
                  pl.BlockSpec((HY_ORDER + 1, 1, tc), lambda ci, bi: (0, 0, ci)),
                  pl.BlockSpec((2, l, l), lambda ci, bi: (0, 0, 0), pipeline_mode=pl.Buffered(1)),
                  spec_cols(0), spec_cols(1),
                  pl.BlockSpec((HY_ORDER, 1, tc), lambda ci, bi: (0, 0, ci))],
        out_specs=pl.BlockSpec((None, l, tc), lambda ci, bi: (bi, 0, ci)),
        out_shape=jax.ShapeDtypeStruct((b, l, HY_WIDTH), BF16),
        scratch_shapes=[pltpu.VMEM((l, tc), F32), pltpu.VMEM((l, tc), F32), pltpu.VMEM((l, tc), BF16),
                        pltpu.VMEM((l, tc), BF16), pltpu.VMEM((l, tc), BF16)],
        compiler_params=_cparams(("arbitrary", "arbitrary")),
        name="hyena_mixer",
    )(proj3, proj3, proj3, cw, cb, sym, spectra, spectra, hy_bias.reshape(HY_ORDER, 1, HY_WIDTH))


def _outproj_router_kernel(hg_ref, hy_ref, x_ref, w_ref, hyg_ref, fg_ref, wr_ref, br_ref, tri_ref,
                           x2_ref, xn_ref, meta_ref, cnt_ref, carry_ref):
    @pl.when(pl.program_id(0) == 0)
    def _():
        carry_ref[...] = jnp.zeros_like(carry_ref)

    hy = hy_ref[...].astype(F32)
    hy = hy * lax.rsqrt(jnp.mean(hy * hy, axis=-1, keepdims=True) + NORM_EPS) * hyg_ref[...]
    m = (jnp.dot(hg_ref[...], w_ref[0:HG_WIDTH, :], preferred_element_type=F32)
         + jnp.dot(hy.astype(BF16), w_ref[HG_WIDTH:, :], preferred_element_type=F32))
    x2 = x_ref[...] + m
    x2_ref[...] = x2
    xn = x2 * lax.rsqrt(jnp.mean(x2 * x2, axis=-1, keepdims=True) + NORM_EPS) * fg_ref[...]
    xn_ref[...] = xn
    xn_hi = xn.astype(BF16)
    xn_lo = (xn - xn_hi.astype(F32)).astype(BF16)
    wr = wr_ref[...]
    wr_hi = wr.astype(BF16)
    wr_lo = (wr - wr_hi.astype(F32)).astype(BF16)
    logits = (jnp.dot(xn_hi, wr_hi, preferred_element_type=F32)
              + jnp.dot(xn_hi, wr_lo, preferred_element_type=F32)
              + jnp.dot(xn_lo, wr_hi, preferred_element_type=F32)) + br_ref[...]
    tm = logits.shape[0]
    lane = lax.broadcasted_iota(jnp.int32, (tm, LANES), 1).astype(F32)
    neg = jnp.float32(-jnp.inf)
    logits = jnp.where(lane < N_EXPERTS, logits, neg)
    tops, idxs = [], []
    for _ in range(TOP_K):
        mx = jnp.max(logits, axis=-1, keepdims=True)
        idx = jnp.min(jnp.where(logits == mx, lane, float(LANES)), axis=-1, keepdims=True)
        tops.append(mx)
        idxs.append(idx)
        logits = jnp.where(lane == idx, neg, logits)
    exps = [jnp.exp(t - tops[0]) for t in tops]
    denom = exps[0] + exps[1] + exps[2] + exps[3]
    onehot = jnp.zeros((tm, LANES), F32)
    for idx in idxs:
        onehot = onehot + jnp.where(lane == idx, 1.0, 0.0)
    cum = jnp.dot(tri_ref[...], onehot.astype(BF16), preferred_element_type=F32) + carry_ref[0:1, :]
    meta = jnp.zeros((tm, LANES), F32)
    for k in range(TOP_K):
        rank = jnp.sum(jnp.where(lane == idxs[k], cum, 0.0), axis=-1, keepdims=True)
        meta = jnp.where(lane == k, idxs[k], meta)
        meta = jnp.where(lane == TOP_K + k, rank, meta)
        meta = jnp.where(lane == 2 * TOP_K + k, exps[k] / denom, meta)
    meta_ref[...] = meta
    carry = carry_ref[...] + jnp.sum(onehot, axis=0, keepdims=True)
    carry_ref[...] = carry
    cnt_ref[...] = carry


def outproj_router(hg2d, hy2d, x2d, w_out_bf16, hy_gain, ffn_gain, w_router, b_router, tm=512):
    n, d = x2d.shape
    wr = jnp.pad(w_router, ((0, 0), (0, LANES - N_EXPERTS)))
    br = jnp.pad(b_router, (0, LANES - N_EXPERTS)).reshape(1, LANES)
    tri = jnp.asarray(np.tril(np.ones((tm, tm), np.float32), -1), BF16)
    row = lambda w: pl.BlockSpec((tm, w), lambda i: (i, 0))
    full = lambda shape, **kw: pl.BlockSpec(shape, lambda i: (0,) * len(shape), **kw)
    return pl.pallas_call(
        _outproj_router_kernel,
        grid=(n // tm,),
        in_specs=[row(HG_WIDTH), row(HY_WIDTH), row(d),
                  full((d, d), pipeline_mode=pl.Buffered(1)),
                  full((1, HY_WIDTH)), full((1, d)), full((d, LANES)), full((1, LANES)), full((tm, tm))],
        out_specs=[row(d), row(d), row(LANES), full((8, LANES))],
        out_shape=[jax.ShapeDtypeStruct((n, d), F32), jax.ShapeDtypeStruct((n, d), F32),
                   jax.ShapeDtypeStruct((n, LANES), F32), jax.ShapeDtypeStruct((8, LANES), F32)],
        scratch_shapes=[pltpu.VMEM((8, LANES), F32)],
        compiler_params=_cparams(("arbitrary",)),
        name="outproj_router",
    )(hg2d, hy2d, x2d, w_out_bf16, hy_gain.reshape(1, HY_WIDTH), ffn_gain.reshape(1, d), wr, br, tri)


def _slot_table_kernel(dest_ref, slot_ref):
    n_slots = slot_ref.shape[0]
    n_pairs = dest_ref.shape[0]

    def fill(i, carry):
        slot_ref[i] = 0
        return carry

    lax.fori_loop(0, n_slots, fill, 0, unroll=16)

    def scatter(a, carry):
        slot_ref[dest_ref[a]] = lax.shift_right_logical(a, int(math.log2(TOP_K)))
        return carry

    lax.fori_loop(0, n_pairs, scatter, 0, unroll=16)


def build_slot_table(dest_flat, n_slots):
    return pl.pallas_call(
        _slot_table_kernel,
        in_specs=[pl.BlockSpec(memory_space=pltpu.SMEM)],
        out_specs=pl.BlockSpec(memory_space=pltpu.SMEM),
        out_shape=jax.ShapeDtypeStruct((n_slots,), jnp.int32),
        name="build_slot_table",
    )(dest_flat)


GATHER_UNROLL = 8


def _row_gather_kernel(slot_ref, rows_ref, used_ref, x_hbm, o_ref, buf_ref, sems):
    t = pl.program_id(0)
    tr = buf_ref.shape[1]
    used = used_ref[0]

    def n_copies(tile):
        return ((rows_ref[tile] + GATHER_UNROLL - 1) // GATHER_UNROLL) * GATHER_UNROLL

    def issue(tile):
        b = tile % 2
        base = tile * tr

        def body(i, carry):
            for u in range(GATHER_UNROLL):
                r = i * GATHER_UNROLL + u
                pltpu.make_async_copy(x_hbm.at[pl.ds(slot_ref[base + r], 1)], buf_ref.at[b, pl.ds(r, 1)],
                                      sems.at[b]).start()
            return carry

        lax.fori_loop(0, n_copies(tile) // GATHER_UNROLL, body, 0)

    @pl.when(t == 0)
    def _():
        buf_ref[...] = jnp.zeros(buf_ref.shape, buf_ref.dtype)
        issue(t)

    @pl.when(t + 1 < used)
    def _():
        issue(t + 1)

    @pl.when(t < used)
    def _():
        b = t % 2
        nc = n_copies(t)

        @pl.when(nc > 0)
        def _():
            pltpu.make_async_copy(x_hbm.at[pl.ds(0, nc)], buf_ref.at[b, pl.ds(0, nc)], sems.at[b]).wait()

        row = lax.broadcasted_iota(jnp.int32, o_ref.shape, 0)
        o_ref[...] = jnp.where(row < rows_ref[t], buf_ref[b], 0.0).astype(o_ref.dtype)


def gather_rows(slot_tok, tile_rows, used_tiles, xn, n_slots, tr):
    n, d = xn.shape
    grid_spec = pltpu.PrefetchScalarGridSpec(
        num_scalar_prefetch=3,
        grid=(n_slots // tr,),
        in_specs=[pl.BlockSpec(memory_space=pl.ANY)],
        out_specs=pl.BlockSpec((tr, d), lambda t, slot, rows, used: (jnp.minimum(t, used[0] - 1), 0)),
        scratch_shapes=[pltpu.VMEM((2, tr, d), F32), pltpu.SemaphoreType.DMA((2,))],
    )
    return pl.pallas_call(
        _row_gather_kernel,
        grid_spec=grid_spec,
        out_shape=jax.ShapeDtypeStruct((n_slots, d), BF16),
        compiler_params=pltpu.CompilerParams(dimension_semantics=("arbitrary",), vmem_limit_bytes=VMEM_LIMIT,
                                             disable_bounds_checks=True),
        name="moe_gather_rows",
    )(slot_tok, tile_rows, used_tiles, xn)


def _expert_kernel(te_ref, tv_ref, tb_ref, x_ref, wg_ref, wl_ref, bg_ref, bl_ref, w2_ref, b2_ref, o_ref,
                   act_s, *, nf, blocks):
    j = pl.program_id(0)
    s = pl.program_id(1)
    valid = tv_ref[j]
    tf = wg_ref.shape[1]

    @pl.when(jnp.logical_and(valid > 0, s < nf))
    def _():
        for r0, rn in blocks:
            @pl.when(r0 < valid)
            def _(r0=r0, rn=rn):
                x = x_ref[r0:r0 + rn, :]
                glu = jnp.dot(x, wg_ref[...].astype(BF16), preferred_element_type=F32) + bg_ref[...]
                lin = jnp.dot(x, wl_ref[...].astype(BF16), preferred_element_type=F32) + bl_ref[...]
                glu = jnp.minimum(glu, SWIGLU_LIMIT)
                lin = jnp.clip(lin, -SWIGLU_LIMIT, SWIGLU_LIMIT)
                act = glu * jax.nn.sigmoid(SWIGLU_ALPHA * glu) * (lin + 1.0)
                act_s[s, r0:r0 + rn, :] = act.astype(BF16)

    @pl.when(jnp.logical_and(valid > 0, s >= nf))
    def _():
        for r0, rn in blocks:
            @pl.when(r0 < valid)
            def _(r0=r0, rn=rn):
                y = b2_ref[...] + jnp.dot(act_s[0, r0:r0 + rn, :], w2_ref[0:tf, :].astype(BF16),
                                          preferred_element_type=F32)
                for f in range(1, nf):
                    y = y + jnp.dot(act_s[f, r0:r0 + rn, :], w2_ref[f * tf:(f + 1) * tf, :].astype(BF16),
                                    preferred_element_type=F32)
                o_ref[r0:r0 + rn, :] = y

            @pl.when(r0 >= valid)
            def _(r0=r0, rn=rn):
                o_ref[r0:r0 + rn, :] = jnp.zeros((rn, o_ref.shape[1]), o_ref.dtype)


def expert_ffn(tile_expert, tile_valid, tile_block, xs, w1, b1, w2, b2, tm, blocks, tf=512):
    n_slots, d = xs.shape
    nf = D_FF // tf
    nn = d // tf
    assert nn == nf
    b1r = b1.reshape(N_EXPERTS, 1, 2 * D_FF)
    b2r = b2.reshape(N_EXPERTS, 1, d)

    def w1_blk(j, s, tv):
        return jnp.where(tv[j] > 0, jnp.minimum(s, nf - 1), nf - 1)

    def w2_blk(j, s, tv):
        return jnp.where(tv[j] > 0, jnp.maximum(s - nf, 0), nn - 1)

    grid_spec = pltpu.PrefetchScalarGridSpec(
        num_scalar_prefetch=3,
        grid=(n_slots // tm, nf + nn),
        in_specs=[
            pl.BlockSpec((tm, d), lambda j, s, te, tv, tb: (tb[j], 0)),
            pl.BlockSpec((None, d, tf), lambda j, s, te, tv, tb: (te[j], 0, w1_blk(j, s, tv))),
            pl.BlockSpec((None, d, tf), lambda j, s, te, tv, tb: (te[j], 0, nf + w1_blk(j, s, tv))),
            pl.BlockSpec((None, 1, tf), lambda j, s, te, tv, tb: (te[j], 0, w1_blk(j, s, tv))),
            pl.BlockSpec((None, 1, tf), lambda j, s, te, tv, tb: (te[j], 0, nf + w1_blk(j, s, tv))),
            pl.BlockSpec((None, D_FF, tf), lambda j, s, te, tv, tb: (te[j], 0, w2_blk(j, s, tv))),
            pl.BlockSpec((None, 1, tf), lambda j, s, te, tv, tb: (te[j], 0, w2_blk(j, s, tv))),
        ],
        out_specs=pl.BlockSpec((tm, tf), lambda j, s, te, tv, tb: (tb[j], w2_blk(j, s, tv))),
        scratch_shapes=[pltpu.VMEM((nf, tm, tf), BF16)],
    )
    return pl.pallas_call(
        functools.partial(_expert_kernel, nf=nf, blocks=blocks),
        grid_spec=grid_spec,
        out_shape=jax.ShapeDtypeStruct((n_slots, d), F32),
        compiler_params=_cparams(("arbitrary", "arbitrary")),
        name="moe_expert_ffn",
    )(tile_expert, tile_valid, tile_block, xs, w1, w1, b1r, b1r, w2, b2r)


def _combine_kernel(dest_ref, ys_hbm, x2_ref, meta_ref, gain_ref, o_ref, buf_ref, sems):
    t = pl.program_id(0)
    nt = pl.num_programs(0)
    tt = x2_ref.shape[0]

    def issue(tile):
        b = tile % 2
        base = tile * (tt * TOP_K)

        def body(r, carry):
            for k in range(TOP_K):
                pltpu.make_async_copy(ys_hbm.at[pl.ds(dest_ref[base + r * TOP_K + k], 1)],
                                      buf_ref.at[b, k, pl.ds(r, 1)], sems.at[b]).start()
            return carry

        lax.fori_loop(0, tt, body, 0, unroll=4)

    @pl.when(t == 0)
    def _():
        issue(t)

    @pl.when(t + 1 < nt)
    def _():
        issue(t + 1)

    b = t % 2
    for k in range(TOP_K):
        pltpu.make_async_copy(ys_hbm.at[pl.ds(0, tt)], buf_ref.at[b, k], sems.at[b]).wait()
    meta = meta_ref[...]
    y = x2_ref[...]
    for k in range(TOP_K):
        y = y + meta[:, 2 * TOP_K + k:2 * TOP_K + k + 1] * buf_ref[b, k]
    o_ref[...] = y * lax.rsqrt(jnp.mean(y * y, axis=-1, keepdims=True) + NORM_EPS) * gain_ref[...]


def combine_final(dest_flat, ys, x2, meta, final_gain, tt=128):
    n, d = x2.shape
    grid_spec = pltpu.PrefetchScalarGridSpec(
        num_scalar_prefetch=1,
        grid=(n // tt,),
        in_specs=[pl.BlockSpec(memory_space=pl.ANY),
                  pl.BlockSpec((tt, d), lambda t, dest: (t, 0)),
                  pl.BlockSpec((tt, LANES), lambda t, dest: (t, 0)),
                  pl.BlockSpec((1, d), lambda t, dest: (0, 0))],
        out_specs=pl.BlockSpec((tt, d), lambda t, dest: (t, 0)),
        scratch_shapes=[pltpu.VMEM((2, TOP_K, tt, d), F32), pltpu.SemaphoreType.DMA((2,))],
    )
    return pl.pallas_call(
        _combine_kernel,
        grid_spec=grid_spec,
        out_shape=jax.ShapeDtypeStruct((n, d), F32),
        compiler_params=pltpu.CompilerParams(dimension_semantics=("arbitrary",), vmem_limit_bytes=VMEM_LIMIT,
                                             disable_bounds_checks=True),
        name="moe_combine_final",
    )(dest_flat, ys, x2, meta, final_gain.reshape(1, d))


MOE_TILE = 1280
MOE_ROW_BLOCKS = ((0, 512), (512, 512), (1024, 128), (1152, 128))
MOE_GATHER_TILE = 256


def moe_ffn_final(x2, xn, meta, counts, w1, b1, w2, b2, final_gain, tm=MOE_TILE):
    n, d = x2.shape
    a = n * TOP_K
    n_tiles = -(-a // tm) + N_EXPERTS
    n_slots = n_tiles * tm
    idx = meta[:, 0:TOP_K].astype(jnp.int32)
    rank = meta[:, TOP_K:2 * TOP_K].astype(jnp.int32)
    cnt = counts[0, :N_EXPERTS].astype(jnp.int32)
    tiles_e = (cnt + tm - 1) // tm
    tile_end = jnp.cumsum(tiles_e)
    tile_start = tile_end - tiles_e
    used = tile_end[-1]
    dest = (tile_start * tm)[idx] + rank
    dest_flat = dest.reshape(a)
    tj = jnp.arange(n_tiles, dtype=jnp.int32)
    te = jnp.minimum(jnp.sum(tj[:, None] >= tile_end[None, :], axis=1), N_EXPERTS - 1).astype(jnp.int32)
    last_e = te[jnp.maximum(used - 1, 0)]
    tile_expert = jnp.where(tj < used, te, last_e).astype(jnp.int32)
    tile_valid = jnp.where(tj < used, jnp.clip(cnt[te] - (tj - tile_start[te]) * tm, 0, tm), 0).astype(jnp.int32)
    tile_block = jnp.maximum(jnp.minimum(tj, used - 1), 0).astype(jnp.int32)

    slot_tok = build_slot_table(dest_flat, n_slots)
    tr = MOE_GATHER_TILE
    per = tm // tr
    gj = jnp.arange(n_tiles * per, dtype=jnp.int32)
    gather_rows_valid = jnp.clip(tile_valid[gj // per] - (gj % per) * tr, 0, tr).astype(jnp.int32)
    used_gather = jnp.maximum(used * per, 1).astype(jnp.int32).reshape(1)
    xs = gather_rows(slot_tok, gather_rows_valid, used_gather, xn, n_slots, tr)
    ys = expert_ffn(tile_expert, tile_valid, tile_block, xs, w1, b1, w2, b2, tm, MOE_ROW_BLOCKS)
    return combine_final(dest_flat, ys, x2, meta, final_gain)


def kernel(x, norm_mix_gain, w_in, hgrn_lb_logits, hgrn_norm_gain, hy_conv_w, hy_conv_b, hy_filt_w1,
           hy_filt_b1, hy_filt_w2, hy_filt_b2, hy_filt_w3, hy_filt_b3, hy_filt_w4, hy_filt_freq, hy_bias,
           hy_norm_gain, w_out, norm_ffn_gain, w_router, b_router, moe_w1, moe_b1, moe_w2, moe_b2,
           final_norm_gain):
    b, l, d = x.shape
    n = b * l
    x2d = x.reshape(n, d)
    proj = in_projection(x2d, norm_mix_gain[0], w_in[0].astype(BF16))
    proj3 = proj.reshape(b, l, D_IN)
    hg = hgrn2_mixer(proj3, hgrn_lb_logits, hgrn_norm_gain[0])

    sym, flt = (jnp.asarray(tab).astype(BF16) for tab in _dft_tables(l))
    tap_sum, tap_diff = hyena_filter_taps(l, hy_filt_w1[0], hy_filt_b1[0], hy_filt_w2[0], hy_filt_b2[0],
                                          hy_filt_w3[0], hy_filt_b3[0], hy_filt_w4[0], hy_filt_freq[0])
    spectra = hyena_filter_spectra(flt, jnp.stack([tap_sum, tap_diff]))
    hy = hyena_mixer(proj3, hy_conv_w[0], hy_conv_b[0], sym, spectra, hy_bias[0])

    x2, xn, meta, counts = outproj_router(hg.reshape(n, HG_WIDTH), hy.reshape(n, HY_WIDTH), x2d,
                                          w_out[0].astype(BF16), hy_norm_gain[0], norm_ffn_gain[0],
                                          w_router[0], b_router[0])
    out = moe_ffn_final(x2, xn, meta, counts, moe_w1[0], moe_b1[0], moe_w2[0], moe_b2[0], final_norm_gain)
    return out.reshape(b, l, d)
```

```python
import functools
import math

import numpy as np
import jax
import jax.numpy as jnp
from jax import lax
from jax.experimental import pallas as pl
from jax.experimental.pallas import tpu as pltpu

F32 = jnp.float32
BF16 = jnp.bfloat16

D_MODEL = 2048
HG_WIDTH = 1024
HG_HEAD_DIM = 128
HG_HEADS = HG_WIDTH // HG_HEAD_DIM
HY_WIDTH = 1024
HY_ORDER = 2
HY_EMB = 33
HY_BANDS = (HY_EMB - 1) // 2
HY_FILTER_HIDDEN = 64
HY_FAST_DECAY_PCT = 0.3
HY_SLOW_DECAY_PCT = 1.5
HY_DECAY_TARGET = 1e-2
N_EXPERTS = 32
TOP_K = 4
D_FF = D_MODEL
SWIGLU_ALPHA = 1.702
SWIGLU_LIMIT = 7.0
NORM_EPS = 1e-5
D_IN = 5 * HG_WIDTH + (HY_ORDER + 1) * HY_WIDTH

LANES = 128
GLA_CHUNK = 128
GLA_LEVELS = (64, 32, 16, 8, 4, 2, 1)
VMEM_LIMIT = 56 * 1024 * 1024


def _cparams(sem, vmem=VMEM_LIMIT):
    return pltpu.CompilerParams(dimension_semantics=sem, vmem_limit_bytes=vmem)


def _inproj_kernel(x_ref, g_ref, w_ref, o_ref, h_s):
    @pl.when(pl.program_id(1) == 0)
    def _():
        x = x_ref[...]
        ms = jnp.mean(x * x, axis=-1, keepdims=True)
        h_s[...] = (x * lax.rsqrt(ms + NORM_EPS) * g_ref[...]).astype(BF16)

    o_ref[...] = jnp.dot(h_s[...], w_ref[...], preferred_element_type=F32)


def in_projection(x2d, gain, w_bf16, tm=1024, tn=1024):
    n, d = x2d.shape
    dout = w_bf16.shape[1]
    return pl.pallas_call(
        _inproj_kernel,
        grid=(n // tm, dout // tn),
        in_specs=[
            pl.BlockSpec((tm, d), lambda i, j: (i, 0)),
            pl.BlockSpec((1, d), lambda i, j: (0, 0)),
            pl.BlockSpec((d, tn), lambda i, j: (0, j)),
        ],
        out_specs=pl.BlockSpec((tm, tn), lambda i, j: (i, j)),
        out_shape=jax.ShapeDtypeStruct((n, dout), F32),
        scratch_shapes=[pltpu.VMEM((tm, d), BF16)],
        compiler_params=_cparams(("arbitrary", "arbitrary")),
        name="in_projection",
    )(x2d, gain.reshape(1, d), w_bf16)


def _gla_constants():
    c = GLA_CHUNK
    t = np.arange(c)[:, None]
    r = np.arange(c)[None, :]
    fwd = [r <= t, r > t]
    bwd = [r >= t, r < t]
    for m in GLA_LEVELS:
        pos = t % (2 * m)
        mid = t - pos + m
        second = pos >= m
        fwd.append(np.where(second, (r >= mid) & (r <= t), (r > t) & (r < mid)))
        bwd.append(np.where(second, (r >= mid) & (r < t), (r >= t) & (r < mid)))
    x = t ^ r
    lv = np.full((c, c), -1, np.int32)
    for j in range(int(math.log2(c))):
        lv = np.where((x >> j) == 1, j, lv)
    mf = jnp.asarray(np.concatenate(fwd, 0).astype(np.float32), BF16)
    mb = jnp.asarray(np.concatenate(bwd, 0).astype(np.float32), BF16)
    return mf, mb, jnp.asarray(lv, jnp.int32)


def _hgrn_kernel(q_ref, ff_ref, fb_ref, i_ref, g_ref, lb_ref, gain_ref, mf_ref, mb_ref, lv_ref,
                 o_ref, acc_ref, qb_ref, kb_ref, db_ref):
    c = GLA_CHUNK
    n_chunks = q_ref.shape[0] // c
    lb = lb_ref[...]
    l0, l1 = lb[0], lb[1]
    mx = jnp.maximum(l0, l1)
    e0 = jnp.exp(l0 - mx)
    e1 = jnp.exp(l1 - mx)
    p0 = e0 / (e0 + e1)
    lb_f = p0[0:1, :]
    lb_b = p0[1:2, :]
    lv = lv_ref[...]
    row = lax.broadcasted_iota(jnp.int32, (c, LANES), 0)
    nt = (((1,), (1,)), ((), ()))

    def gates(z, lower):
        f = lower + (1.0 - lower) * jax.nn.sigmoid(z)
        return 1.0 - f, jnp.log(f)

    def exponents(m_ref, g):
        g_hi = g.astype(BF16)
        g_lo = (g - g_hi.astype(F32)).astype(BF16)
        m = m_ref[...]
        return (jnp.dot(m, g_hi, preferred_element_type=F32)
                + jnp.dot(m, g_lo, preferred_element_type=F32))

    def fwd_body(ci, st):
        rows = pl.ds(pl.multiple_of(ci * c, c), c)
        qr = q_ref[rows, :]
        q = qr * jax.nn.sigmoid(qr)
        v = i_ref[rows, :]
        kf, gf = gates(ff_ref[rows, :], lb_f)
        kb, gb = gates(fb_ref[rows, :], lb_b)
        ef_all = exponents(mf_ref, gf)
        eb_all = exponents(mb_ref, gb)
        scores = jnp.zeros((c, c), F32)
        for li, m in enumerate(GLA_LEVELS):
            ef = ef_all[(2 + li) * c:(3 + li) * c]
            eb = eb_all[(2 + li) * c:(3 + li) * c]
            second = (row & m) != 0
            a = q * jnp.exp(jnp.where(second, ef, eb))
            b = jnp.where(second, kb, kf) * jnp.exp(jnp.where(second, eb, ef))
            s = lax.dot_general(a.astype(BF16), b.astype(BF16), nt, preferred_element_type=F32)
            scores = jnp.where(lv == int(math.log2(m)), s, scores)
        v_bf = v.astype(BF16)
        o = jnp.dot(scores.astype(BF16), v_bf, preferred_element_type=F32)
        o = o + jnp.sum(q * (kf + kb), axis=-1, keepdims=True) * v
        b_inc = ef_all[0:c]
        q_dec = (q * jnp.exp(b_inc)).astype(BF16)
        o = o + lax.dot_general(q_dec, st.astype(BF16), nt, preferred_element_type=F32)
        k_dec = (kf * jnp.exp(ef_all[c:2 * c])).astype(BF16)
        vt_bf = v.T.astype(BF16)
        st = st * jnp.exp(b_inc[c - 1:c, :]) + jnp.dot(vt_bf, k_dec, preferred_element_type=F32)
        acc_ref[rows, :] = o
        bb = eb_all[0:c]
        qb_ref[rows, :] = (q * jnp.exp(bb)).astype(BF16)
        kb_ref[rows, :] = (kb * jnp.exp(eb_all[c:2 * c])).astype(BF16)
        db_ref[ci] = jnp.broadcast_to(jnp.exp(bb[0:1, :]), (8, LANES))
        return st

    lax.fori_loop(0, n_chunks, fwd_body, jnp.zeros((c, c), F32), unroll=4)

    gain = gain_ref[...]

    def bwd_body(i, st):
        ci = n_chunks - 1 - i
        rows = pl.ds(pl.multiple_of(ci * c, c), c)
        o = acc_ref[rows, :] + lax.dot_general(qb_ref[rows, :], st.astype(BF16), nt,
                                               preferred_element_type=F32)
        vt_bf = i_ref[rows, :].T.astype(BF16)
        st = st * db_ref[ci][0:1, :] + jnp.dot(vt_bf, kb_ref[rows, :], preferred_element_type=F32)
        o = o * lax.rsqrt(jnp.mean(o * o, axis=-1, keepdims=True) + NORM_EPS) * gain
        gr = g_ref[rows, :]
        o_ref[rows, :] = (o * (gr * jax.nn.sigmoid(gr))).astype(o_ref.dtype)
        return st

    lax.fori_loop(0, n_chunks, bwd_body, jnp.zeros((c, c), F32), unroll=4)


def hgrn2_mixer(proj3, lb_logits, norm_gain):
    b, l, _ = proj3.shape
    h, dh = HG_HEADS, HG_HEAD_DIM
    mf, mb, lv = _gla_constants()
    nblk = mf.shape[0]

    def col(off):
        return pl.BlockSpec((None, l, dh), lambda bi, hi, off=off: (bi, 0, off + hi))

    const2 = lambda bi, hi: (0, 0)
    return pl.pallas_call(
        _hgrn_kernel,
        grid=(b, h),
        in_specs=[col(0), col(h), col(2 * h), col(3 * h), col(4 * h),
                  pl.BlockSpec((2, 2, dh), lambda bi, hi: (0, 0, hi)),
                  pl.BlockSpec((1, dh), lambda bi, hi: (0, hi)),
                  pl.BlockSpec((nblk, GLA_CHUNK), const2),
                  pl.BlockSpec((nblk, GLA_CHUNK), const2),
                  pl.BlockSpec((GLA_CHUNK, GLA_CHUNK), const2)],
        out_specs=pl.BlockSpec((None, l, dh), lambda bi, hi: (bi, 0, hi)),
        out_shape=jax.ShapeDtypeStruct((b, l, HG_WIDTH), BF16),
        scratch_shapes=[pltpu.VMEM((l, dh), F32), pltpu.VMEM((l, dh), BF16), pltpu.VMEM((l, dh), BF16),
                        pltpu.VMEM((l // GLA_CHUNK, 8, LANES), F32)],
        compiler_params=_cparams(("arbitrary", "arbitrary")),
        name="hgrn2_mixer",
    )(proj3, proj3, proj3, proj3, proj3, lb_logits, norm_gain.reshape(1, HG_WIDTH), mf, mb, lv)


@functools.lru_cache(maxsize=None)
def _dft_tables(l):
    n = 2 * l
    k2 = 2 * np.arange(l, dtype=np.int64)[:, None] + 1
    m1 = np.arange(l, dtype=np.int64)[None, :]
    ang_s = ((k2 * (2 * m1 + 1)) % (4 * n)).astype(np.float64) * (2.0 * math.pi / (4 * n))
    ang_f = ((k2 * m1) % (2 * n)).astype(np.float64) * (2.0 * math.pi / (2 * n))
    sym = np.stack([np.cos(ang_s), np.sin(ang_s)]).astype(np.float32)
    flt = np.stack([np.cos(ang_f), np.sin(ang_f)]).astype(np.float32)
    return sym, flt


def _filter_features(l):
    pos = jnp.arange(l, dtype=F32)
    t = pos / max(l - 1, 1)
    bands = jnp.linspace(1e-4, HY_BANDS - 1, HY_BANDS, dtype=F32)
    ang = (2.0 * math.pi / l) * pos[:, None] * bands[None, :]
    z = jnp.concatenate([t[:, None], jnp.cos(ang), -jnp.sin(ang)], axis=-1)
    z = jnp.pad(z, ((0, 0), (0, LANES - HY_EMB)))
    min_decay = math.log(HY_DECAY_TARGET) / HY_FAST_DECAY_PCT
    max_decay = math.log(HY_DECAY_TARGET) / HY_SLOW_DECAY_PCT
    deltas = jnp.abs(jnp.linspace(min_decay, max_decay, HY_WIDTH, dtype=F32))
    window = jnp.exp(-t[:, None] * deltas[None, :])
    return z, window


def _filter_kernel(z_ref, win_ref, w1_ref, b1_ref, w2_ref, b2_ref, w3_ref, b3_ref, w4_ref, fr_ref,
                   sum_ref, diff_ref):
    hp = lax.Precision.HIGHEST
    fr = fr_ref[...]
    h = jnp.sin(fr * (jnp.dot(z_ref[...], w1_ref[...], precision=hp, preferred_element_type=F32) + b1_ref[...]))
    h = jnp.sin(fr * (jnp.dot(h, w2_ref[...], precision=hp, preferred_element_type=F32) + b2_ref[...]))
    h = jnp.sin(fr * (jnp.dot(h, w3_ref[...], precision=hp, preferred_element_type=F32) + b3_ref[...]))
    h = jnp.dot(h, w4_ref[...], precision=hp, preferred_element_type=F32)
    win = win_ref[...]
    tl = h.shape[0]
    lag = pl.program_id(0) * tl + lax.broadcasted_iota(jnp.int32, (tl, HY_WIDTH), 0)
    for o in range(HY_ORDER):
        hf = h[:, (2 * o) * HY_WIDTH:(2 * o + 1) * HY_WIDTH] * win
        hb = h[:, (2 * o + 1) * HY_WIDTH:(2 * o + 2) * HY_WIDTH] * win
        hb = jnp.where(lag == 0, 0.0, hb)
        sum_ref[:, o * HY_WIDTH:(o + 1) * HY_WIDTH] = (hf + hb).astype(sum_ref.dtype)
        diff_ref[:, o * HY_WIDTH:(o + 1) * HY_WIDTH] = (hf - hb).astype(diff_ref.dtype)


def hyena_filter_taps(l, w1, b1, w2, b2, w3, b3, w4, freq, tl=256):
    z, window = _filter_features(l)
    w1p = jnp.pad(w1, ((0, LANES - HY_EMB), (0, 0)))
    hid = HY_FILTER_HIDDEN
    full = lambda shape: pl.BlockSpec(shape, lambda i: (0,) * len(shape))
    nout = HY_ORDER * HY_WIDTH
    return pl.pallas_call(
        _filter_kernel,
        grid=(l // tl,),
        in_specs=[pl.BlockSpec((tl, LANES), lambda i: (i, 0)),
                  pl.BlockSpec((tl, HY_WIDTH), lambda i: (i, 0)),
                  full((LANES, hid)), full((1, hid)), full((hid, hid)), full((1, hid)),
                  full((hid, hid)), full((1, hid)), full((hid, 2 * nout)), full((1, hid))],
        out_specs=[pl.BlockSpec((tl, nout), lambda i: (i, 0)), pl.BlockSpec((tl, nout), lambda i: (i, 0))],
        out_shape=[jax.ShapeDtypeStruct((l, nout), BF16), jax.ShapeDtypeStruct((l, nout), BF16)],
        compiler_params=_cparams(("arbitrary",)),
        name="hyena_filter_taps",
    )(z, window, w1p, b1.reshape(1, hid), w2, b2.reshape(1, hid), w3, b3.reshape(1, hid), w4,
      freq.reshape(1, hid))


def _spectrum_kernel(f_ref, h_ref, o_ref):
    o_ref[...] = jnp.dot(f_ref[...], h_ref[...], preferred_element_type=F32).astype(o_ref.dtype)


def hyena_filter_spectra(flt, taps, tn=512):
    _, l, nout = taps.shape
    return pl.pallas_call(
        _spectrum_kernel,
        grid=(2, nout // tn),
        in_specs=[pl.BlockSpec((None, l, l), lambda p, j: (p, 0, 0)),
                  pl.BlockSpec((None, l, tn), lambda p, j: (p, 0, j))],
        out_specs=pl.BlockSpec((None, l, tn), lambda p, j: (p, 0, j)),
        out_shape=jax.ShapeDtypeStruct((2, l, nout), BF16),
        compiler_params=_cparams(("arbitrary", "arbitrary")),
        name="hyena_filter_spectra",
    )(flt, taps)


def _hyena_kernel(xv_ref, x1_ref, x2_ref, cw_ref, cb_ref, sym_ref, sp0_ref, sp1_ref, bias_ref, o_ref,
                  z_s, g_s, zb_s, pa_s, pb_s, *, rb):
    l, tc = xv_ref.shape
    scale = 2.0 / (2 * l)
    first = lax.broadcasted_iota(jnp.int32, (l, tc), 0) == 0
    last = lax.broadcasted_iota(jnp.int32, (l, tc), 0) == l - 1

    def short_conv(x_ref, part):
        x = x_ref[...]
        w = cw_ref[part]
        prev = jnp.where(first, 0.0, pltpu.roll(x, 1, 0))
        nxt = jnp.where(last, 0.0, pltpu.roll(x, l - 1, 0))
        return prev * w[0:1, :] + x * w[1:2, :] + nxt * w[2:3, :] + cb_ref[part]

    z_s[...] = short_conv(xv_ref, 0)
    for o, (gate_ref, part, sp_ref) in enumerate(((x1_ref, 1, sp0_ref), (x2_ref, 2, sp1_ref))):
        zb_s[...] = z_s[...].astype(BF16)
        g_s[...] = short_conv(gate_ref, part)
        for r0 in range(0, l, rb):
            a = jnp.dot(sym_ref[0, r0:r0 + rb, :], zb_s[...], preferred_element_type=F32)
            b = jnp.dot(sym_ref[1, r0:r0 + rb, :], zb_s[...], preferred_element_type=F32)
            hr = sp_ref[0, r0:r0 + rb, :].astype(F32)
            hs = sp_ref[1, r0:r0 + rb, :].astype(F32)
            pa_s[r0:r0 + rb, :] = (a * hr - b * hs).astype(BF16)
            pb_s[r0:r0 + rb, :] = (a * hs + b * hr).astype(BF16)
        for r0 in range(0, l, rb):
            y = (jnp.dot(sym_ref[0, r0:r0 + rb, :], pa_s[...], preferred_element_type=F32)
                 + jnp.dot(sym_ref[1, r0:r0 + rb, :], pb_s[...], preferred_element_type=F32)) * scale
            y = y + z_s[r0:r0 + rb, :] * bias_ref[o]
            z_s[r0:r0 + rb, :] = g_s[r0:r0 + rb, :] * y
    o_ref[...] = z_s[...].astype(o_ref.dtype)


def hyena_mixer(proj3, conv_w, conv_b, sym, spectra, hy_bias, tc=256, rb=1024):
    b, l, _ = proj3.shape
    nct = HY_WIDTH // tc
    base = (5 * HG_WIDTH) // tc

    def col(part):
        return pl.BlockSpec((None, l, tc), lambda ci, bi, part=part: (bi, 0, base + part * nct + ci))

    def spec_cols(o):
        return pl.BlockSpec((2, l, tc), lambda ci, bi, o=o: (0, 0, o * nct + ci))

    cw = conv_w.reshape(3, HY_ORDER + 1, HY_WIDTH).transpose(1, 0, 2)
    cb = conv_b.reshape(HY_ORDER + 1, 1, HY_WIDTH)
    return pl.pallas_call(
        functools.partial(_hyena_kernel, rb=rb),
        grid=(nct, b),
        in_specs=[col(0), col(1), col(2),
                  pl.BlockSpec((HY_ORDER + 1, 3, tc), lambda ci, bi: (0, 0, ci)),
                  pl.BlockSpec((HY_ORDER + 1, 1, tc), lambda ci, bi: (0, 0, ci)),
                  pl.BlockSpec((2, l, l), lambda ci, bi: (0, 0, 0), pipeline_mode=pl.Buffered(1)),
                  spec_cols(0), spec_cols(1),
                  pl.BlockSpec((HY_ORDER, 1, tc), lambda ci, bi: (0, 0, ci))],
        out_specs=pl.BlockSpec((None, l, tc), lambda ci, bi: (bi, 0, ci)),
        out_shape=jax.ShapeDtypeStruct((b, l, HY_WIDTH), BF16),
        scratch_shapes=[pltpu.VMEM((l, tc), F32), pltpu.VMEM((l, tc), F32), pltpu.VMEM((l, tc), BF16),
                        pltpu.VMEM((l, tc), BF16), pltpu.VMEM((l, tc), BF16)],
        compiler_params=_cparams(("arbitrary", "arbitrary")),
        name="hyena_mixer",
    )(proj3, proj3, proj3, cw, cb, sym, spectra, spectra, hy_bias.reshape(HY_ORDER, 1, HY_WIDTH))


def _outproj_router_kernel(hg_ref, hy_ref, x_ref, w_ref, hyg_ref, fg_ref, wr_ref, br_ref, tri_ref,
                           x2_ref, xn_ref, meta_ref, cnt_ref, carry_ref):
    @pl.when(pl.program_id(0) == 0)
    def _():
        carry_ref[...] = jnp.zeros_like(carry_ref)

    hy = hy_ref[...].astype(F32)
    hy = hy * lax.rsqrt(jnp.mean(hy * hy, axis=-1, keepdims=True) + NORM_EPS) * hyg_ref[...]
    m = (jnp.dot(hg_ref[...], w_ref[0:HG_WIDTH, :], preferred_element_type=F32)
         + jnp.dot(hy.astype(BF16), w_ref[HG_WIDTH:, :], preferred_element_type=F32))
    x2 = x_ref[...] + m
    x2_ref[...] = x2
    xn = x2 * lax.rsqrt(jnp.mean(x2 * x2, axis=-1, keepdims=True) + NORM_EPS) * fg_ref[...]
    xn_ref[...] = xn
    xn_hi = xn.astype(BF16)
    xn_lo = (xn - xn_hi.astype(F32)).astype(BF16)
    wr = wr_ref[...]
    wr_hi = wr.astype(BF16)
    wr_lo = (wr - wr_hi.astype(F32)).astype(BF16)
    logits = (jnp.dot(xn_hi, wr_hi, preferred_element_type=F32)
              + jnp.dot(xn_hi, wr_lo, preferred_element_type=F32)
              + jnp.dot(xn_lo, wr_hi, preferred_element_type=F32)) + br_ref[...]
    tm = logits.shape[0]
    lane = lax.broadcasted_iota(jnp.int32, (tm, LANES), 1).astype(F32)
    neg = jnp.float32(-jnp.inf)
    logits = jnp.where(lane < N_EXPERTS, logits, neg)
    tops, idxs = [], []
    for _ in range(TOP_K):
        mx = jnp.max(logits, axis=-1, keepdims=True)
        idx = jnp.min(jnp.where(logits == mx, lane, float(LANES)), axis=-1, keepdims=True)
        tops.append(mx)
        idxs.append(idx)
        logits = jnp.where(lane == idx, neg, logits)
    exps = [jnp.exp(t - tops[0]) for t in tops]
    denom = exps[0] + exps[1] + exps[2] + exps[3]
    onehot = jnp.zeros((tm, LANES), F32)
    for idx in idxs:
        onehot = onehot + jnp.where(lane == idx, 1.0, 0.0)
    cum = jnp.dot(tri_ref[...], onehot.astype(BF16), preferred_element_type=F32) + carry_ref[0:1, :]
    meta = jnp.zeros((tm, LANES), F32)
    for k in range(TOP_K):
        rank = jnp.sum(jnp.where(lane == idxs[k], cum, 0.0), axis=-1, keepdims=True)
        meta = jnp.where(lane == k, idxs[k], meta)
        meta = jnp.where(lane == TOP_K + k, rank, meta)
        meta = jnp.where(lane == 2 * TOP_K + k, exps[k] / denom, meta)
    meta_ref[...] = meta
    carry = carry_ref[...] + jnp.sum(onehot, axis=0, keepdims=True)
    carry_ref[...] = carry
    cnt_ref[...] = carry


def outproj_router(hg2d, hy2d, x2d, w_out_bf16, hy_gain, ffn_gain, w_router, b_router, tm=512):
    n, d = x2d.shape
    wr = jnp.pad(w_router, ((0, 0), (0, LANES - N_EXPERTS)))
    br = jnp.pad(b_router, (0, LANES - N_EXPERTS)).reshape(1, LANES)
    tri = jnp.asarray(np.tril(np.ones((tm, tm), np.float32), -1), BF16)
    row = lambda w: pl.BlockSpec((tm, w), lambda i: (i, 0))
    full = lambda shape, **kw: pl.BlockSpec(shape, lambda i: (0,) * len(shape), **kw)
    return pl.pallas_call(
        _outproj_router_kernel,
        grid=(n // tm,),
        in_specs=[row(HG_WIDTH), row(HY_WIDTH), row(d),
                  full((d, d), pipeline_mode=pl.Buffered(1)),
                  full((1, HY_WIDTH)), full((1, d)), full((d, LANES)), full((1, LANES)), full((tm, tm))],
        out_specs=[row(d), row(d), row(LANES), full((8, LANES))],
        out_shape=[jax.ShapeDtypeStruct((n, d), F32), jax.ShapeDtypeStruct((n, d), F32),
                   jax.ShapeDtypeStruct((n, LANES), F32), jax.ShapeDtypeStruct((8, LANES), F32)],
        scratch_shapes=[pltpu.VMEM((8, LANES), F32)],
        compiler_params=_cparams(("arbitrary",)),
        name="outproj_router",
    )(hg2d, hy2d, x2d, w_out_bf16, hy_gain.reshape(1, HY_WIDTH), ffn_gain.reshape(1, d), wr, br, tri)


def _slot_table_kernel(dest_ref, slot_ref):
    n_slots = slot_ref.shape[0]
    n_pairs = dest_ref.shape[0]

    def fill(i, carry):
        slot_ref[i] = 0
        return carry

    lax.fori_loop(0, n_slots, fill, 0, unroll=16)

    def scatter(a, carry):
        slot_ref[dest_ref[a]] = lax.shift_right_logical(a, int(math.log2(TOP_K)))
        return carry

    lax.fori_loop(0, n_pairs, scatter, 0, unroll=16)


def build_slot_table(dest_flat, n_slots):
    return pl.pallas_call(
        _slot_table_kernel,
        in_specs=[pl.BlockSpec(memory_space=pltpu.SMEM)],
        out_specs=pl.BlockSpec(memory_space=pltpu.SMEM),
        out_shape=jax.ShapeDtypeStruct((n_slots,), jnp.int32),
        name="build_slot_table",
    )(dest_flat)


GATHER_UNROLL = 8


def _row_gather_kernel(slot_ref, rows_ref, used_ref, x_hbm, o_ref, buf_ref, sems):
    t = pl.program_id(0)
    tr = buf_ref.shape[1]
    used = used_ref[0]

    def n_copies(tile):
        return ((rows_ref[tile] + GATHER_UNROLL - 1) // GATHER_UNROLL) * GATHER_UNROLL

    def issue(tile):
        base = tile * tr
        nc = n_copies(tile)
        for b in range(2):
            @pl.when(tile % 2 == b)
            def _(b=b):
                for g in range(tr // GATHER_UNROLL):
                    @pl.when(g * GATHER_UNROLL < nc)
                    def _(g=g):
                        for u in range(GATHER_UNROLL):
                            r = g * GATHER_UNROLL + u
                            pltpu.make_async_copy(x_hbm.at[pl.ds(slot_ref[base + r], 1)],
                                                  buf_ref.at[b, pl.ds(r, 1)], sems.at[b]).start()

    @pl.when(t == 0)
    def _():
        buf_ref[...] = jnp.zeros(buf_ref.shape, buf_ref.dtype)

        def body(r, carry):
            pltpu.make_async_copy(x_hbm.at[pl.ds(slot_ref[r], 1)], buf_ref.at[0, pl.ds(r, 1)],
                                  sems.at[0]).start()
            return carry

        lax.fori_loop(0, n_copies(0), body, 0)

    @pl.when(t + 1 < used)
    def _():
        issue(t + 1)

    @pl.when(t < used)
    def _():
        b = t % 2
        nc = n_copies(t)

        @pl.when(nc > 0)
        def _():
            pltpu.make_async_copy(x_hbm.at[pl.ds(0, nc)], buf_ref.at[b, pl.ds(0, nc)], sems.at[b]).wait()

        row = lax.broadcasted_iota(jnp.int32, o_ref.shape, 0)
        o_ref[...] = jnp.where(row < rows_ref[t], buf_ref[b], 0.0).astype(o_ref.dtype)


def gather_rows(slot_tok, tile_rows, used_tiles, xn, n_slots, tr):
    n, d = xn.shape
    grid_spec = pltpu.PrefetchScalarGridSpec(
        num_scalar_prefetch=3,
        grid=(n_slots // tr,),
        in_specs=[pl.BlockSpec(memory_space=pl.ANY)],
        out_specs=pl.BlockSpec((tr, d), lambda t, slot, rows, used: (jnp.minimum(t, used[0] - 1), 0)),
        scratch_shapes=[pltpu.VMEM((2, tr, d), F32), pltpu.SemaphoreType.DMA((2,))],
    )
    return pl.pallas_call(
        _row_gather_kernel,
        grid_spec=grid_spec,
        out_shape=jax.ShapeDtypeStruct((n_slots, d), BF16),
        compiler_params=pltpu.CompilerParams(dimension_semantics=("arbitrary",), vmem_limit_bytes=VMEM_LIMIT,
                                             disable_bounds_checks=True),
        name="moe_gather_rows",
    )(slot_tok, tile_rows, used_tiles, xn)


def _expert_kernel(te_ref, tv_ref, tb_ref, x_ref, wg_ref, wl_ref, bg_ref, bl_ref, w2_ref, b2_ref, o_ref,
                   act_s, *, nf, blocks):
    j = pl.program_id(0)
    s = pl.program_id(1)
    valid = tv_ref[j]
    tf = wg_ref.shape[1]

    @pl.when(jnp.logical_and(valid > 0, s < nf))
    def _():
        for r0, rn in blocks:
            @pl.when(r0 < valid)
            def _(r0=r0, rn=rn):
                x = x_ref[r0:r0 + rn, :]
                glu = jnp.dot(x, wg_ref[...].astype(BF16), preferred_element_type=F32) + bg_ref[...]
                lin = jnp.dot(x, wl_ref[...].astype(BF16), preferred_element_type=F32) + bl_ref[...]
                glu = jnp.minimum(glu, SWIGLU_LIMIT)
                lin = jnp.clip(lin, -SWIGLU_LIMIT, SWIGLU_LIMIT)
                act = glu * jax.nn.sigmoid(SWIGLU_ALPHA * glu) * (lin + 1.0)
                act_s[s, r0:r0 + rn, :] = act.astype(BF16)

    @pl.when(jnp.logical_and(valid > 0, s >= nf))
    def _():
        for r0, rn in blocks:
            @pl.when(r0 < valid)
            def _(r0=r0, rn=rn):
                y = b2_ref[...] + jnp.dot(act_s[0, r0:r0 + rn, :], w2_ref[0:tf, :].astype(BF16),
                                          preferred_element_type=F32)
                for f in range(1, nf):
                    y = y + jnp.dot(act_s[f, r0:r0 + rn, :], w2_ref[f * tf:(f + 1) * tf, :].astype(BF16),
                                    preferred_element_type=F32)
                o_ref[r0:r0 + rn, :] = y

            @pl.when(r0 >= valid)
            def _(r0=r0, rn=rn):
                o_ref[r0:r0 + rn, :] = jnp.zeros((rn, o_ref.shape[1]), o_ref.dtype)


def expert_ffn(tile_expert, tile_valid, tile_block, xs, w1, b1, w2, b2, tm, blocks, tf=512):
    n_slots, d = xs.shape
    nf = D_FF // tf
    nn = d // tf
    assert nn == nf
    b1r = b1.reshape(N_EXPERTS, 1, 2 * D_FF)
    b2r = b2.reshape(N_EXPERTS, 1, d)

    def w1_blk(j, s, tv):
        return jnp.where(tv[j] > 0, jnp.minimum(s, nf - 1), nf - 1)

    def w2_blk(j, s, tv):
        return jnp.where(tv[j] > 0, jnp.maximum(s - nf, 0), nn - 1)

    grid_spec = pltpu.PrefetchScalarGridSpec(
        num_scalar_prefetch=3,
        grid=(n_slots // tm, nf + nn),
        in_specs=[
            pl.BlockSpec((tm, d), lambda j, s, te, tv, tb: (tb[j], 0)),
            pl.BlockSpec((None, d, tf), lambda j, s, te, tv, tb: (te[j], 0, w1_blk(j, s, tv))),
            pl.BlockSpec((None, d, tf), lambda j, s, te, tv, tb: (te[j], 0, nf + w1_blk(j, s, tv))),
            pl.BlockSpec((None, 1, tf), lambda j, s, te, tv, tb: (te[j], 0, w1_blk(j, s, tv))),
            pl.BlockSpec((None, 1, tf), lambda j, s, te, tv, tb: (te[j], 0, nf + w1_blk(j, s, tv))),
            pl.BlockSpec((None, D_FF, tf), lambda j, s, te, tv, tb: (te[j], 0, w2_blk(j, s, tv))),
            pl.BlockSpec((None, 1, tf), lambda j, s, te, tv, tb: (te[j], 0, w2_blk(j, s, tv))),
        ],
        out_specs=pl.BlockSpec((tm, tf), lambda j, s, te, tv, tb: (tb[j], w2_blk(j, s, tv))),
        scratch_shapes=[pltpu.VMEM((nf, tm, tf), BF16)],
    )
    return pl.pallas_call(
        functools.partial(_expert_kernel, nf=nf, blocks=blocks),
        grid_spec=grid_spec,
        out_shape=jax.ShapeDtypeStruct((n_slots, d), F32),
        compiler_params=_cparams(("arbitrary", "arbitrary")),
        name="moe_expert_ffn",
    )(tile_expert, tile_valid, tile_block, xs, w1, w1, b1r, b1r, w2, b2r)


def _combine_kernel(dest_ref, ys_hbm, x2_ref, meta_ref, gain_ref, o_ref, buf_ref, sems):
    t = pl.program_id(0)
    nt = pl.num_programs(0)
    tt = x2_ref.shape[0]

    def issue_rolled(tile, b):
        base = tile * (tt * TOP_K)

        def body(r, carry):
            for k in range(TOP_K):
                pltpu.make_async_copy(ys_hbm.at[pl.ds(dest_ref[base + r * TOP_K + k], 1)],
                                      buf_ref.at[b, k, pl.ds(r, 1)], sems.at[b]).start()
            return carry

        lax.fori_loop(0, tt, body, 0, unroll=4)

    def issue(tile):
        base = tile * (tt * TOP_K)
        for b in range(2):
            @pl.when(tile % 2 == b)
            def _(b=b):
                for r in range(tt):
                    for k in range(TOP_K):
                        pltpu.make_async_copy(ys_hbm.at[pl.ds(dest_ref[base + r * TOP_K + k], 1)],
                                              buf_ref.at[b, k, pl.ds(r, 1)], sems.at[b]).start()

    @pl.when(t == 0)
    def _():
        issue_rolled(t, 0)

    @pl.when(t + 1 < nt)
    def _():
        issue(t + 1)

    b = t % 2
    for k in range(TOP_K):
        pltpu.make_async_copy(ys_hbm.at[pl.ds(0, tt)], buf_ref.at[b, k], sems.at[b]).wait()
    meta = meta_ref[...]
    y = x2_ref[...]
    for k in range(TOP_K):
        y = y + meta[:, 2 * TOP_K + k:2 * TOP_K + k + 1] * buf_ref[b, k]
    o_ref[...] = y * lax.rsqrt(jnp.mean(y * y, axis=-1, keepdims=True) + NORM_EPS) * gain_ref[...]


def combine_final(dest_flat, ys, x2, meta, final_gain, tt=128):
    n, d = x2.shape
    grid_spec = pltpu.PrefetchScalarGridSpec(
        num_scalar_prefetch=1,
        grid=(n // tt,),
        in_specs=[pl.BlockSpec(memory_space=pl.ANY),
                  pl.BlockSpec((tt, d), lambda t, dest: (t, 0)),
                  pl.BlockSpec((tt, LANES), lambda t, dest: (t, 0)),
                  pl.BlockSpec((1, d), lambda t, dest: (0, 0))],
        out_specs=pl.BlockSpec((tt, d), lambda t, dest: (t, 0)),
        scratch_shapes=[pltpu.VMEM((2, TOP_K, tt, d), F32), pltpu.SemaphoreType.DMA((2,))],
    )
    return pl.pallas_call(
        _combine_kernel,
        grid_spec=grid_spec,
        out_shape=jax.ShapeDtypeStruct((n, d), F32),
        compiler_params=pltpu.CompilerParams(dimension_semantics=("arbitrary",), vmem_limit_bytes=VMEM_LIMIT,
                                             disable_bounds_checks=True),
        name="moe_combine_final",
    )(dest_flat, ys, x2, meta, final_gain.reshape(1, d))


MOE_TILE = 1280
MOE_ROW_BLOCKS = ((0, 512), (512, 512), (1024, 128), (1152, 128))
MOE_GATHER_TILE = 256


def moe_ffn_final(x2, xn, meta, counts, w1, b1, w2, b2, final_gain, tm=MOE_TILE):
    n, d = x2.shape
    a = n * TOP_K
    n_tiles = -(-a // tm) + N_EXPERTS
    n_slots = n_tiles * tm
    idx = meta[:, 0:TOP_K].astype(jnp.int32)
    rank = meta[:, TOP_K:2 * TOP_K].astype(jnp.int32)
    cnt = counts[0, :N_EXPERTS].astype(jnp.int32)
    tiles_e = (cnt + tm - 1) // tm
    tile_end = jnp.cumsum(tiles_e)
    tile_start = tile_end - tiles_e
    used = tile_end[-1]
    dest = (tile_start * tm)[idx] + rank
    dest_flat = dest.reshape(a)
    tj = jnp.arange(n_tiles, dtype=jnp.int32)
    te = jnp.minimum(jnp.sum(tj[:, None] >= tile_end[None, :], axis=1), N_EXPERTS - 1).astype(jnp.int32)
    last_e = te[jnp.maximum(used - 1, 0)]
    tile_expert = jnp.where(tj < used, te, last_e).astype(jnp.int32)
    tile_valid = jnp.where(tj < used, jnp.clip(cnt[te] - (tj - tile_start[te]) * tm, 0, tm), 0).astype(jnp.int32)
    tile_block = jnp.maximum(jnp.minimum(tj, used - 1), 0).astype(jnp.int32)

    slot_tok = build_slot_table(dest_flat, n_slots)
    tr = MOE_GATHER_TILE
    per = tm // tr
    gj = jnp.arange(n_tiles * per, dtype=jnp.int32)
    gather_rows_valid = jnp.clip(tile_valid[gj // per] - (gj % per) * tr, 0, tr).astype(jnp.int32)
    used_gather = jnp.maximum(used * per, 1).astype(jnp.int32).reshape(1)
    xs = gather_rows(slot_tok, gather_rows_valid, used_gather, xn, n_slots, tr)
    ys = expert_ffn(tile_expert, tile_valid, tile_block, xs, w1, b1, w2, b2, tm, MOE_ROW_BLOCKS)
    return combine_final(dest_flat, ys, x2, meta, final_gain)


def kernel(x, norm_mix_gain, w_in, hgrn_lb_logits, hgrn_norm_gain, hy_conv_w, hy_conv_b, hy_filt_w1,
           hy_filt_b1, hy_filt_w2, hy_filt_b2, hy_filt_w3, hy_filt_b3, hy_filt_w4, hy_filt_freq, hy_bias,
           hy_norm_gain, w_out, norm_ffn_gain, w_router, b_router, moe_w1, moe_b1, moe_w2, moe_b2,
           final_norm_gain):
    b, l, d = x.shape
    n = b * l
    x2d = x.reshape(n, d)
    proj = in_projection(x2d, norm_mix_gain[0], w_in[0].astype(BF16))
    proj3 = proj.reshape(b, l, D_IN)
    hg = hgrn2_mixer(proj3, hgrn_lb_logits, hgrn_norm_gain[0])

    sym, flt = (jnp.asarray(tab).astype(BF16) for tab in _dft_tables(l))
    tap_sum, tap_diff = hyena_filter_taps(l, hy_filt_w1[0], hy_filt_b1[0], hy_filt_w2[0], hy_filt_b2[0],
                                          hy_filt_w3[0], hy_filt_b3[0], hy_filt_w4[0], hy_filt_freq[0])
    spectra = hyena_filter_spectra(flt, jnp.stack([tap_sum, tap_diff]))
    hy = hyena_mixer(proj3, hy_conv_w[0], hy_conv_b[0], sym, spectra, hy_bias[0])

    x2, xn, meta, counts = outproj_router(hg.reshape(n, HG_WIDTH), hy.reshape(n, HY_WIDTH), x2d,
                                          w_out[0].astype(BF16), hy_norm_gain[0], norm_ffn_gain[0],
                                          w_router[0], b_router[0])
    out = moe_ffn_final(x2, xn, meta, counts, moe_w1[0], moe_b1[0], moe_w2[0], moe_b2[0], final_norm_gain)
    return out.reshape(b, l, d)
```

```python
import functools
import math

import numpy as np
import jax
import jax.numpy as jnp
from jax import lax
from jax.experimental import pallas as pl
from jax.experimental.pallas import tpu as pltpu

F32 = jnp.float32
BF16 = jnp.bfloat16

D_MODEL = 2048
HG_WIDTH = 1024
HG_HEAD_DIM = 128
HG_HEADS = HG_WIDTH // HG_HEAD_DIM
HY_WIDTH = 1024
HY_ORDER = 2
HY_EMB = 33
HY_BANDS = (HY_EMB - 1) // 2
HY_FILTER_HIDDEN = 64
HY_FAST_DECAY_PCT = 0.3
HY_SLOW_DECAY_PCT = 1.5
HY_DECAY_TARGET = 1e-2
N_EXPERTS = 32
TOP_K = 4
D_FF = D_MODEL
SWIGLU_ALPHA = 1.702
SWIGLU_LIMIT = 7.0
NORM_EPS = 1e-5
D_IN = 5 * HG_WIDTH + (HY_ORDER + 1) * HY_WIDTH

LANES = 128
GLA_CHUNK = 128
GLA_LEVELS = (64, 32, 16, 8, 4, 2, 1)
VMEM_LIMIT = 56 * 1024 * 1024


def _cparams(sem, vmem=VMEM_LIMIT):
    return pltpu.CompilerParams(dimension_semantics=sem, vmem_limit_bytes=vmem)


def _inproj_kernel(x_ref, g_ref, w_ref, o_ref, h_s):
    @pl.when(pl.program_id(1) == 0)
    def _():
        x = x_ref[...]
        ms = jnp.mean(x * x, axis=-1, keepdims=True)
        h_s[...] = (x * lax.rsqrt(ms + NORM_EPS) * g_ref[...]).astype(BF16)

    o_ref[...] = jnp.dot(h_s[...], w_ref[...], preferred_element_type=F32)


def in_projection(x2d, gain, w_bf16, tm=1024, tn=1024):
    n, d = x2d.shape
    dout = w_bf16.shape[1]
    return pl.pallas_call(
        _inproj_kernel,
        grid=(n // tm, dout // tn),
        in_specs=[
            pl.BlockSpec((tm, d), lambda i, j: (i, 0)),
            pl.BlockSpec((1, d), lambda i, j: (0, 0)),
            pl.BlockSpec((d, tn), lambda i, j: (0, j)),
        ],
        out_specs=pl.BlockSpec((tm, tn), lambda i, j: (i, j)),
        out_shape=jax.ShapeDtypeStruct((n, dout), F32),
        scratch_shapes=[pltpu.VMEM((tm, d), BF16)],
        compiler_params=_cparams(("arbitrary", "arbitrary")),
        name="in_projection",
    )(x2d, gain.reshape(1, d), w_bf16)


def _gla_constants():
    c = GLA_CHUNK
    t = np.arange(c)[:, None]
    r = np.arange(c)[None, :]
    fwd = [r <= t, r > t]
    bwd = [r >= t, r < t]
    for m in GLA_LEVELS:
        pos = t % (2 * m)
        mid = t - pos + m
        second = pos >= m
        fwd.append(np.where(second, (r >= mid) & (r <= t), (r > t) & (r < mid)))
        bwd.append(np.where(second, (r >= mid) & (r < t), (r >= t) & (r < mid)))
    x = t ^ r
    lv = np.full((c, c), -1, np.int32)
    for j in range(int(math.log2(c))):
        lv = np.where((x >> j) == 1, j, lv)
    mf = jnp.asarray(np.concatenate(fwd, 0).astype(np.float32), BF16)
    mb = jnp.asarray(np.concatenate(bwd, 0).astype(np.float32), BF16)
    return mf, mb, jnp.asarray(lv, jnp.int32)


def _hgrn_kernel(q_ref, ff_ref, fb_ref, i_ref, g_ref, lb_ref, gain_ref, mf_ref, mb_ref, lv_ref,
                 o_ref, acc_ref, qb_ref, kb_ref, db_ref):
    c = GLA_CHUNK
    n_chunks = q_ref.shape[0] // c
    lb = lb_ref[...]
    l0, l1 = lb[0], lb[1]
    mx = jnp.maximum(l0, l1)
    e0 = jnp.exp(l0 - mx)
    e1 = jnp.exp(l1 - mx)
    p0 = e0 / (e0 + e1)
    lb_f = p0[0:1, :]
    lb_b = p0[1:2, :]
    lv = lv_ref[...]
    row = lax.broadcasted_iota(jnp.int32, (c, LANES), 0)
    nt = (((1,), (1,)), ((), ()))

    def gates(z, lower):
        f = lower + (1.0 - lower) * jax.nn.sigmoid(z)
        return 1.0 - f, jnp.log(f)

    def exponents(m_ref, g):
        g_hi = g.astype(BF16)
        g_lo = (g - g_hi.astype(F32)).astype(BF16)
        m = m_ref[...]
        return (jnp.dot(m, g_hi, preferred_element_type=F32)
                + jnp.dot(m, g_lo, preferred_element_type=F32))

    def fwd_body(ci, st):
        rows = pl.ds(pl.multiple_of(ci * c, c), c)
        qr = q_ref[rows, :]
        q = qr * jax.nn.sigmoid(qr)
        v = i_ref[rows, :]
        kf, gf = gates(ff_ref[rows, :], lb_f)
        kb, gb = gates(fb_ref[rows, :], lb_b)
        ef_all = exponents(mf_ref, gf)
        eb_all = exponents(mb_ref, gb)
        scores = jnp.zeros((c, c), F32)
        for li, m in enumerate(GLA_LEVELS):
            ef = ef_all[(2 + li) * c:(3 + li) * c]
            eb = eb_all[(2 + li) * c:(3 + li) * c]
            second = (row & m) != 0
            a = q * jnp.exp(jnp.where(second, ef, eb))
            b = jnp.where(second, kb, kf) * jnp.exp(jnp.where(second, eb, ef))
            s = lax.dot_general(a.astype(BF16), b.astype(BF16), nt, preferred_element_type=F32)
            scores = jnp.where(lv == int(math.log2(m)), s, scores)
        v_bf = v.astype(BF16)
        o = jnp.dot(scores.astype(BF16), v_bf, preferred_element_type=F32)
        o = o + jnp.sum(q * (kf + kb), axis=-1, keepdims=True) * v
        b_inc = ef_all[0:c]
        q_dec = (q * jnp.exp(b_inc)).astype(BF16)
        o = o + lax.dot_general(q_dec, st.astype(BF16), nt, preferred_element_type=F32)
        k_dec = (kf * jnp.exp(ef_all[c:2 * c])).astype(BF16)
        vt_bf = v.T.astype(BF16)
        st = st * jnp.exp(b_inc[c - 1:c, :]) + jnp.dot(vt_bf, k_dec, preferred_element_type=F32)
        acc_ref[rows, :] = o
        bb = eb_all[0:c]
        qb_ref[rows, :] = (q * jnp.exp(bb)).astype(BF16)
        kb_ref[rows, :] = (kb * jnp.exp(eb_all[c:2 * c])).astype(BF16)
        db_ref[ci] = jnp.broadcast_to(jnp.exp(bb[0:1, :]), (8, LANES))
        return st

    lax.fori_loop(0, n_chunks, fwd_body, jnp.zeros((c, c), F32), unroll=4)

    gain = gain_ref[...]

    def bwd_body(i, st):
        ci = n_chunks - 1 - i
        rows = pl.ds(pl.multiple_of(ci * c, c), c)
        o = acc_ref[rows, :] + lax.dot_general(qb_ref[rows, :], st.astype(BF16), nt,
                                               preferred_element_type=F32)
        vt_bf = i_ref[rows, :].T.astype(BF16)
        st = st * db_ref[ci][0:1, :] + jnp.dot(vt_bf, kb_ref[rows, :], preferred_element_type=F32)
        o = o * lax.rsqrt(jnp.mean(o * o, axis=-1, keepdims=True) + NORM_EPS) * gain
        gr = g_ref[rows, :]
        o_ref[rows, :] = (o * (gr * jax.nn.sigmoid(gr))).astype(o_ref.dtype)
        return st

    lax.fori_loop(0, n_chunks, bwd_body, jnp.zeros((c, c), F32), unroll=4)


def hgrn2_mixer(proj3, lb_logits, norm_gain):
    b, l, _ = proj3.shape
    h, dh = HG_HEADS, HG_HEAD_DIM
    mf, mb, lv = _gla_constants()
    nblk = mf.shape[0]

    def col(off):
        return pl.BlockSpec((None, l, dh), lambda bi, hi, off=off: (bi, 0, off + hi))

    const2 = lambda bi, hi: (0, 0)
    return pl.pallas_call(
        _hgrn_kernel,
        grid=(b, h),
        in_specs=[col(0), col(h), col(2 * h), col(3 * h), col(4 * h),
                  pl.BlockSpec((2, 2, dh), lambda bi, hi: (0, 0, hi)),
                  pl.BlockSpec((1, dh), lambda bi, hi: (0, hi)),
                  pl.BlockSpec((nblk, GLA_CHUNK), const2),
                  pl.BlockSpec((nblk, GLA_CHUNK), const2),
                  pl.BlockSpec((GLA_CHUNK, GLA_CHUNK), const2)],
        out_specs=pl.BlockSpec((None, l, dh), lambda bi, hi: (bi, 0, hi)),
        out_shape=jax.ShapeDtypeStruct((b, l, HG_WIDTH), BF16),
        scratch_shapes=[pltpu.VMEM((l, dh), F32), pltpu.VMEM((l, dh), BF16), pltpu.VMEM((l, dh), BF16),
                        pltpu.VMEM((l // GLA_CHUNK, 8, LANES), F32)],
        compiler_params=_cparams(("arbitrary", "arbitrary")),
        name="hgrn2_mixer",
    )(proj3, proj3, proj3, proj3, proj3, lb_logits, norm_gain.reshape(1, HG_WIDTH), mf, mb, lv)


@functools.lru_cache(maxsize=None)
def _dft_tables(l):
    n = 2 * l
    k2 = 2 * np.arange(l, dtype=np.int64)[:, None] + 1
    m1 = np.arange(l, dtype=np.int64)[None, :]
    ang_s = ((k2 * (2 * m1 + 1)) % (4 * n)).astype(np.float64) * (2.0 * math.pi / (4 * n))
    ang_f = ((k2 * m1) % (2 * n)).astype(np.float64) * (2.0 * math.pi / (2 * n))
    sym = np.stack([np.cos(ang_s), np.sin(ang_s)]).astype(np.float32)
    flt = np.stack([np.cos(ang_f), np.sin(ang_f)]).astype(np.float32)
    return sym, flt


def _filter_features(l):
    pos = jnp.arange(l, dtype=F32)
    t = pos / max(l - 1, 1)
    bands = jnp.linspace(1e-4, HY_BANDS - 1, HY_BANDS, dtype=F32)
    ang = (2.0 * math.pi / l) * pos[:, None] * bands[None, :]
    z = jnp.concatenate([t[:, None], jnp.cos(ang), -jnp.sin(ang)], axis=-1)
    z = jnp.pad(z, ((0, 0), (0, LANES - HY_EMB)))
    min_decay = math.log(HY_DECAY_TARGET) / HY_FAST_DECAY_PCT
    max_decay = math.log(HY_DECAY_TARGET) / HY_SLOW_DECAY_PCT
    deltas = jnp.abs(jnp.linspace(min_decay, max_decay, HY_WIDTH, dtype=F32))
    window = jnp.exp(-t[:, None] * deltas[None, :])
    return z, window


def _filter_kernel(z_ref, win_ref, w1_ref, b1_ref, w2_ref, b2_ref, w3_ref, b3_ref, w4_ref, fr_ref,
                   sum_ref, diff_ref):
    hp = lax.Precision.HIGHEST
    fr = fr_ref[...]
    h = jnp.sin(fr * (jnp.dot(z_ref[...], w1_ref[...], precision=hp, preferred_element_type=F32) + b1_ref[...]))
    h = jnp.sin(fr * (jnp.dot(h, w2_ref[...], precision=hp, preferred_element_type=F32) + b2_ref[...]))
    h = jnp.sin(fr * (jnp.dot(h, w3_ref[...], precision=hp, preferred_element_type=F32) + b3_ref[...]))
    h = jnp.dot(h, w4_ref[...], precision=hp, preferred_element_type=F32)
    win = win_ref[...]
    tl = h.shape[0]
    lag = pl.program_id(0) * tl + lax.broadcasted_iota(jnp.int32, (tl, HY_WIDTH), 0)
    for o in range(HY_ORDER):
        hf = h[:, (2 * o) * HY_WIDTH:(2 * o + 1) * HY_WIDTH] * win
        hb = h[:, (2 * o + 1) * HY_WIDTH:(2 * o + 2) * HY_WIDTH] * win
        hb = jnp.where(lag == 0, 0.0, hb)
        sum_ref[:, o * HY_WIDTH:(o + 1) * HY_WIDTH] = (hf + hb).astype(sum_ref.dtype)
        diff_ref[:, o * HY_WIDTH:(o + 1) * HY_WIDTH] = (hf - hb).astype(diff_ref.dtype)


def hyena_filter_taps(l, w1, b1, w2, b2, w3, b3, w4, freq, tl=256):
    z, window = _filter_features(l)
    w1p = jnp.pad(w1, ((0, LANES - HY_EMB), (0, 0)))
    hid = HY_FILTER_HIDDEN
    full = lambda shape: pl.BlockSpec(shape, lambda i: (0,) * len(shape))
    nout = HY_ORDER * HY_WIDTH
    return pl.pallas_call(
        _filter_kernel,
        grid=(l // tl,),
        in_specs=[pl.BlockSpec((tl, LANES), lambda i: (i, 0)),
                  pl.BlockSpec((tl, HY_WIDTH), lambda i: (i, 0)),
                  full((LANES, hid)), full((1, hid)), full((hid, hid)), full((1, hid)),
                  full((hid, hid)), full((1, hid)), full((hid, 2 * nout)), full((1, hid))],
        out_specs=[pl.BlockSpec((tl, nout), lambda i: (i, 0)), pl.BlockSpec((tl, nout), lambda i: (i, 0))],
        out_shape=[jax.ShapeDtypeStruct((l, nout), BF16), jax.ShapeDtypeStruct((l, nout), BF16)],
        compiler_params=_cparams(("arbitrary",)),
        name="hyena_filter_taps",
    )(z, window, w1p, b1.reshape(1, hid), w2, b2.reshape(1, hid), w3, b3.reshape(1, hid), w4,
      freq.reshape(1, hid))


def _spectrum_kernel(f_ref, h_ref, o_ref):
    o_ref[...] = jnp.dot(f_ref[...], h_ref[...], preferred_element_type=F32).astype(o_ref.dtype)


def hyena_filter_spectra(flt, taps, tn=512):
    _, l, nout = taps.shape
    return pl.pallas_call(
        _spectrum_kernel,
        grid=(2, nout // tn),
        in_specs=[pl.BlockSpec((None, l, l), lambda p, j: (p, 0, 0)),
                  pl.BlockSpec((None, l, tn), lambda p, j: (p, 0, j))],
        out_specs=pl.BlockSpec((None, l, tn), lambda p, j: (p, 0, j)),
        out_shape=jax.ShapeDtypeStruct((2, l, nout), BF16),
        compiler_params=_cparams(("arbitrary", "arbitrary")),
        name="hyena_filter_spectra",
    )(flt, taps)


def _hyena_kernel(xv_ref, x1_ref, x2_ref, cw_ref, cb_ref, sym_ref, sp0_ref, sp1_ref, bias_ref, o_ref,
                  z_s, g_s, zb_s, pa_s, pb_s, *, rb):
    l, tc = xv_ref.shape
    scale = 2.0 / (2 * l)
    first = lax.broadcasted_iota(jnp.int32, (l, tc), 0) == 0
    last = lax.broadcasted_iota(jnp.int32, (l, tc), 0) == l - 1

    def short_conv(x_ref, part):
        x = x_ref[...]
        w = cw_ref[part]
        prev = jnp.where(first, 0.0, pltpu.roll(x, 1, 0))
        nxt = jnp.where(last, 0.0, pltpu.roll(x, l - 1, 0))
        return prev * w[0:1, :] + x * w[1:2, :] + nxt * w[2:3, :] + cb_ref[part]

    z_s[...] = short_conv(xv_ref, 0)
    for o, (gate_ref, part, sp_ref) in enumerate(((x1_ref, 1, sp0_ref), (x2_ref, 2, sp1_ref))):
        zb_s[...] = z_s[...].astype(BF16)
        g_s[...] = short_conv(gate_ref, part)
        for r0 in range(0, l, rb):
            a = jnp.dot(sym_ref[0, r0:r0 + rb, :], zb_s[...], preferred_element_type=F32)
            b = jnp.dot(sym_ref[1, r0:r0 + rb, :], zb_s[...], preferred_element_type=F32)
            hr = sp_ref[0, r0:r0 + rb, :].astype(F32)
            hs = sp_ref[1, r0:r0 + rb, :].astype(F32)
            pa_s[r0:r0 + rb, :] = (a * hr - b * hs).astype(BF16)
            pb_s[r0:r0 + rb, :] = (a * hs + b * hr).astype(BF16)
        for r0 in range(0, l, rb):
            y = (jnp.dot(sym_ref[0, r0:r0 + rb, :], pa_s[...], preferred_element_type=F32)
                 + jnp.dot(sym_ref[1, r0:r0 + rb, :], pb_s[...], preferred_element_type=F32)) * scale
            y = y + z_s[r0:r0 + rb, :] * bias_ref[o]
            z_s[r0:r0 + rb, :] = g_s[r0:r0 + rb, :] * y
    o_ref[...] = z_s[...].astype(o_ref.dtype)


def hyena_mixer(proj3, conv_w, conv_b, sym, spectra, hy_bias, tc=256, rb=1024):
    b, l, _ = proj3.shape
    nct = HY_WIDTH // tc
    base = (5 * HG_WIDTH) // tc

    def col(part):
        return pl.BlockSpec((None, l, tc), lambda ci, bi, part=part: (bi, 0, base + part * nct + ci))

    def spec_cols(o):
        return pl.BlockSpec((2, l, tc), lambda ci, bi, o=o: (0, 0, o * nct + ci))

    cw = conv_w.reshape(3, HY_ORDER + 1, HY_WIDTH).transpose(1, 0, 2)
    cb = conv_b.reshape(HY_ORDER + 1, 1, HY_WIDTH)
    return pl.pallas_call(
        functools.partial(_hyena_kernel, rb=rb),
        grid=(nct, b),
        in_specs=[col(0), col(1), col(2),
                  pl.BlockSpec((HY_ORDER + 1, 3, tc), lambda ci, bi: (0, 0, ci)),
                  pl.BlockSpec((HY_ORDER + 1, 1, tc), lambda ci, bi: (0, 0, ci)),
                  pl.BlockSpec((2, l, l), lambda ci, bi: (0, 0, 0), pipeline_mode=pl.Buffered(1)),
                  spec_cols(0), spec_cols(1),
                  pl.BlockSpec((HY_ORDER, 1, tc), lambda ci, bi: (0, 0, ci))],
        out_specs=pl.BlockSpec((None, l, tc), lambda ci, bi: (bi, 0, ci)),
        out_shape=jax.ShapeDtypeStruct((b, l, HY_WIDTH), BF16),
        scratch_shapes=[pltpu.VMEM((l, tc), F32), pltpu.VMEM((l, tc), F32), pltpu.VMEM((l, tc), BF16),
                        pltpu.VMEM((l, tc), BF16), pltpu.VMEM((l, tc), BF16)],
        compiler_params=_cparams(("arbitrary", "arbitrary")),
        name="hyena_mixer",
    )(proj3, proj3, proj3, cw, cb, sym, spectra, spectra, hy_bias.reshape(HY_ORDER, 1, HY_WIDTH))


def _outproj_router_kernel(hg_ref, hy_ref, x_ref, w_ref, hyg_ref, fg_ref, wr_ref, br_ref, tri_ref,
                           x2_ref, xn_ref, meta_ref, cnt_ref, carry_ref):
    @pl.when(pl.program_id(0) == 0)
    def _():
        carry_ref[...] = jnp.zeros_like(carry_ref)

    hy = hy_ref[...].astype(F32)
    hy = hy * lax.rsqrt(jnp.mean(hy * hy, axis=-1, keepdims=True) + NORM_EPS) * hyg_ref[...]
    m = (jnp.dot(hg_ref[...], w_ref[0:HG_WIDTH, :], preferred_element_type=F32)
         + jnp.dot(hy.astype(BF16), w_ref[HG_WIDTH:, :], preferred_element_type=F32))
    x2 = x_ref[...] + m
    x2_ref[...] = x2
    xn = x2 * lax.rsqrt(jnp.mean(x2 * x2, axis=-1, keepdims=True) + NORM_EPS) * fg_ref[...]
    xn_ref[...] = xn
    xn_hi = xn.astype(BF16)
    xn_lo = (xn - xn_hi.astype(F32)).astype(BF16)
    wr = wr_ref[...]
    wr_hi = wr.astype(BF16)
    wr_lo = (wr - wr_hi.astype(F32)).astype(BF16)
    logits = (jnp.dot(xn_hi, wr_hi, preferred_element_type=F32)
              + jnp.dot(xn_hi, wr_lo, preferred_element_type=F32)
              + jnp.dot(xn_lo, wr_hi, preferred_element_type=F32)) + br_ref[...]
    tm = logits.shape[0]
    lane = lax.broadcasted_iota(jnp.int32, (tm, LANES), 1).astype(F32)
    neg = jnp.float32(-jnp.inf)
    logits = jnp.where(lane < N_EXPERTS, logits, neg)
    tops, idxs = [], []
    for _ in range(TOP_K):
        mx = jnp.max(logits, axis=-1, keepdims=True)
        idx = jnp.min(jnp.where(logits == mx, lane, float(LANES)), axis=-1, keepdims=True)
        tops.append(mx)
        idxs.append(idx)
        logits = jnp.where(lane == idx, neg, logits)
    exps = [jnp.exp(t - tops[0]) for t in tops]
    denom = exps[0] + exps[1] + exps[2] + exps[3]
    onehot = jnp.zeros((tm, LANES), F32)
    for idx in idxs:
        onehot = onehot + jnp.where(lane == idx, 1.0, 0.0)
    cum = jnp.dot(tri_ref[...], onehot.astype(BF16), preferred_element_type=F32) + carry_ref[0:1, :]
    meta = jnp.zeros((tm, LANES), F32)
    for k in range(TOP_K):
        rank = jnp.sum(jnp.where(lane == idxs[k], cum, 0.0), axis=-1, keepdims=True)
        meta = jnp.where(lane == k, idxs[k], meta)
        meta = jnp.where(lane == TOP_K + k, rank, meta)
        meta = jnp.where(lane == 2 * TOP_K + k, exps[k] / denom, meta)
    meta_ref[...] = meta
    carry = carry_ref[...] + jnp.sum(onehot, axis=0, keepdims=True)
    carry_ref[...] = carry
    cnt_ref[...] = carry


def outproj_router(hg2d, hy2d, x2d, w_out_bf16, hy_gain, ffn_gain, w_router, b_router, tm=512):
    n, d = x2d.shape
    wr = jnp.pad(w_router, ((0, 0), (0, LANES - N_EXPERTS)))
    br = jnp.pad(b_router, (0, LANES - N_EXPERTS)).reshape(1, LANES)
    tri = jnp.asarray(np.tril(np.ones((tm, tm), np.float32), -1), BF16)
    row = lambda w: pl.BlockSpec((tm, w), lambda i: (i, 0))
    full = lambda shape, **kw: pl.BlockSpec(shape, lambda i: (0,) * len(shape), **kw)
    return pl.pallas_call(
        _outproj_router_kernel,
        grid=(n // tm,),
        in_specs=[row(HG_WIDTH), row(HY_WIDTH), row(d),
                  full((d, d), pipeline_mode=pl.Buffered(1)),
                  full((1, HY_WIDTH)), full((1, d)), full((d, LANES)), full((1, LANES)), full((tm, tm))],
        out_specs=[row(d), row(d), row(LANES), full((8, LANES))],
        out_shape=[jax.ShapeDtypeStruct((n, d), F32), jax.ShapeDtypeStruct((n, d), F32),
                   jax.ShapeDtypeStruct((n, LANES), F32), jax.ShapeDtypeStruct((8, LANES), F32)],
        scratch_shapes=[pltpu.VMEM((8, LANES), F32)],
        compiler_params=_cparams(("arbitrary",)),
        name="outproj_router",
    )(hg2d, hy2d, x2d, w_out_bf16, hy_gain.reshape(1, HY_WIDTH), ffn_gain.reshape(1, d), wr, br, tri)


def _slot_table_kernel(dest_ref, slot_ref):
    n_slots = slot_ref.shape[0]
    n_pairs = dest_ref.shape[0]

    def fill(i, carry):
        slot_ref[i] = 0
        return carry

    lax.fori_loop(0, n_slots, fill, 0, unroll=16)

    def scatter(a, carry):
        slot_ref[dest_ref[a]] = lax.shift_right_logical(a, int(math.log2(TOP_K)))
        return carry

    lax.fori_loop(0, n_pairs, scatter, 0, unroll=16)


def build_slot_table(dest_flat, n_slots):
    return pl.pallas_call(
        _slot_table_kernel,
        in_specs=[pl.BlockSpec(memory_space=pltpu.SMEM)],
        out_specs=pl.BlockSpec(memory_space=pltpu.SMEM),
        out_shape=jax.ShapeDtypeStruct((n_slots,), jnp.int32),
        name="build_slot_table",
    )(dest_flat)


GATHER_UNROLL = 8


def _row_gather_kernel(slot_ref, rows_ref, used_ref, x_hbm, o_ref, buf_ref, sems):
    t = pl.program_id(0)
    tr = buf_ref.shape[1]
    used = used_ref[0]

    def n_copies(tile):
        return ((rows_ref[tile] + GATHER_UNROLL - 1) // GATHER_UNROLL) * GATHER_UNROLL

    def issue(tile):
        base = tile * tr
        nc = n_copies(tile)
        for b in range(2):
            @pl.when(tile % 2 == b)
            def _(b=b):
                for g in range(tr // GATHER_UNROLL):
                    @pl.when(g * GATHER_UNROLL < nc)
                    def _(g=g):
                        for u in range(GATHER_UNROLL):
                            r = g * GATHER_UNROLL + u
                            pltpu.make_async_copy(x_hbm.at[pl.ds(slot_ref[base + r], 1)],
                                                  buf_ref.at[b, pl.ds(r, 1)], sems.at[b]).start()

    @pl.when(t == 0)
    def _():
        buf_ref[...] = jnp.zeros(buf_ref.shape, buf_ref.dtype)

        def body(r, carry):
            pltpu.make_async_copy(x_hbm.at[pl.ds(slot_ref[r], 1)], buf_ref.at[0, pl.ds(r, 1)],
                                  sems.at[0]).start()
            return carry

        lax.fori_loop(0, n_copies(0), body, 0)

    @pl.when(t + 1 < used)
    def _():
        issue(t + 1)

    @pl.when(t < used)
    def _():
        b = t % 2
        nc = n_copies(t)

        @pl.when(nc > 0)
        def _():
            pltpu.make_async_copy(x_hbm.at[pl.ds(0, nc)], buf_ref.at[b, pl.ds(0, nc)], sems.at[b]).wait()

        row = lax.broadcasted_iota(jnp.int32, o_ref.shape, 0)
        o_ref[...] = jnp.where(row < rows_ref[t], buf_ref[b], 0.0).astype(o_ref.dtype)


def gather_rows(slot_tok, tile_rows, used_tiles, xn, n_slots, tr):
    n, d = xn.shape
    grid_spec = pltpu.PrefetchScalarGridSpec(
        num_scalar_prefetch=3,
        grid=(n_slots // tr,),
        in_specs=[pl.BlockSpec(memory_space=pl.ANY)],
        out_specs=pl.BlockSpec((tr, d), lambda t, slot, rows, used: (jnp.minimum(t, used[0] - 1), 0)),
        scratch_shapes=[pltpu.VMEM((2, tr, d), F32), pltpu.SemaphoreType.DMA((2,))],
    )
    return pl.pallas_call(
        _row_gather_kernel,
        grid_spec=grid_spec,
        out_shape=jax.ShapeDtypeStruct((n_slots, d), BF16),
        compiler_params=pltpu.CompilerParams(dimension_semantics=("arbitrary",), vmem_limit_bytes=VMEM_LIMIT,
                                             disable_bounds_checks=True),
        name="moe_gather_rows",
    )(slot_tok, tile_rows, used_tiles, xn)


def _expert_kernel(te1_ref, tv1_ref, tb1_ref, te2_ref, tv2_ref, tb2_ref,
                   x_ref, wg_ref, wl_ref, bg_ref, bl_ref, w2_ref, b2_ref, o_ref, act_s, *, nf, blocks):
    jj = pl.program_id(0)
    s = pl.program_id(1)
    tf = wg_ref.shape[1]
    valid_in = tv1_ref[jj]
    valid_out = tv2_ref[jj]
    buf_in = jj % 2
    buf_out = 1 - buf_in

    @pl.when(valid_in > 0)
    def _():
        for r0, rn in blocks:
            @pl.when(r0 < valid_in)
            def _(r0=r0, rn=rn):
                x = x_ref[r0:r0 + rn, :]
                glu = jnp.dot(x, wg_ref[...].astype(BF16), preferred_element_type=F32) + bg_ref[...]
                lin = jnp.dot(x, wl_ref[...].astype(BF16), preferred_element_type=F32) + bl_ref[...]
                glu = jnp.minimum(glu, SWIGLU_LIMIT)
                lin = jnp.clip(lin, -SWIGLU_LIMIT, SWIGLU_LIMIT)
                act = glu * jax.nn.sigmoid(SWIGLU_ALPHA * glu) * (lin + 1.0)
                act_s[buf_in, s, r0:r0 + rn, :] = act.astype(BF16)

    @pl.when(valid_out > 0)
    def _():
        for r0, rn in blocks:
            @pl.when(r0 < valid_out)
            def _(r0=r0, rn=rn):
                y = b2_ref[...] + jnp.dot(act_s[buf_out, 0, r0:r0 + rn, :], w2_ref[0:tf, :].astype(BF16),
                                          preferred_element_type=F32)
                for f in range(1, nf):
                    y = y + jnp.dot(act_s[buf_out, f, r0:r0 + rn, :],
                                    w2_ref[f * tf:(f + 1) * tf, :].astype(BF16),
                                    preferred_element_type=F32)
                o_ref[r0:r0 + rn, :] = y

            @pl.when(r0 >= valid_out)
            def _(r0=r0, rn=rn):
                o_ref[r0:r0 + rn, :] = jnp.zeros((rn, o_ref.shape[1]), o_ref.dtype)


def expert_ffn(tile_expert, tile_valid, tile_block, xs, w1, b1, w2, b2, tm, blocks, tf=512):
    n_slots, d = xs.shape
    n_tiles = n_slots // tm
    nf = D_FF // tf
    assert d // tf == nf
    b1r = b1.reshape(N_EXPERTS, 1, 2 * D_FF)
    b2r = b2.reshape(N_EXPERTS, 1, d)
    zero = jnp.zeros((1,), jnp.int32)
    te1 = jnp.concatenate([tile_expert, tile_expert[-1:]])
    tv1 = jnp.concatenate([tile_valid, zero])
    tb1 = jnp.concatenate([tile_block, tile_block[-1:]])
    te2 = jnp.concatenate([tile_expert[:1], tile_expert])
    tv2 = jnp.concatenate([zero, tile_valid])
    tb2 = jnp.concatenate([tile_block[:1], tile_block])

    def blk1(jj, s, tv1):
        return jnp.where(tv1[jj] > 0, s, nf - 1)

    def blk2(jj, s, tv2):
        return jnp.where(tv2[jj] > 0, s, jnp.where(jj == 0, 0, nf - 1))

    grid_spec = pltpu.PrefetchScalarGridSpec(
        num_scalar_prefetch=6,
        grid=(n_tiles + 1, nf),
        in_specs=[
            pl.BlockSpec((tm, d), lambda jj, s, te1, tv1, tb1, te2, tv2, tb2: (tb1[jj], 0)),
            pl.BlockSpec((None, d, tf),
                         lambda jj, s, te1, tv1, tb1, te2, tv2, tb2: (te1[jj], 0, blk1(jj, s, tv1))),
            pl.BlockSpec((None, d, tf),
                         lambda jj, s, te1, tv1, tb1, te2, tv2, tb2: (te1[jj], 0, nf + blk1(jj, s, tv1))),
            pl.BlockSpec((None, 1, tf),
                         lambda jj, s, te1, tv1, tb1, te2, tv2, tb2: (te1[jj], 0, blk1(jj, s, tv1))),
            pl.BlockSpec((None, 1, tf),
                         lambda jj, s, te1, tv1, tb1, te2, tv2, tb2: (te1[jj], 0, nf + blk1(jj, s, tv1))),
            pl.BlockSpec((None, D_FF, tf),
                         lambda jj, s, te1, tv1, tb1, te2, tv2, tb2: (te2[jj], 0, blk2(jj, s, tv2))),
            pl.BlockSpec((None, 1, tf),
                         lambda jj, s, te1, tv1, tb1, te2, tv2, tb2: (te2[jj], 0, blk2(jj, s, tv2))),
        ],
        out_specs=pl.BlockSpec((tm, tf),
                               lambda jj, s, te1, tv1, tb1, te2, tv2, tb2: (tb2[jj], blk2(jj, s, tv2))),
        scratch_shapes=[pltpu.VMEM((2, nf, tm, tf), BF16)],
    )
    return pl.pallas_call(
        functools.partial(_expert_kernel, nf=nf, blocks=blocks),
        grid_spec=grid_spec,
        out_shape=jax.ShapeDtypeStruct((n_slots, d), F32),
        compiler_params=_cparams(("arbitrary", "arbitrary")),
        name="moe_expert_ffn",
    )(te1, tv1, tb1, te2, tv2, tb2, xs, w1, w1, b1r, b1r, w2, b2r)


def _combine_kernel(dest_ref, ys_hbm, x2_ref, meta_ref, gain_ref, o_ref, buf_ref, sems):
    t = pl.program_id(0)
    nt = pl.num_programs(0)
    tt = x2_ref.shape[0]

    def issue_rolled(tile, b):
        base = tile * (tt * TOP_K)

        def body(r, carry):
            for k in range(TOP_K):
                pltpu.make_async_copy(ys_hbm.at[pl.ds(dest_ref[base + r * TOP_K + k], 1)],
                                      buf_ref.at[b, k, pl.ds(r, 1)], sems.at[b]).start()
            return carry

        lax.fori_loop(0, tt, body, 0, unroll=4)

    def issue(tile):
        base = tile * (tt * TOP_K)
        for b in range(2):
            @pl.when(tile % 2 == b)
            def _(b=b):
                for r in range(tt):
                    for k in range(TOP_K):
                        pltpu.make_async_copy(ys_hbm.at[pl.ds(dest_ref[base + r * TOP_K + k], 1)],
                                              buf_ref.at[b, k, pl.ds(r, 1)], sems.at[b]).start()

    @pl.when(t == 0)
    def _():
        issue_rolled(t, 0)

    @pl.when(t + 1 < nt)
    def _():
        issue(t + 1)

    b = t % 2
    for k in range(TOP_K):
        pltpu.make_async_copy(ys_hbm.at[pl.ds(0, tt)], buf_ref.at[b, k], sems.at[b]).wait()
    meta = meta_ref[...]
    y = x2_ref[...]
    for k in range(TOP_K):
        y = y + meta[:, 2 * TOP_K + k:2 * TOP_K + k + 1] * buf_ref[b, k]
    o_ref[...] = y * lax.rsqrt(jnp.mean(y * y, axis=-1, keepdims=True) + NORM_EPS) * gain_ref[...]


def combine_final(dest_flat, ys, x2, meta, final_gain, tt=128):
    n, d = x2.shape
    grid_spec = pltpu.PrefetchScalarGridSpec(
        num_scalar_prefetch=1,
        grid=(n // tt,),
        in_specs=[pl.BlockSpec(memory_space=pl.ANY),
                  pl.BlockSpec((tt, d), lambda t, dest: (t, 0)),
                  pl.BlockSpec((tt, LANES), lambda t, dest: (t, 0)),
                  pl.BlockSpec((1, d), lambda t, dest: (0, 0))],
        out_specs=pl.BlockSpec((tt, d), lambda t, dest: (t, 0)),
        scratch_shapes=[pltpu.VMEM((2, TOP_K, tt, d), F32), pltpu.SemaphoreType.DMA((2,))],
    )
    return pl.pallas_call(
        _combine_kernel,
        grid_spec=grid_spec,
        out_shape=jax.ShapeDtypeStruct((n, d), F32),
        compiler_params=pltpu.CompilerParams(dimension_semantics=("arbitrary",), vmem_limit_bytes=VMEM_LIMIT,
                                             disable_bounds_checks=True),
        name="moe_combine_final",
    )(dest_flat, ys, x2, meta, final_gain.reshape(1, d))


MOE_TILE = 1152
MOE_ROW_BLOCKS = ((0, 512), (512, 512), (1024, 128))
MOE_GATHER_TILE = 384


def moe_ffn_final(x2, xn, meta, counts, w1, b1, w2, b2, final_gain, tm=MOE_TILE):
    n, d = x2.shape
    a = n * TOP_K
    n_tiles = -(-a // tm) + N_EXPERTS
    n_slots = n_tiles * tm
    idx = meta[:, 0:TOP_K].astype(jnp.int32)
    rank = meta[:, TOP_K:2 * TOP_K].astype(jnp.int32)
    cnt = counts[0, :N_EXPERTS].astype(jnp.int32)
    tiles_e = (cnt + tm - 1) // tm
    tile_end = jnp.cumsum(tiles_e)
    tile_start = tile_end - tiles_e
    used = tile_end[-1]
    dest = (tile_start * tm)[idx] + rank
    dest_flat = dest.reshape(a)
    tj = jnp.arange(n_tiles, dtype=jnp.int32)
    te = jnp.minimum(jnp.sum(tj[:, None] >= tile_end[None, :], axis=1), N_EXPERTS - 1).astype(jnp.int32)
    last_e = te[jnp.maximum(used - 1, 0)]
    tile_expert = jnp.where(tj < used, te, last_e).astype(jnp.int32)
    tile_valid = jnp.where(tj < used, jnp.clip(cnt[te] - (tj - tile_start[te]) * tm, 0, tm), 0).astype(jnp.int32)
    tile_block = jnp.maximum(jnp.minimum(tj, used - 1), 0).astype(jnp.int32)

    slot_tok = build_slot_table(dest_flat, n_slots)
    tr = MOE_GATHER_TILE
    per = tm // tr
    gj = jnp.arange(n_tiles * per, dtype=jnp.int32)
    gather_rows_valid = jnp.clip(tile_valid[gj // per] - (gj % per) * tr, 0, tr).astype(jnp.int32)
    used_gather = jnp.maximum(used * per, 1).astype(jnp.int32).reshape(1)
    xs = gather_rows(slot_tok, gather_rows_valid, used_gather, xn, n_slots, tr)
    ys = expert_ffn(tile_expert, tile_valid, tile_block, xs, w1, b1, w2, b2, tm, MOE_ROW_BLOCKS)
    return combine_final(dest_flat, ys, x2, meta, final_gain)


def kernel(x, norm_mix_gain, w_in, hgrn_lb_logits, hgrn_norm_gain, hy_conv_w, hy_conv_b, hy_filt_w1,
           hy_filt_b1, hy_filt_w2, hy_filt_b2, hy_filt_w3, hy_filt_b3, hy_filt_w4, hy_filt_freq, hy_bias,
           hy_norm_gain, w_out, norm_ffn_gain, w_router, b_router, moe_w1, moe_b1, moe_w2, moe_b2,
           final_norm_gain):
    b, l, d = x.shape
    n = b * l
    x2d = x.reshape(n, d)
    proj = in_projection(x2d, norm_mix_gain[0], w_in[0].astype(BF16))
    proj3 = proj.reshape(b, l, D_IN)
    hg = hgrn2_mixer(proj3, hgrn_lb_logits, hgrn_norm_gain[0])

    sym, flt = (jnp.asarray(tab).astype(BF16) for tab in _dft_tables(l))
    tap_sum, tap_diff = hyena_filter_taps(l, hy_filt_w1[0], hy_filt_b1[0], hy_filt_w2[0], hy_filt_b2[0],
                                          hy_filt_w3[0], hy_filt_b3[0], hy_filt_w4[0], hy_filt_freq[0])
    spectra = hyena_filter_spectra(flt, jnp.stack([tap_sum, tap_diff]))
    hy = hyena_mixer(proj3, hy_conv_w[0], hy_conv_b[0], sym, spectra, hy_bias[0])

    x2, xn, meta, counts = outproj_router(hg.reshape(n, HG_WIDTH), hy.reshape(n, HY_WIDTH), x2d,
                                          w_out[0].astype(BF16), hy_norm_gain[0], norm_ffn_gain[0],
                                          w_router[0], b_router[0])
    out = moe_ffn_final(x2, xn, meta, counts, moe_w1[0], moe_b1[0], moe_w2[0], moe_b2[0], final_norm_gain)
    return out.reshape(b, l, d)
```

```python
import functools
import math

import numpy as np
import jax
import jax.numpy as jnp
from jax import lax
from jax.experimental import pallas as pl
from jax.experimental.pallas import tpu as pltpu

F32 = jnp.float32
BF16 = jnp.bfloat16

D_MODEL = 2048
HG_WIDTH = 1024
HG_HEAD_DIM = 128
HG_HEADS = HG_WIDTH // HG_HEAD_DIM
HY_WIDTH = 1024
HY_ORDER = 2
HY_EMB = 33
HY_BANDS = (HY_EMB - 1) // 2
HY_FILTER_HIDDEN = 64
HY_FAST_DECAY_PCT = 0.3
HY_SLOW_DECAY_PCT = 1.5
HY_DECAY_TARGET = 1e-2
N_EXPERTS = 32
TOP_K = 4
D_FF = D_MODEL
SWIGLU_ALPHA = 1.702
SWIGLU_LIMIT = 7.0
NORM_EPS = 1e-5
D_IN = 5 * HG_WIDTH + (HY_ORDER + 1) * HY_WIDTH

LANES = 128
GLA_CHUNK = 128
GLA_LEVELS = (64, 32, 16, 8, 4, 2, 1)
VMEM_LIMIT = 56 * 1024 * 1024


def _cparams(sem, vmem=VMEM_LIMIT):
    return pltpu.CompilerParams(dimension_semantics=sem, vmem_limit_bytes=vmem)


def _inproj_kernel(x_ref, g_ref, w_ref, o_ref, h_s):
    @pl.when(pl.program_id(1) == 0)
    def _():
        x = x_ref[...]
        ms = jnp.mean(x * x, axis=-1, keepdims=True)
        h_s[...] = (x * lax.rsqrt(ms + NORM_EPS) * g_ref[...]).astype(BF16)

    o_ref[...] = jnp.dot(h_s[...], w_ref[...], preferred_element_type=F32)


def in_projection(x2d, gain, w_bf16, tm=1024, tn=1024):
    n, d = x2d.shape
    dout = w_bf16.shape[1]
    return pl.pallas_call(
        _inproj_kernel,
        grid=(n // tm, dout // tn),
        in_specs=[
            pl.BlockSpec((tm, d), lambda i, j: (i, 0)),
            pl.BlockSpec((1, d), lambda i, j: (0, 0)),
            pl.BlockSpec((d, tn), lambda i, j: (0, j)),
        ],
        out_specs=pl.BlockSpec((tm, tn), lambda i, j: (i, j)),
        out_shape=jax.ShapeDtypeStruct((n, dout), F32),
        scratch_shapes=[pltpu.VMEM((tm, d), BF16)],
        compiler_params=_cparams(("arbitrary", "arbitrary")),
        name="in_projection",
    )(x2d, gain.reshape(1, d), w_bf16)


def _gla_constants():
    c = GLA_CHUNK
    t = np.arange(c)[:, None]
    r = np.arange(c)[None, :]
    fwd = [r <= t, r > t]
    bwd = [r >= t, r < t]
    for m in GLA_LEVELS:
        pos = t % (2 * m)
        mid = t - pos + m
        second = pos >= m
        fwd.append(np.where(second, (r >= mid) & (r <= t), (r > t) & (r < mid)))
        bwd.append(np.where(second, (r >= mid) & (r < t), (r >= t) & (r < mid)))
    x = t ^ r
    lv = np.full((c, c), -1, np.int32)
    for j in range(int(math.log2(c))):
        lv = np.where((x >> j) == 1, j, lv)
    mf = jnp.asarray(np.concatenate(fwd, 0).astype(np.float32), BF16)
    mb = jnp.asarray(np.concatenate(bwd, 0).astype(np.float32), BF16)
    return mf, mb, jnp.asarray(lv, jnp.int32)


def _hgrn_kernel(q_ref, ff_ref, fb_ref, i_ref, g_ref, lb_ref, gain_ref, mf_ref, mb_ref, lv_ref,
                 o_ref, acc_ref, qb_ref, kb_ref, db_ref):
    c = GLA_CHUNK
    n_chunks = q_ref.shape[0] // c
    lb = lb_ref[...]
    l0, l1 = lb[0], lb[1]
    mx = jnp.maximum(l0, l1)
    e0 = jnp.exp(l0 - mx)
    e1 = jnp.exp(l1 - mx)
    p0 = e0 / (e0 + e1)
    lb_f = p0[0:1, :]
    lb_b = p0[1:2, :]
    lv = lv_ref[...]
    row = lax.broadcasted_iota(jnp.int32, (c, LANES), 0)
    nt = (((1,), (1,)), ((), ()))

    def gates(z, lower):
        f = lower + (1.0 - lower) * jax.nn.sigmoid(z)
        return 1.0 - f, jnp.log(f)

    def exponents(m_ref, g):
        g_hi = g.astype(BF16)
        g_lo = (g - g_hi.astype(F32)).astype(BF16)
        m = m_ref[...]
        return (jnp.dot(m, g_hi, preferred_element_type=F32)
                + jnp.dot(m, g_lo, preferred_element_type=F32))

    def fwd_body(ci, st):
        rows = pl.ds(pl.multiple_of(ci * c, c), c)
        qr = q_ref[rows, :]
        q = qr * jax.nn.sigmoid(qr)
        v = i_ref[rows, :]
        kf, gf = gates(ff_ref[rows, :], lb_f)
        kb, gb = gates(fb_ref[rows, :], lb_b)
        ef_all = exponents(mf_ref, gf)
        eb_all = exponents(mb_ref, gb)
        scores = jnp.zeros((c, c), F32)
        for li, m in enumerate(GLA_LEVELS):
            ef = ef_all[(2 + li) * c:(3 + li) * c]
            eb = eb_all[(2 + li) * c:(3 + li) * c]
            second = (row & m) != 0
            a = q * jnp.exp(jnp.where(second, ef, eb))
            b = jnp.where(second, kb, kf) * jnp.exp(jnp.where(second, eb, ef))
            s = lax.dot_general(a.astype(BF16), b.astype(BF16), nt, preferred_element_type=F32)
            scores = jnp.where(lv == int(math.log2(m)), s, scores)
        v_bf = v.astype(BF16)
        o = jnp.dot(scores.astype(BF16), v_bf, preferred_element_type=F32)
        o = o + jnp.sum(q * (kf + kb), axis=-1, keepdims=True) * v
        b_inc = ef_all[0:c]
        q_dec = (q * jnp.exp(b_inc)).astype(BF16)
        o = o + lax.dot_general(q_dec, st.astype(BF16), nt, preferred_element_type=F32)
        k_dec = (kf * jnp.exp(ef_all[c:2 * c])).astype(BF16)
        vt_bf = v.T.astype(BF16)
        st = st * jnp.exp(b_inc[c - 1:c, :]) + jnp.dot(vt_bf, k_dec, preferred_element_type=F32)
        acc_ref[rows, :] = o
        bb = eb_all[0:c]
        qb_ref[rows, :] = (q * jnp.exp(bb)).astype(BF16)
        kb_ref[rows, :] = (kb * jnp.exp(eb_all[c:2 * c])).astype(BF16)
        db_ref[ci] = jnp.broadcast_to(jnp.exp(bb[0:1, :]), (8, LANES))
        return st

    lax.fori_loop(0, n_chunks, fwd_body, jnp.zeros((c, c), F32), unroll=4)

    gain = gain_ref[...]

    def bwd_body(i, st):
        ci = n_chunks - 1 - i
        rows = pl.ds(pl.multiple_of(ci * c, c), c)
        o = acc_ref[rows, :] + lax.dot_general(qb_ref[rows, :], st.astype(BF16), nt,
                                               preferred_element_type=F32)
        vt_bf = i_ref[rows, :].T.astype(BF16)
        st = st * db_ref[ci][0:1, :] + jnp.dot(vt_bf, kb_ref[rows, :], preferred_element_type=F32)
        o = o * lax.rsqrt(jnp.mean(o * o, axis=-1, keepdims=True) + NORM_EPS) * gain
        gr = g_ref[rows, :]
        o_ref[rows, :] = (o * (gr * jax.nn.sigmoid(gr))).astype(o_ref.dtype)
        return st

    lax.fori_loop(0, n_chunks, bwd_body, jnp.zeros((c, c), F32), unroll=4)


def hgrn2_mixer(proj3, lb_logits, norm_gain):
    b, l, _ = proj3.shape
    h, dh = HG_HEADS, HG_HEAD_DIM
    mf, mb, lv = _gla_constants()
    nblk = mf.shape[0]

    def col(off):
        return pl.BlockSpec((None, l, dh), lambda bi, hi, off=off: (bi, 0, off + hi))

    const2 = lambda bi, hi: (0, 0)
    return pl.pallas_call(
        _hgrn_kernel,
        grid=(b, h),
        in_specs=[col(0), col(h), col(2 * h), col(3 * h), col(4 * h),
                  pl.BlockSpec((2, 2, dh), lambda bi, hi: (0, 0, hi)),
                  pl.BlockSpec((1, dh), lambda bi, hi: (0, hi)),
                  pl.BlockSpec((nblk, GLA_CHUNK), const2),
                  pl.BlockSpec((nblk, GLA_CHUNK), const2),
                  pl.BlockSpec((GLA_CHUNK, GLA_CHUNK), const2)],
        out_specs=pl.BlockSpec((None, l, dh), lambda bi, hi: (bi, 0, hi)),
        out_shape=jax.ShapeDtypeStruct((b, l, HG_WIDTH), BF16),
        scratch_shapes=[pltpu.VMEM((l, dh), F32), pltpu.VMEM((l, dh), BF16), pltpu.VMEM((l, dh), BF16),
                        pltpu.VMEM((l // GLA_CHUNK, 8, LANES), F32)],
        compiler_params=_cparams(("arbitrary", "arbitrary")),
        name="hgrn2_mixer",
    )(proj3, proj3, proj3, proj3, proj3, lb_logits, norm_gain.reshape(1, HG_WIDTH), mf, mb, lv)


@functools.lru_cache(maxsize=None)
def _dft_tables(l):
    n = 2 * l
    k2 = 2 * np.arange(l, dtype=np.int64)[:, None] + 1
    m1 = np.arange(l, dtype=np.int64)[None, :]
    ang_s = ((k2 * (2 * m1 + 1)) % (4 * n)).astype(np.float64) * (2.0 * math.pi / (4 * n))
    ang_f = ((k2 * m1) % (2 * n)).astype(np.float64) * (2.0 * math.pi / (2 * n))
    sym = np.stack([np.cos(ang_s), np.sin(ang_s)]).astype(np.float32)
    flt = np.stack([np.cos(ang_f), np.sin(ang_f)]).astype(np.float32)
    return sym, flt


def _filter_features(l):
    pos = jnp.arange(l, dtype=F32)
    t = pos / max(l - 1, 1)
    bands = jnp.linspace(1e-4, HY_BANDS - 1, HY_BANDS, dtype=F32)
    ang = (2.0 * math.pi / l) * pos[:, None] * bands[None, :]
    z = jnp.concatenate([t[:, None], jnp.cos(ang), -jnp.sin(ang)], axis=-1)
    z = jnp.pad(z, ((0, 0), (0, LANES - HY_EMB)))
    min_decay = math.log(HY_DECAY_TARGET) / HY_FAST_DECAY_PCT
    max_decay = math.log(HY_DECAY_TARGET) / HY_SLOW_DECAY_PCT
    deltas = jnp.abs(jnp.linspace(min_decay, max_decay, HY_WIDTH, dtype=F32))
    window = jnp.exp(-t[:, None] * deltas[None, :])
    return z, window


def _filter_kernel(z_ref, win_ref, w1_ref, b1_ref, w2_ref, b2_ref, w3_ref, b3_ref, w4_ref, fr_ref,
                   sum_ref, diff_ref):
    hp = lax.Precision.HIGHEST
    fr = fr_ref[...]
    h = jnp.sin(fr * (jnp.dot(z_ref[...], w1_ref[...], precision=hp, preferred_element_type=F32) + b1_ref[...]))
    h = jnp.sin(fr * (jnp.dot(h, w2_ref[...], precision=hp, preferred_element_type=F32) + b2_ref[...]))
    h = jnp.sin(fr * (jnp.dot(h, w3_ref[...], precision=hp, preferred_element_type=F32) + b3_ref[...]))
    h = jnp.dot(h, w4_ref[...], precision=hp, preferred_element_type=F32)
    win = win_ref[...]
    tl = h.shape[0]
    lag = pl.program_id(0) * tl + lax.broadcasted_iota(jnp.int32, (tl, HY_WIDTH), 0)
    for o in range(HY_ORDER):
        hf = h[:, (2 * o) * HY_WIDTH:(2 * o + 1) * HY_WIDTH] * win
        hb = h[:, (2 * o + 1) * HY_WIDTH:(2 * o + 2) * HY_WIDTH] * win
        hb = jnp.where(lag == 0, 0.0, hb)
        sum_ref[:, o * HY_WIDTH:(o + 1) * HY_WIDTH] = (hf + hb).astype(sum_ref.dtype)
        diff_ref[:, o * HY_WIDTH:(o + 1) * HY_WIDTH] = (hf - hb).astype(diff_ref.dtype)


def hyena_filter_taps(l, w1, b1, w2, b2, w3, b3, w4, freq, tl=256):
    z, window = _filter_features(l)
    w1p = jnp.pad(w1, ((0, LANES - HY_EMB), (0, 0)))
    hid = HY_FILTER_HIDDEN
    full = lambda shape: pl.BlockSpec(shape, lambda i: (0,) * len(shape))
    nout = HY_ORDER * HY_WIDTH
    return pl.pallas_call(
        _filter_kernel,
        grid=(l // tl,),
        in_specs=[pl.BlockSpec((tl, LANES), lambda i: (i, 0)),
                  pl.BlockSpec((tl, HY_WIDTH), lambda i: (i, 0)),
                  full((LANES, hid)), full((1, hid)), full((hid, hid)), full((1, hid)),
                  full((hid, hid)), full((1, hid)), full((hid, 2 * nout)), full((1, hid))],
        out_specs=[pl.BlockSpec((tl, nout), lambda i: (i, 0)), pl.BlockSpec((tl, nout), lambda i: (i, 0))],
        out_shape=[jax.ShapeDtypeStruct((l, nout), BF16), jax.ShapeDtypeStruct((l, nout), BF16)],
        compiler_params=_cparams(("arbitrary",)),
        name="hyena_filter_taps",
    )(z, window, w1p, b1.reshape(1, hid), w2, b2.reshape(1, hid), w3, b3.reshape(1, hid), w4,
      freq.reshape(1, hid))


def _spectrum_kernel(f_ref, h_ref, o_ref):
    o_ref[...] = jnp.dot(f_ref[...], h_ref[...], preferred_element_type=F32).astype(o_ref.dtype)


def hyena_filter_spectra(flt, taps, tn=512):
    _, l, nout = taps.shape
    return pl.pallas_call(
        _spectrum_kernel,
        grid=(2, nout // tn),
        in_specs=[pl.BlockSpec((None, l, l), lambda p, j: (p, 0, 0)),
                  pl.BlockSpec((None, l, tn), lambda p, j: (p, 0, j))],
        out_specs=pl.BlockSpec((None, l, tn), lambda p, j: (p, 0, j)),
        out_shape=jax.ShapeDtypeStruct((2, l, nout), BF16),
        compiler_params=_cparams(("arbitrary", "arbitrary")),
        name="hyena_filter_spectra",
    )(flt, taps)


def _hyena_kernel(xv_ref, x1_ref, x2_ref, cw_ref, cb_ref, sym_ref, sp0_ref, sp1_ref, bias_ref, o_ref,
                  z_s, g_s, zb_s, pa_s, pb_s, *, rb):
    l, tc = xv_ref.shape
    scale = 2.0 / (2 * l)
    first = lax.broadcasted_iota(jnp.int32, (l, tc), 0) == 0
    last = lax.broadcasted_iota(jnp.int32, (l, tc), 0) == l - 1

    def short_conv(x_ref, part):
        x = x_ref[...]
        w = cw_ref[part]
        prev = jnp.where(first, 0.0, pltpu.roll(x, 1, 0))
        nxt = jnp.where(last, 0.0, pltpu.roll(x, l - 1, 0))
        return prev * w[0:1, :] + x * w[1:2, :] + nxt * w[2:3, :] + cb_ref[part]

    z_s[...] = short_conv(xv_ref, 0)
    for o, (gate_ref, part, sp_ref) in enumerate(((x1_ref, 1, sp0_ref), (x2_ref, 2, sp1_ref))):
        zb_s[...] = z_s[...].astype(BF16)
        g_s[...] = short_conv(gate_ref, part)
        for r0 in range(0, l, rb):
            a = jnp.dot(sym_ref[0, r0:r0 + rb, :], zb_s[...], preferred_element_type=F32)
            b = jnp.dot(sym_ref[1, r0:r0 + rb, :], zb_s[...], preferred_element_type=F32)
            hr = sp_ref[0, r0:r0 + rb, :].astype(F32)
            hs = sp_ref[1, r0:r0 + rb, :].astype(F32)
            pa_s[r0:r0 + rb, :] = (a * hr - b * hs).astype(BF16)
            pb_s[r0:r0 + rb, :] = (a * hs + b * hr).astype(BF16)
        for r0 in range(0, l, rb):
            y = (jnp.dot(sym_ref[0, r0:r0 + rb, :], pa_s[...], preferred_element_type=F32)
                 + jnp.dot(sym_ref[1, r0:r0 + rb, :], pb_s[...], preferred_element_type=F32)) * scale
            y = y + z_s[r0:r0 + rb, :] * bias_ref[o]
            z_s[r0:r0 + rb, :] = g_s[r0:r0 + rb, :] * y
    o_ref[...] = z_s[...].astype(o_ref.dtype)


def hyena_mixer(proj3, conv_w, conv_b, sym, spectra, hy_bias, tc=256, rb=1024):
    b, l, _ = proj3.shape
    nct = HY_WIDTH // tc
    base = (5 * HG_WIDTH) // tc

    def col(part):
        return pl.BlockSpec((None, l, tc), lambda ci, bi, part=part: (bi, 0, base + part * nct + ci))

    def spec_cols(o):
        return pl.BlockSpec((2, l, tc), lambda ci, bi, o=o: (0, 0, o * nct + ci))

    cw = conv_w.reshape(3, HY_ORDER + 1, HY_WIDTH).transpose(1, 0, 2)
    cb = conv_b.reshape(HY_ORDER + 1, 1, HY_WIDTH)
    return pl.pallas_call(
        functools.partial(_hyena_kernel, rb=rb),
        grid=(nct, b),
        in_specs=[col(0), col(1), col(2),
                  pl.BlockSpec((HY_ORDER + 1, 3, tc), lambda ci, bi: (0, 0, ci)),
                  pl.BlockSpec((HY_ORDER + 1, 1, tc), lambda ci, bi: (0, 0, ci)),
                  pl.BlockSpec((2, l, l), lambda ci, bi: (0, 0, 0), pipeline_mode=pl.Buffered(1)),
                  spec_cols(0), spec_cols(1),
                  pl.BlockSpec((HY_ORDER, 1, tc), lambda ci, bi: (0, 0, ci))],
        out_specs=pl.BlockSpec((None, l, tc), lambda ci, bi: (bi, 0, ci)),
        out_shape=jax.ShapeDtypeStruct((b, l, HY_WIDTH), BF16),
        scratch_shapes=[pltpu.VMEM((l, tc), F32), pltpu.VMEM((l, tc), F32), pltpu.VMEM((l, tc), BF16),
                        pltpu.VMEM((l, tc), BF16), pltpu.VMEM((l, tc), BF16)],
        compiler_params=_cparams(("arbitrary", "arbitrary")),
        name="hyena_mixer",
    )(proj3, proj3, proj3, cw, cb, sym, spectra, spectra, hy_bias.reshape(HY_ORDER, 1, HY_WIDTH))


def _outproj_router_kernel(hg_ref, hy_ref, x_ref, w_ref, hyg_ref, fg_ref, wr_ref, br_ref, tri_ref,
                           x2_ref, xn_ref, meta_ref, metat_ref, cnt_ref, carry_ref):
    @pl.when(pl.program_id(0) == 0)
    def _():
        carry_ref[...] = jnp.zeros_like(carry_ref)

    hy = hy_ref[...].astype(F32)
    hy = hy * lax.rsqrt(jnp.mean(hy * hy, axis=-1, keepdims=True) + NORM_EPS) * hyg_ref[...]
    m = (jnp.dot(hg_ref[...], w_ref[0:HG_WIDTH, :], preferred_element_type=F32)
         + jnp.dot(hy.astype(BF16), w_ref[HG_WIDTH:, :], preferred_element_type=F32))
    x2 = x_ref[...] + m
    x2_ref[...] = x2
    xn = x2 * lax.rsqrt(jnp.mean(x2 * x2, axis=-1, keepdims=True) + NORM_EPS) * fg_ref[...]
    xn_ref[...] = xn
    xn_hi = xn.astype(BF16)
    xn_lo = (xn - xn_hi.astype(F32)).astype(BF16)
    wr = wr_ref[...]
    wr_hi = wr.astype(BF16)
    wr_lo = (wr - wr_hi.astype(F32)).astype(BF16)
    logits = (jnp.dot(xn_hi, wr_hi, preferred_element_type=F32)
              + jnp.dot(xn_hi, wr_lo, preferred_element_type=F32)
              + jnp.dot(xn_lo, wr_hi, preferred_element_type=F32)) + br_ref[...]
    tm = logits.shape[0]
    lane = lax.broadcasted_iota(jnp.int32, (tm, LANES), 1).astype(F32)
    neg = jnp.float32(-jnp.inf)
    logits = jnp.where(lane < N_EXPERTS, logits, neg)
    tops, idxs = [], []
    for _ in range(TOP_K):
        mx = jnp.max(logits, axis=-1, keepdims=True)
        idx = jnp.min(jnp.where(logits == mx, lane, float(LANES)), axis=-1, keepdims=True)
        tops.append(mx)
        idxs.append(idx)
        logits = jnp.where(lane == idx, neg, logits)
    exps = [jnp.exp(t - tops[0]) for t in tops]
    denom = exps[0] + exps[1] + exps[2] + exps[3]
    onehot = jnp.zeros((tm, LANES), F32)
    for idx in idxs:
        onehot = onehot + jnp.where(lane == idx, 1.0, 0.0)
    cum = jnp.dot(tri_ref[...], onehot.astype(BF16), preferred_element_type=F32) + carry_ref[0:1, :]
    meta = jnp.zeros((tm, LANES), F32)
    for k in range(TOP_K):
        rank = jnp.sum(jnp.where(lane == idxs[k], cum, 0.0), axis=-1, keepdims=True)
        meta = jnp.where(lane == k, idxs[k], meta)
        meta = jnp.where(lane == TOP_K + k, rank, meta)
        meta = jnp.where(lane == 2 * TOP_K + k, exps[k] / denom, meta)
    meta_ref[...] = meta
    metat_ref[...] = meta.T[0:metat_ref.shape[0], :]
    carry = carry_ref[...] + jnp.sum(onehot, axis=0, keepdims=True)
    carry_ref[...] = carry
    cnt_ref[...] = carry


def outproj_router(hg2d, hy2d, x2d, w_out_bf16, hy_gain, ffn_gain, w_router, b_router, tm=512):
    n, d = x2d.shape
    wr = jnp.pad(w_router, ((0, 0), (0, LANES - N_EXPERTS)))
    br = jnp.pad(b_router, (0, LANES - N_EXPERTS)).reshape(1, LANES)
    tri = jnp.asarray(np.tril(np.ones((tm, tm), np.float32), -1), BF16)
    row = lambda w: pl.BlockSpec((tm, w), lambda i: (i, 0))
    full = lambda shape, **kw: pl.BlockSpec(shape, lambda i: (0,) * len(shape), **kw)
    return pl.pallas_call(
        _outproj_router_kernel,
        grid=(n // tm,),
        in_specs=[row(HG_WIDTH), row(HY_WIDTH), row(d),
                  full((d, d), pipeline_mode=pl.Buffered(1)),
                  full((1, HY_WIDTH)), full((1, d)), full((d, LANES)), full((1, LANES)), full((tm, tm))],
        out_specs=[row(d), row(d), row(LANES), pl.BlockSpec((2 * TOP_K, tm), lambda i: (0, i)),
                   full((8, LANES))],
        out_shape=[jax.ShapeDtypeStruct((n, d), F32), jax.ShapeDtypeStruct((n, d), F32),
                   jax.ShapeDtypeStruct((n, LANES), F32), jax.ShapeDtypeStruct((2 * TOP_K, n), F32),
                   jax.ShapeDtypeStruct((8, LANES), F32)],
        scratch_shapes=[pltpu.VMEM((8, LANES), F32)],
        compiler_params=_cparams(("arbitrary",)),
        name="outproj_router",
    )(hg2d, hy2d, x2d, w_out_bf16, hy_gain.reshape(1, HY_WIDTH), ffn_gain.reshape(1, d), wr, br, tri)


GATHER_UNROLL = 8


def _row_gather_kernel(dest_ref, rows_ref, used_ref, x_hbm, o_ref, slot_s, buf_ref, sems, *, n_tok):
    t = pl.program_id(0)
    tr = buf_ref.shape[1]
    used = used_ref[0]

    def n_copies(tile):
        return ((rows_ref[tile] + GATHER_UNROLL - 1) // GATHER_UNROLL) * GATHER_UNROLL

    def source_row(tile, r):
        return slot_s[tile * tr + jnp.minimum(r, rows_ref[tile] - 1)]

    def issue(tile):
        nc = n_copies(tile)
        for b in range(2):
            @pl.when(tile % 2 == b)
            def _(b=b):
                for g in range(tr // GATHER_UNROLL):
                    @pl.when(g * GATHER_UNROLL < nc)
                    def _(g=g):
                        for u in range(GATHER_UNROLL):
                            r = g * GATHER_UNROLL + u
                            pltpu.make_async_copy(x_hbm.at[pl.ds(source_row(tile, r), 1)],
                                                  buf_ref.at[b, pl.ds(r, 1)], sems.at[b]).start()

    @pl.when(t == 0)
    def _():
        def scatter(a, carry):
            tok = (a & (n_tok - 1)) if n_tok & (n_tok - 1) == 0 else lax.rem(a, jnp.int32(n_tok))
            slot_s[dest_ref[a]] = tok
            return carry

        lax.fori_loop(0, dest_ref.shape[0], scatter, 0, unroll=16)
        buf_ref[...] = jnp.zeros(buf_ref.shape, buf_ref.dtype)

        def body(r, carry):
            pltpu.make_async_copy(x_hbm.at[pl.ds(source_row(0, r), 1)], buf_ref.at[0, pl.ds(r, 1)],
                                  sems.at[0]).start()
            return carry

        lax.fori_loop(0, n_copies(0), body, 0)

    @pl.when(t + 1 < used)
    def _():
        issue(t + 1)

    @pl.when(t < used)
    def _():
        b = t % 2
        nc = n_copies(t)

        @pl.when(nc > 0)
        def _():
            pltpu.make_async_copy(x_hbm.at[pl.ds(0, nc)], buf_ref.at[b, pl.ds(0, nc)], sems.at[b]).wait()

        row = lax.broadcasted_iota(jnp.int32, o_ref.shape, 0)
        o_ref[...] = jnp.where(row < rows_ref[t], buf_ref[b], 0.0).astype(o_ref.dtype)


def gather_rows(dest_flat, tile_rows, used_tiles, xn, n_slots, tr):
    n, d = xn.shape
    grid_spec = pltpu.PrefetchScalarGridSpec(
        num_scalar_prefetch=3,
        grid=(n_slots // tr,),
        in_specs=[pl.BlockSpec(memory_space=pl.ANY)],
        out_specs=pl.BlockSpec((tr, d), lambda t, dest, rows, used: (jnp.minimum(t, used[0] - 1), 0)),
        scratch_shapes=[pltpu.SMEM((n_slots,), jnp.int32), pltpu.VMEM((2, tr, d), F32),
                        pltpu.SemaphoreType.DMA((2,))],
    )
    return pl.pallas_call(
        functools.partial(_row_gather_kernel, n_tok=n),
        grid_spec=grid_spec,
        out_shape=jax.ShapeDtypeStruct((n_slots, d), BF16),
        compiler_params=pltpu.CompilerParams(dimension_semantics=("arbitrary",), vmem_limit_bytes=VMEM_LIMIT,
                                             disable_bounds_checks=True),
        name="moe_gather_rows",
    )(dest_flat, tile_rows, used_tiles, xn)


def _expert_kernel(te1_ref, tv1_ref, tb1_ref, te2_ref, tv2_ref, tb2_ref,
                   x_ref, wg_ref, wl_ref, bg_ref, bl_ref, w2_ref, b2_ref, o_ref, act_s, *, nf, blocks):
    jj = pl.program_id(0)
    s = pl.program_id(1)
    tf = wg_ref.shape[1]
    valid_in = tv1_ref[jj]
    valid_out = tv2_ref[jj]
    buf_in = jj % 2
    buf_out = 1 - buf_in

    @pl.when(valid_in > 0)
    def _():
        for r0, rn in blocks:
            @pl.when(r0 < valid_in)
            def _(r0=r0, rn=rn):
                x = x_ref[r0:r0 + rn, :]
                glu = jnp.dot(x, wg_ref[...].astype(BF16), preferred_element_type=F32) + bg_ref[...]
                lin = jnp.dot(x, wl_ref[...].astype(BF16), preferred_element_type=F32) + bl_ref[...]
                glu = jnp.minimum(glu, SWIGLU_LIMIT)
                lin = jnp.clip(lin, -SWIGLU_LIMIT, SWIGLU_LIMIT)
                act = glu * jax.nn.sigmoid(SWIGLU_ALPHA * glu) * (lin + 1.0)
                act_s[buf_in, s, r0:r0 + rn, :] = act.astype(BF16)

    @pl.when(valid_out > 0)
    def _():
        for r0, rn in blocks:
            @pl.when(r0 < valid_out)
            def _(r0=r0, rn=rn):
                y = b2_ref[...] + jnp.dot(act_s[buf_out, 0, r0:r0 + rn, :], w2_ref[0:tf, :].astype(BF16),
                                          preferred_element_type=F32)
                for f in range(1, nf):
                    y = y + jnp.dot(act_s[buf_out, f, r0:r0 + rn, :],
                                    w2_ref[f * tf:(f + 1) * tf, :].astype(BF16),
                                    preferred_element_type=F32)
                o_ref[r0:r0 + rn, :] = y

            @pl.when(r0 >= valid_out)
            def _(r0=r0, rn=rn):
                o_ref[r0:r0 + rn, :] = jnp.zeros((rn, o_ref.shape[1]), o_ref.dtype)


def expert_ffn(tile_expert, tile_valid, tile_block, xs, w1, b1, w2, b2, tm, blocks, tf=512):
    n_slots, d = xs.shape
    n_tiles = n_slots // tm
    nf = D_FF // tf
    assert d // tf == nf
    b1r = b1.reshape(N_EXPERTS, 1, 2 * D_FF)
    b2r = b2.reshape(N_EXPERTS, 1, d)
    zero = jnp.zeros((1,), jnp.int32)
    te1 = jnp.concatenate([tile_expert, tile_expert[-1:]])
    tv1 = jnp.concatenate([tile_valid, zero])
    tb1 = jnp.concatenate([tile_block, tile_block[-1:]])
    te2 = jnp.concatenate([tile_expert[:1], tile_expert])
    tv2 = jnp.concatenate([zero, tile_valid])
    tb2 = jnp.concatenate([tile_block[:1], tile_block])

    def blk1(jj, s, tv1):
        return jnp.where(tv1[jj] > 0, s, nf - 1)

    def blk2(jj, s, tv2):
        return jnp.where(tv2[jj] > 0, s, jnp.where(jj == 0, 0, nf - 1))

    grid_spec = pltpu.PrefetchScalarGridSpec(
        num_scalar_prefetch=6,
        grid=(n_tiles + 1, nf),
        in_specs=[
            pl.BlockSpec((tm, d), lambda jj, s, te1, tv1, tb1, te2, tv2, tb2: (tb1[jj], 0)),
            pl.BlockSpec((None, d, tf),
                         lambda jj, s, te1, tv1, tb1, te2, tv2, tb2: (te1[jj], 0, blk1(jj, s, tv1))),
            pl.BlockSpec((None, d, tf),
                         lambda jj, s, te1, tv1, tb1, te2, tv2, tb2: (te1[jj], 0, nf + blk1(jj, s, tv1))),
            pl.BlockSpec((None, 1, tf),
                         lambda jj, s, te1, tv1, tb1, te2, tv2, tb2: (te1[jj], 0, blk1(jj, s, tv1))),
            pl.BlockSpec((None, 1, tf),
                         lambda jj, s, te1, tv1, tb1, te2, tv2, tb2: (te1[jj], 0, nf + blk1(jj, s, tv1))),
            pl.BlockSpec((None, D_FF, tf),
                         lambda jj, s, te1, tv1, tb1, te2, tv2, tb2: (te2[jj], 0, blk2(jj, s, tv2))),
            pl.BlockSpec((None, 1, tf),
                         lambda jj, s, te1, tv1, tb1, te2, tv2, tb2: (te2[jj], 0, blk2(jj, s, tv2))),
        ],
        out_specs=pl.BlockSpec((tm, tf),
                               lambda jj, s, te1, tv1, tb1, te2, tv2, tb2: (tb2[jj], blk2(jj, s, tv2))),
        scratch_shapes=[pltpu.VMEM((2, nf, tm, tf), BF16)],
    )
    return pl.pallas_call(
        functools.partial(_expert_kernel, nf=nf, blocks=blocks),
        grid_spec=grid_spec,
        out_shape=jax.ShapeDtypeStruct((n_slots, d), F32),
        compiler_params=_cparams(("arbitrary", "arbitrary")),
        name="moe_expert_ffn",
    )(te1, tv1, tb1, te2, tv2, tb2, xs, w1, w1, b1r, b1r, w2, b2r)


def _combine_kernel(dest_ref, ys_hbm, x2_ref, meta_ref, gain_ref, o_ref, buf_ref, sems):
    t = pl.program_id(0)
    nt = pl.num_programs(0)
    tt = x2_ref.shape[0]
    n_tok = dest_ref.shape[0] // TOP_K

    def issue_rolled(tile, b):
        base = tile * tt

        def body(r, carry):
            for k in range(TOP_K):
                pltpu.make_async_copy(ys_hbm.at[pl.ds(dest_ref[k * n_tok + base + r], 1)],
                                      buf_ref.at[b, k, pl.ds(r, 1)], sems.at[b]).start()
            return carry

        lax.fori_loop(0, tt, body, 0, unroll=4)

    def issue(tile):
        base = tile * tt
        for b in range(2):
            @pl.when(tile % 2 == b)
            def _(b=b):
                for r in range(tt):
                    for k in range(TOP_K):
                        pltpu.make_async_copy(ys_hbm.at[pl.ds(dest_ref[k * n_tok + base + r], 1)],
                                              buf_ref.at[b, k, pl.ds(r, 1)], sems.at[b]).start()

    @pl.when(t == 0)
    def _():
        issue_rolled(t, 0)

    @pl.when(t + 1 < nt)
    def _():
        issue(t + 1)

    b = t % 2
    for k in range(TOP_K):
        pltpu.make_async_copy(ys_hbm.at[pl.ds(0, tt)], buf_ref.at[b, k], sems.at[b]).wait()
    meta = meta_ref[...]
    y = x2_ref[...]
    for k in range(TOP_K):
        y = y + meta[:, 2 * TOP_K + k:2 * TOP_K + k + 1] * buf_ref[b, k]
    o_ref[...] = y * lax.rsqrt(jnp.mean(y * y, axis=-1, keepdims=True) + NORM_EPS) * gain_ref[...]


def combine_final(dest_flat, ys, x2, meta, final_gain, tt=128):
    n, d = x2.shape
    grid_spec = pltpu.PrefetchScalarGridSpec(
        num_scalar_prefetch=1,
        grid=(n // tt,),
        in_specs=[pl.BlockSpec(memory_space=pl.ANY),
                  pl.BlockSpec((tt, d), lambda t, dest: (t, 0)),
                  pl.BlockSpec((tt, LANES), lambda t, dest: (t, 0)),
                  pl.BlockSpec((1, d), lambda t, dest: (0, 0))],
        out_specs=pl.BlockSpec((tt, d), lambda t, dest: (t, 0)),
        scratch_shapes=[pltpu.VMEM((2, TOP_K, tt, d), F32), pltpu.SemaphoreType.DMA((2,))],
    )
    return pl.pallas_call(
        _combine_kernel,
        grid_spec=grid_spec,
        out_shape=jax.ShapeDtypeStruct((n, d), F32),
        compiler_params=pltpu.CompilerParams(dimension_semantics=("arbitrary",), vmem_limit_bytes=VMEM_LIMIT,
                                             disable_bounds_checks=True),
        name="moe_combine_final",
    )(dest_flat, ys, x2, meta, final_gain.reshape(1, d))


MOE_TILE = 1152
MOE_ROW_BLOCKS = ((0, 512), (512, 512), (1024, 128))
MOE_GATHER_TILE = 384


def moe_ffn_final(x2, xn, meta, meta_t, counts, w1, b1, w2, b2, final_gain, tm=MOE_TILE):
    n, d = x2.shape
    a = n * TOP_K
    n_tiles = -(-a // tm) + N_EXPERTS
    n_slots = n_tiles * tm
    idx = meta_t[0:TOP_K].astype(jnp.int32)
    rank = meta_t[TOP_K:2 * TOP_K].astype(jnp.int32)
    cnt = counts[0, :N_EXPERTS].astype(jnp.int32)
    tiles_e = (cnt + tm - 1) // tm
    tile_end = jnp.cumsum(tiles_e)
    tile_start = tile_end - tiles_e
    used = tile_end[-1]
    dest = (tile_start * tm)[idx] + rank
    dest_flat = dest.reshape(a)
    tj = jnp.arange(n_tiles, dtype=jnp.int32)
    te = jnp.minimum(jnp.sum(tj[:, None] >= tile_end[None, :], axis=1), N_EXPERTS - 1).astype(jnp.int32)
    last_e = te[jnp.maximum(used - 1, 0)]
    tile_expert = jnp.where(tj < used, te, last_e).astype(jnp.int32)
    tile_valid = jnp.where(tj < used, jnp.clip(cnt[te] - (tj - tile_start[te]) * tm, 0, tm), 0).astype(jnp.int32)
    tile_block = jnp.maximum(jnp.minimum(tj, used - 1), 0).astype(jnp.int32)

    tr = MOE_GATHER_TILE
    per = tm // tr
    gj = jnp.arange(n_tiles * per, dtype=jnp.int32)
    gather_rows_valid = jnp.clip(tile_valid[gj // per] - (gj % per) * tr, 0, tr).astype(jnp.int32)
    used_gather = jnp.maximum(used * per, 1).astype(jnp.int32).reshape(1)
    xs = gather_rows(dest_flat, gather_rows_valid, used_gather, xn, n_slots, tr)
    ys = expert_ffn(tile_expert, tile_valid, tile_block, xs, w1, b1, w2, b2, tm, MOE_ROW_BLOCKS)
    return combine_final(dest_flat, ys, x2, meta, final_gain)


def kernel(x, norm_mix_gain, w_in, hgrn_lb_logits, hgrn_norm_gain, hy_conv_w, hy_conv_b, hy_filt_w1,
           hy_filt_b1, hy_filt_w2, hy_filt_b2, hy_filt_w3, hy_filt_b3, hy_filt_w4, hy_filt_freq, hy_bias,
           hy_norm_gain, w_out, norm_ffn_gain, w_router, b_router, moe_w1, moe_b1, moe_w2, moe_b2,
           final_norm_gain):
    b, l, d = x.shape
    n = b * l
    x2d = x.reshape(n, d)
    proj = in_projection(x2d, norm_mix_gain[0], w_in[0].astype(BF16))
    proj3 = proj.reshape(b, l, D_IN)
    hg = hgrn2_mixer(proj3, hgrn_lb_logits, hgrn_norm_gain[0])

    sym, flt = (jnp.asarray(tab).astype(BF16) for tab in _dft_tables(l))
    tap_sum, tap_diff = hyena_filter_taps(l, hy_filt_w1[0], hy_filt_b1[0], hy_filt_w2[0], hy_filt_b2[0],
                                          hy_filt_w3[0], hy_filt_b3[0], hy_filt_w4[0], hy_filt_freq[0])
    spectra = hyena_filter_spectra(flt, jnp.stack([tap_sum, tap_diff]))
    hy = hyena_mixer(proj3, hy_conv_w[0], hy_conv_b[0], sym, spectra, hy_bias[0])

    x2, xn, meta, meta_t, counts = outproj_router(hg.reshape(n, HG_WIDTH), hy.reshape(n, HY_WIDTH), x2d,
                                                  w_out[0].astype(BF16), hy_norm_gain[0], norm_ffn_gain[0],
                                                  w_router[0], b_router[0])
    out = moe_ffn_final(x2, xn, meta, meta_t, counts, moe_w1[0], moe_b1[0], moe_w2[0], moe_b2[0],
                        final_norm_gain)
    return out.reshape(b, l, d)
```

```python
import functools
import math

import numpy as np
import jax
import jax.numpy as jnp
from jax import lax
from jax.experimental import pallas as pl
from jax.experimental.pallas import tpu as pltpu

F32 = jnp.float32
BF16 = jnp.bfloat16

D_MODEL = 2048
HG_WIDTH = 1024
HG_HEAD_DIM = 128
HG_HEADS = HG_WIDTH // HG_HEAD_DIM
HY_WIDTH = 1024
HY_ORDER = 2
HY_EMB = 33
HY_BANDS = (HY_EMB - 1) // 2
HY_FILTER_HIDDEN = 64
HY_FAST_DECAY_PCT = 0.3
HY_SLOW_DECAY_PCT = 1.5
HY_DECAY_TARGET = 1e-2
N_EXPERTS = 32
TOP_K = 4
D_FF = D_MODEL
SWIGLU_ALPHA = 1.702
SWIGLU_LIMIT = 7.0
NORM_EPS = 1e-5
D_IN = 5 * HG_WIDTH + (HY_ORDER + 1) * HY_WIDTH

LANES = 128
GLA_CHUNK = 128
GLA_LEVELS = (64, 32, 16, 8, 4, 2, 1)
VMEM_LIMIT = 56 * 1024 * 1024


def _cparams(sem, vmem=VMEM_LIMIT):
    return pltpu.CompilerParams(dimension_semantics=sem, vmem_limit_bytes=vmem)


def _inproj_kernel(x_ref, g_ref, w_ref, o_ref, h_s):
    @pl.when(pl.program_id(1) == 0)
    def _():
        x = x_ref[...]
        ms = jnp.mean(x * x, axis=-1, keepdims=True)
        h_s[...] = (x * lax.rsqrt(ms + NORM_EPS) * g_ref[...]).astype(BF16)

    o_ref[...] = jnp.dot(h_s[...], w_ref[...], preferred_element_type=F32)


def in_projection(x2d, gain, w_bf16, tm=1024, tn=1024):
    n, d = x2d.shape
    dout = w_bf16.shape[1]
    return pl.pallas_call(
        _inproj_kernel,
        grid=(n // tm, dout // tn),
        in_specs=[
            pl.BlockSpec((tm, d), lambda i, j: (i, 0)),
            pl.BlockSpec((1, d), lambda i, j: (0, 0)),
            pl.BlockSpec((d, tn), lambda i, j: (0, j)),
        ],
        out_specs=pl.BlockSpec((tm, tn), lambda i, j: (i, j)),
        out_shape=jax.ShapeDtypeStruct((n, dout), F32),
        scratch_shapes=[pltpu.VMEM((tm, d), BF16)],
        compiler_params=_cparams(("arbitrary", "arbitrary")),
        name="in_projection",
    )(x2d, gain.reshape(1, d), w_bf16)


def _gla_constants():
    c = GLA_CHUNK
    t = np.arange(c)[:, None]
    r = np.arange(c)[None, :]
    fwd = [r <= t, r > t]
    bwd = [r >= t, r < t]
    for m in GLA_LEVELS:
        pos = t % (2 * m)
        mid = t - pos + m
        second = pos >= m
        fwd.append(np.where(second, (r >= mid) & (r <= t), (r > t) & (r < mid)))
        bwd.append(np.where(second, (r >= mid) & (r < t), (r >= t) & (r < mid)))
    x = t ^ r
    lv = np.full((c, c), -1, np.int32)
    for j in range(int(math.log2(c))):
        lv = np.where((x >> j) == 1, j, lv)
    mf = jnp.asarray(np.concatenate(fwd, 0).astype(np.float32), BF16)
    mb = jnp.asarray(np.concatenate(bwd, 0).astype(np.float32), BF16)
    return mf, mb, jnp.asarray(lv, jnp.int32)


def _hgrn_kernel(q_ref, ff_ref, fb_ref, i_ref, g_ref, lb_ref, gain_ref, mf_ref, mb_ref, lv_ref,
                 o_ref, acc_ref, qb_ref, kb_ref, db_ref):
    c = GLA_CHUNK
    n_chunks = q_ref.shape[0] // c
    lb = lb_ref[...]
    l0, l1 = lb[0], lb[1]
    mx = jnp.maximum(l0, l1)
    e0 = jnp.exp(l0 - mx)
    e1 = jnp.exp(l1 - mx)
    p0 = e0 / (e0 + e1)
    lb_f = p0[0:1, :]
    lb_b = p0[1:2, :]
    lv = lv_ref[...]
    row = lax.broadcasted_iota(jnp.int32, (c, LANES), 0)
    nt = (((1,), (1,)), ((), ()))

    def gates(z, lower):
        f = lower + (1.0 - lower) * jax.nn.sigmoid(z)
        return 1.0 - f, jnp.log(f)

    def exponents(m_ref, g):
        g_hi = g.astype(BF16)
        g_lo = (g - g_hi.astype(F32)).astype(BF16)
        m = m_ref[...]
        return (jnp.dot(m, g_hi, preferred_element_type=F32)
                + jnp.dot(m, g_lo, preferred_element_type=F32))

    def fwd_body(ci, st):
        rows = pl.ds(pl.multiple_of(ci * c, c), c)
        qr = q_ref[rows, :]
        q = qr * jax.nn.sigmoid(qr)
        v = i_ref[rows, :]
        kf, gf = gates(ff_ref[rows, :], lb_f)
        kb, gb = gates(fb_ref[rows, :], lb_b)
        ef_all = exponents(mf_ref, gf)
        eb_all = exponents(mb_ref, gb)
        scores = jnp.zeros((c, c), F32)
        for li, m in enumerate(GLA_LEVELS):
            ef = ef_all[(2 + li) * c:(3 + li) * c]
            eb = eb_all[(2 + li) * c:(3 + li) * c]
            second = (row & m) != 0
            a = q * jnp.exp(jnp.where(second, ef, eb))
            b = jnp.where(second, kb, kf) * jnp.exp(jnp.where(second, eb, ef))
            s = lax.dot_general(a.astype(BF16), b.astype(BF16), nt, preferred_element_type=F32)
            scores = jnp.where(lv == int(math.log2(m)), s, scores)
        v_bf = v.astype(BF16)
        o = jnp.dot(scores.astype(BF16), v_bf, preferred_element_type=F32)
        o = o + jnp.sum(q * (kf + kb), axis=-1, keepdims=True) * v
        b_inc = ef_all[0:c]
        q_dec = (q * jnp.exp(b_inc)).astype(BF16)
        o = o + lax.dot_general(q_dec, st.astype(BF16), nt, preferred_element_type=F32)
        k_dec = (kf * jnp.exp(ef_all[c:2 * c])).astype(BF16)
        vt_bf = v.T.astype(BF16)
        st = st * jnp.exp(b_inc[c - 1:c, :]) + jnp.dot(vt_bf, k_dec, preferred_element_type=F32)
        acc_ref[rows, :] = o
        bb = eb_all[0:c]
        qb_ref[rows, :] = (q * jnp.exp(bb)).astype(BF16)
        kb_ref[rows, :] = (kb * jnp.exp(eb_all[c:2 * c])).astype(BF16)
        db_ref[ci] = jnp.broadcast_to(jnp.exp(bb[0:1, :]), (8, LANES))
        return st

    lax.fori_loop(0, n_chunks, fwd_body, jnp.zeros((c, c), F32), unroll=4)

    gain = gain_ref[...]

    def bwd_body(i, st):
        ci = n_chunks - 1 - i
        rows = pl.ds(pl.multiple_of(ci * c, c), c)
        o = acc_ref[rows, :] + lax.dot_general(qb_ref[rows, :], st.astype(BF16), nt,
                                               preferred_element_type=F32)
        vt_bf = i_ref[rows, :].T.astype(BF16)
        st = st * db_ref[ci][0:1, :] + jnp.dot(vt_bf, kb_ref[rows, :], preferred_element_type=F32)
        o = o * lax.rsqrt(jnp.mean(o * o, axis=-1, keepdims=True) + NORM_EPS) * gain
        gr = g_ref[rows, :]
        o_ref[rows, :] = (o * (gr * jax.nn.sigmoid(gr))).astype(o_ref.dtype)
        return st

    lax.fori_loop(0, n_chunks, bwd_body, jnp.zeros((c, c), F32), unroll=4)


def hgrn2_mixer(proj3, lb_logits, norm_gain):
    b, l, _ = proj3.shape
    h, dh = HG_HEADS, HG_HEAD_DIM
    mf, mb, lv = _gla_constants()
    nblk = mf.shape[0]

    def col(off):
        return pl.BlockSpec((None, l, dh), lambda bi, hi, off=off: (bi, 0, off + hi))

    const2 = lambda bi, hi: (0, 0)
    return pl.pallas_call(
        _hgrn_kernel,
        grid=(b, h),
        in_specs=[col(0), col(h), col(2 * h), col(3 * h), col(4 * h),
                  pl.BlockSpec((2, 2, dh), lambda bi, hi: (0, 0, hi)),
                  pl.BlockSpec((1, dh), lambda bi, hi: (0, hi)),
                  pl.BlockSpec((nblk, GLA_CHUNK), const2),
                  pl.BlockSpec((nblk, GLA_CHUNK), const2),
                  pl.BlockSpec((GLA_CHUNK, GLA_CHUNK), const2)],
        out_specs=pl.BlockSpec((None, l, dh), lambda bi, hi: (bi, 0, hi)),
        out_shape=jax.ShapeDtypeStruct((b, l, HG_WIDTH), BF16),
        scratch_shapes=[pltpu.VMEM((l, dh), F32), pltpu.VMEM((l, dh), BF16), pltpu.VMEM((l, dh), BF16),
                        pltpu.VMEM((l // GLA_CHUNK, 8, LANES), F32)],
        compiler_params=_cparams(("arbitrary", "arbitrary")),
        name="hgrn2_mixer",
    )(proj3, proj3, proj3, proj3, proj3, lb_logits, norm_gain.reshape(1, HG_WIDTH), mf, mb, lv)


@functools.lru_cache(maxsize=None)
def _dft_tables(l):
    n = 2 * l
    k2 = 2 * np.arange(l, dtype=np.int64)[:, None] + 1
    m1 = np.arange(l, dtype=np.int64)[None, :]
    ang_s = ((k2 * (2 * m1 + 1)) % (4 * n)).astype(np.float64) * (2.0 * math.pi / (4 * n))
    ang_f = ((k2 * m1) % (2 * n)).astype(np.float64) * (2.0 * math.pi / (2 * n))
    sym = np.stack([np.cos(ang_s), np.sin(ang_s)]).astype(np.float32)
    flt = np.stack([np.cos(ang_f), np.sin(ang_f)]).astype(np.float32)
    return sym, flt


def _filter_features(l):
    pos = jnp.arange(l, dtype=F32)
    t = pos / max(l - 1, 1)
    bands = jnp.linspace(1e-4, HY_BANDS - 1, HY_BANDS, dtype=F32)
    ang = (2.0 * math.pi / l) * pos[:, None] * bands[None, :]
    z = jnp.concatenate([t[:, None], jnp.cos(ang), -jnp.sin(ang)], axis=-1)
    z = jnp.pad(z, ((0, 0), (0, LANES - HY_EMB)))
    min_decay = math.log(HY_DECAY_TARGET) / HY_FAST_DECAY_PCT
    max_decay = math.log(HY_DECAY_TARGET) / HY_SLOW_DECAY_PCT
    deltas = jnp.abs(jnp.linspace(min_decay, max_decay, HY_WIDTH, dtype=F32))
    window = jnp.exp(-t[:, None] * deltas[None, :])
    return z, window


def _filter_kernel(z_ref, win_ref, w1_ref, b1_ref, w2_ref, b2_ref, w3_ref, b3_ref, w4_ref, fr_ref,
                   sum_ref, diff_ref):
    def split(a):
        hi = a.astype(BF16)
        return hi, (a - hi.astype(F32)).astype(BF16)

    def dot3(a, b):
        a_hi, a_lo = split(a)
        b_hi, b_lo = split(b)
        return (jnp.dot(a_hi, b_hi, preferred_element_type=F32) + jnp.dot(a_hi, b_lo, preferred_element_type=F32)
                + jnp.dot(a_lo, b_hi, preferred_element_type=F32))

    fr = fr_ref[...]
    h = jnp.sin(fr * (dot3(z_ref[...], w1_ref[...]) + b1_ref[...]))
    h = jnp.sin(fr * (dot3(h, w2_ref[...]) + b2_ref[...]))
    h = jnp.sin(fr * (dot3(h, w3_ref[...]) + b3_ref[...]))
    h = dot3(h, w4_ref[...])
    win = win_ref[...]
    tl = h.shape[0]
    lag = pl.program_id(0) * tl + lax.broadcasted_iota(jnp.int32, (tl, HY_WIDTH), 0)
    for o in range(HY_ORDER):
        hf = h[:, (2 * o) * HY_WIDTH:(2 * o + 1) * HY_WIDTH] * win
        hb = h[:, (2 * o + 1) * HY_WIDTH:(2 * o + 2) * HY_WIDTH] * win
        hb = jnp.where(lag == 0, 0.0, hb)
        sum_ref[:, o * HY_WIDTH:(o + 1) * HY_WIDTH] = (hf + hb).astype(sum_ref.dtype)
        diff_ref[:, o * HY_WIDTH:(o + 1) * HY_WIDTH] = (hf - hb).astype(diff_ref.dtype)


def hyena_filter_taps(l, w1, b1, w2, b2, w3, b3, w4, freq, tl=256):
    z, window = _filter_features(l)
    w1p = jnp.pad(w1, ((0, LANES - HY_EMB), (0, 0)))
    hid = HY_FILTER_HIDDEN
    full = lambda shape: pl.BlockSpec(shape, lambda i: (0,) * len(shape))
    nout = HY_ORDER * HY_WIDTH
    return pl.pallas_call(
        _filter_kernel,
        grid=(l // tl,),
        in_specs=[pl.BlockSpec((tl, LANES), lambda i: (i, 0)),
                  pl.BlockSpec((tl, HY_WIDTH), lambda i: (i, 0)),
                  full((LANES, hid)), full((1, hid)), full((hid, hid)), full((1, hid)),
                  full((hid, hid)), full((1, hid)), full((hid, 2 * nout)), full((1, hid))],
        out_specs=[pl.BlockSpec((tl, nout), lambda i: (i, 0)), pl.BlockSpec((tl, nout), lambda i: (i, 0))],
        out_shape=[jax.ShapeDtypeStruct((l, nout), BF16), jax.ShapeDtypeStruct((l, nout), BF16)],
        compiler_params=_cparams(("arbitrary",)),
        name="hyena_filter_taps",
    )(z, window, w1p, b1.reshape(1, hid), w2, b2.reshape(1, hid), w3, b3.reshape(1, hid), w4,
      freq.reshape(1, hid))


def _spectrum_kernel(f_ref, h_ref, o_ref):
    o_ref[...] = jnp.dot(f_ref[...], h_ref[...], preferred_element_type=F32).astype(o_ref.dtype)


def hyena_filter_spectra(flt, taps, tn=512):
    _, l, nout = taps.shape
    return pl.pallas_call(
        _spectrum_kernel,
        grid=(2, nout // tn),
        in_specs=[pl.BlockSpec((None, l, l), lambda p, j: (p, 0, 0)),
                  pl.BlockSpec((None, l, tn), lambda p, j: (p, 0, j))],
        out_specs=pl.BlockSpec((None, l, tn), lambda p, j: (p, 0, j)),
        out_shape=jax.ShapeDtypeStruct((2, l, nout), BF16),
        compiler_params=_cparams(("arbitrary", "arbitrary")),
        name="hyena_filter_spectra",
    )(flt, taps)


def _hyena_kernel(xv_ref, x1_ref, x2_ref, cw_ref, cb_ref, sym_ref, sp0_ref, sp1_ref, bias_ref, o_ref,
                  z_s, g_s, zb_s, pa_s, pb_s, *, rb):
    l, tc = xv_ref.shape
    scale = 2.0 / (2 * l)
    first = lax.broadcasted_iota(jnp.int32, (l, tc), 0) == 0
    last = lax.broadcasted_iota(jnp.int32, (l, tc), 0) == l - 1

    def short_conv(x_ref, part):
        x = x_ref[...]
        w = cw_ref[part]
        prev = jnp.where(first, 0.0, pltpu.roll(x, 1, 0))
        nxt = jnp.where(last, 0.0, pltpu.roll(x, l - 1, 0))
        return prev * w[0:1, :] + x * w[1:2, :] + nxt * w[2:3, :] + cb_ref[part]

    z_s[...] = short_conv(xv_ref, 0)
    for o, (gate_ref, part, sp_ref) in enumerate(((x1_ref, 1, sp0_ref), (x2_ref, 2, sp1_ref))):
        zb_s[...] = z_s[...].astype(BF16)
        g_s[...] = short_conv(gate_ref, part)
        for r0 in range(0, l, rb):
            a = jnp.dot(sym_ref[0, r0:r0 + rb, :], zb_s[...], preferred_element_type=F32)
            b = jnp.dot(sym_ref[1, r0:r0 + rb, :], zb_s[...], preferred_element_type=F32)
            hr = sp_ref[0, r0:r0 + rb, :].astype(F32)
            hs = sp_ref[1, r0:r0 + rb, :].astype(F32)
            pa_s[r0:r0 + rb, :] = (a * hr - b * hs).astype(BF16)
            pb_s[r0:r0 + rb, :] = (a * hs + b * hr).astype(BF16)
        for r0 in range(0, l, rb):
            y = (jnp.dot(sym_ref[0, r0:r0 + rb, :], pa_s[...], preferred_element_type=F32)
                 + jnp.dot(sym_ref[1, r0:r0 + rb, :], pb_s[...], preferred_element_type=F32)) * scale
            y = y + z_s[r0:r0 + rb, :] * bias_ref[o]
            z_s[r0:r0 + rb, :] = g_s[r0:r0 + rb, :] * y
    o_ref[...] = z_s[...].astype(o_ref.dtype)


def hyena_mixer(proj3, conv_w, conv_b, sym, spectra, hy_bias, tc=256, rb=1024):
    b, l, _ = proj3.shape
    nct = HY_WIDTH // tc
    base = (5 * HG_WIDTH) // tc

    def col(part):
        return pl.BlockSpec((None, l, tc), lambda ci, bi, part=part: (bi, 0, base + part * nct + ci))

    def spec_cols(o):
        return pl.BlockSpec((2, l, tc), lambda ci, bi, o=o: (0, 0, o * nct + ci))

    cw = conv_w.reshape(3, HY_ORDER + 1, HY_WIDTH).transpose(1, 0, 2)
    cb = conv_b.reshape(HY_ORDER + 1, 1, HY_WIDTH)
    return pl.pallas_call(
        functools.partial(_hyena_kernel, rb=rb),
        grid=(nct, b),
        in_specs=[col(0), col(1), col(2),
                  pl.BlockSpec((HY_ORDER + 1, 3, tc), lambda ci, bi: (0, 0, ci)),
                  pl.BlockSpec((HY_ORDER + 1, 1, tc), lambda ci, bi: (0, 0, ci)),
                  pl.BlockSpec((2, l, l), lambda ci, bi: (0, 0, 0), pipeline_mode=pl.Buffered(1)),
                  spec_cols(0), spec_cols(1),
                  pl.BlockSpec((HY_ORDER, 1, tc), lambda ci, bi: (0, 0, ci))],
        out_specs=pl.BlockSpec((None, l, tc), lambda ci, bi: (bi, 0, ci)),
        out_shape=jax.ShapeDtypeStruct((b, l, HY_WIDTH), BF16),
        scratch_shapes=[pltpu.VMEM((l, tc), F32), pltpu.VMEM((l, tc), F32), pltpu.VMEM((l, tc), BF16),
                        pltpu.VMEM((l, tc), BF16), pltpu.VMEM((l, tc), BF16)],
        compiler_params=_cparams(("arbitrary", "arbitrary")),
        name="hyena_mixer",
    )(proj3, proj3, proj3, cw, cb, sym, spectra, spectra, hy_bias.reshape(HY_ORDER, 1, HY_WIDTH))


def _outproj_router_kernel(hg_ref, hy_ref, x_ref, w_ref, hyg_ref, fg_ref, wr_ref, br_ref, tri_ref,
                           x2_ref, xn_ref, meta_ref, metat_ref, cnt_ref, carry_ref):
    @pl.when(pl.program_id(0) == 0)
    def _():
        carry_ref[...] = jnp.zeros_like(carry_ref)

    hy = hy_ref[...].astype(F32)
    hy = hy * lax.rsqrt(jnp.mean(hy * hy, axis=-1, keepdims=True) + NORM_EPS) * hyg_ref[...]
    m = (jnp.dot(hg_ref[...], w_ref[0:HG_WIDTH, :], preferred_element_type=F32)
         + jnp.dot(hy.astype(BF16), w_ref[HG_WIDTH:, :], preferred_element_type=F32))
    x2 = x_ref[...] + m
    x2_ref[...] = x2
    xn = x2 * lax.rsqrt(jnp.mean(x2 * x2, axis=-1, keepdims=True) + NORM_EPS) * fg_ref[...]
    xn_ref[...] = xn
    xn_hi = xn.astype(BF16)
    xn_lo = (xn - xn_hi.astype(F32)).astype(BF16)
    wr = wr_ref[...]
    wr_hi = wr.astype(BF16)
    wr_lo = (wr - wr_hi.astype(F32)).astype(BF16)
    logits = (jnp.dot(xn_hi, wr_hi, preferred_element_type=F32)
              + jnp.dot(xn_hi, wr_lo, preferred_element_type=F32)
              + jnp.dot(xn_lo, wr_hi, preferred_element_type=F32)) + br_ref[...]
    tm = logits.shape[0]
    lane = lax.broadcasted_iota(jnp.int32, (tm, LANES), 1).astype(F32)
    neg = jnp.float32(-jnp.inf)
    logits = jnp.where(lane < N_EXPERTS, logits, neg)
    tops, idxs = [], []
    for _ in range(TOP_K):
        mx = jnp.max(logits, axis=-1, keepdims=True)
        idx = jnp.min(jnp.where(logits == mx, lane, float(LANES)), axis=-1, keepdims=True)
        tops.append(mx)
        idxs.append(idx)
        logits = jnp.where(lane == idx, neg, logits)
    exps = [jnp.exp(t - tops[0]) for t in tops]
    denom = exps[0] + exps[1] + exps[2] + exps[3]
    onehot = jnp.zeros((tm, LANES), F32)
    for idx in idxs:
        onehot = onehot + jnp.where(lane == idx, 1.0, 0.0)
    cum = jnp.dot(tri_ref[...], onehot.astype(BF16), preferred_element_type=F32) + carry_ref[0:1, :]
    meta = jnp.zeros((tm, LANES), F32)
    for k in range(TOP_K):
        rank = jnp.sum(jnp.where(lane == idxs[k], cum, 0.0), axis=-1, keepdims=True)
        meta = jnp.where(lane == k, idxs[k], meta)
        meta = jnp.where(lane == TOP_K + k, rank, meta)
        meta = jnp.where(lane == 2 * TOP_K + k, exps[k] / denom, meta)
    meta_ref[...] = meta
    metat_ref[...] = meta.T[0:metat_ref.shape[0], :]
    carry = carry_ref[...] + jnp.sum(onehot, axis=0, keepdims=True)
    carry_ref[...] = carry
    cnt_ref[...] = carry


def outproj_router(hg2d, hy2d, x2d, w_out_bf16, hy_gain, ffn_gain, w_router, b_router, tm=512):
    n, d = x2d.shape
    wr = jnp.pad(w_router, ((0, 0), (0, LANES - N_EXPERTS)))
    br = jnp.pad(b_router, (0, LANES - N_EXPERTS)).reshape(1, LANES)
    tri = jnp.asarray(np.tril(np.ones((tm, tm), np.float32), -1), BF16)
    row = lambda w: pl.BlockSpec((tm, w), lambda i: (i, 0))
    full = lambda shape, **kw: pl.BlockSpec(shape, lambda i: (0,) * len(shape), **kw)
    return pl.pallas_call(
        _outproj_router_kernel,
        grid=(n // tm,),
        in_specs=[row(HG_WIDTH), row(HY_WIDTH), row(d),
                  full((d, d), pipeline_mode=pl.Buffered(1)),
                  full((1, HY_WIDTH)), full((1, d)), full((d, LANES)), full((1, LANES)), full((tm, tm))],
        out_specs=[row(d), row(d), row(LANES), pl.BlockSpec((2 * TOP_K, tm), lambda i: (0, i)),
                   full((8, LANES))],
        out_shape=[jax.ShapeDtypeStruct((n, d), F32), jax.ShapeDtypeStruct((n, d), F32),
                   jax.ShapeDtypeStruct((n, LANES), F32), jax.ShapeDtypeStruct((2 * TOP_K, n), F32),
                   jax.ShapeDtypeStruct((8, LANES), F32)],
        scratch_shapes=[pltpu.VMEM((8, LANES), F32)],
        compiler_params=_cparams(("arbitrary",)),
        name="outproj_router",
    )(hg2d, hy2d, x2d, w_out_bf16, hy_gain.reshape(1, HY_WIDTH), ffn_gain.reshape(1, d), wr, br, tri)


GATHER_UNROLL = 8


def _row_gather_kernel(dest_ref, rows_ref, used_ref, x_hbm, o_ref, slot_s, buf_ref, sems, *, n_tok):
    t = pl.program_id(0)
    tr = buf_ref.shape[1]
    used = used_ref[0]

    def n_copies(tile):
        return ((rows_ref[tile] + GATHER_UNROLL - 1) // GATHER_UNROLL) * GATHER_UNROLL

    def source_row(tile, r):
        return slot_s[tile * tr + jnp.minimum(r, rows_ref[tile] - 1)]

    def issue(tile):
        nc = n_copies(tile)
        for b in range(2):
            @pl.when(tile % 2 == b)
            def _(b=b):
                for g in range(tr // GATHER_UNROLL):
                    @pl.when(g * GATHER_UNROLL < nc)
                    def _(g=g):
                        for u in range(GATHER_UNROLL):
                            r = g * GATHER_UNROLL + u
                            pltpu.make_async_copy(x_hbm.at[pl.ds(source_row(tile, r), 1)],
                                                  buf_ref.at[b, pl.ds(r, 1)], sems.at[b]).start()

    @pl.when(t == 0)
    def _():
        def scatter(a, carry):
            tok = (a & (n_tok - 1)) if n_tok & (n_tok - 1) == 0 else lax.rem(a, jnp.int32(n_tok))
            slot_s[dest_ref[a]] = tok
            return carry

        lax.fori_loop(0, dest_ref.shape[0], scatter, 0, unroll=16)
        buf_ref[...] = jnp.zeros(buf_ref.shape, buf_ref.dtype)

        def body(r, carry):
            pltpu.make_async_copy(x_hbm.at[pl.ds(source_row(0, r), 1)], buf_ref.at[0, pl.ds(r, 1)],
                                  sems.at[0]).start()
            return carry

        lax.fori_loop(0, n_copies(0), body, 0)

    @pl.when(t + 1 < used)
    def _():
        issue(t + 1)

    @pl.when(t < used)
    def _():
        b = t % 2
        nc = n_copies(t)

        @pl.when(nc > 0)
        def _():
            pltpu.make_async_copy(x_hbm.at[pl.ds(0, nc)], buf_ref.at[b, pl.ds(0, nc)], sems.at[b]).wait()

        row = lax.broadcasted_iota(jnp.int32, o_ref.shape, 0)
        o_ref[...] = jnp.where(row < rows_ref[t], buf_ref[b], 0.0).astype(o_ref.dtype)


def gather_rows(dest_flat, tile_rows, used_tiles, xn, n_slots, tr):
    n, d = xn.shape
    grid_spec = pltpu.PrefetchScalarGridSpec(
        num_scalar_prefetch=3,
        grid=(n_slots // tr,),
        in_specs=[pl.BlockSpec(memory_space=pl.ANY)],
        out_specs=pl.BlockSpec((tr, d), lambda t, dest, rows, used: (jnp.minimum(t, used[0] - 1), 0)),
        scratch_shapes=[pltpu.SMEM((n_slots,), jnp.int32), pltpu.VMEM((2, tr, d), F32),
                        pltpu.SemaphoreType.DMA((2,))],
    )
    return pl.pallas_call(
        functools.partial(_row_gather_kernel, n_tok=n),
        grid_spec=grid_spec,
        out_shape=jax.ShapeDtypeStruct((n_slots, d), BF16),
        compiler_params=pltpu.CompilerParams(dimension_semantics=("arbitrary",), vmem_limit_bytes=VMEM_LIMIT,
                                             disable_bounds_checks=True),
        name="moe_gather_rows",
    )(dest_flat, tile_rows, used_tiles, xn)


def _expert_kernel(te1_ref, tv1_ref, tb1_ref, te2_ref, tv2_ref, tb2_ref,
                   x_ref, wg_ref, wl_ref, bg_ref, bl_ref, w2_ref, b2_ref, o_ref, act_s, *, nf, blocks):
    jj = pl.program_id(0)
    s = pl.program_id(1)
    tf = wg_ref.shape[1]
    valid_in = tv1_ref[jj]
    valid_out = tv2_ref[jj]
    buf_in = jj % 2
    buf_out = 1 - buf_in

    @pl.when(valid_in > 0)
    def _():
        for r0, rn in blocks:
            @pl.when(r0 < valid_in)
            def _(r0=r0, rn=rn):
                x = x_ref[r0:r0 + rn, :]
                glu = jnp.dot(x, wg_ref[...].astype(BF16), preferred_element_type=F32) + bg_ref[...]
                lin = jnp.dot(x, wl_ref[...].astype(BF16), preferred_element_type=F32) + bl_ref[...]
                glu = jnp.minimum(glu, SWIGLU_LIMIT)
                lin = jnp.clip(lin, -SWIGLU_LIMIT, SWIGLU_LIMIT)
                act = glu * jax.nn.sigmoid(SWIGLU_ALPHA * glu) * (lin + 1.0)
                act_s[buf_in, s, r0:r0 + rn, :] = act.astype(BF16)

    @pl.when(valid_out > 0)
    def _():
        for r0, rn in blocks:
            @pl.when(r0 < valid_out)
            def _(r0=r0, rn=rn):
                y = b2_ref[...] + jnp.dot(act_s[buf_out, 0, r0:r0 + rn, :], w2_ref[0:tf, :].astype(BF16),
                                          preferred_element_type=F32)
                for f in range(1, nf):
                    y = y + jnp.dot(act_s[buf_out, f, r0:r0 + rn, :],
                                    w2_ref[f * tf:(f + 1) * tf, :].astype(BF16),
                                    preferred_element_type=F32)
                o_ref[r0:r0 + rn, :] = y

            @pl.when(r0 >= valid_out)
            def _(r0=r0, rn=rn):
                o_ref[r0:r0 + rn, :] = jnp.zeros((rn, o_ref.shape[1]), o_ref.dtype)


def expert_ffn(tile_expert, tile_valid, tile_block, xs, w1, b1, w2, b2, tm, blocks, tf=512):
    n_slots, d = xs.shape
    n_tiles = n_slots // tm
    nf = D_FF // tf
    assert d // tf == nf
    b1r = b1.reshape(N_EXPERTS, 1, 2 * D_FF)
    b2r = b2.reshape(N_EXPERTS, 1, d)
    zero = jnp.zeros((1,), jnp.int32)
    te1 = jnp.concatenate([tile_expert, tile_expert[-1:]])
    tv1 = jnp.concatenate([tile_valid, zero])
    tb1 = jnp.concatenate([tile_block, tile_block[-1:]])
    te2 = jnp.concatenate([tile_expert[:1], tile_expert])
    tv2 = jnp.concatenate([zero, tile_valid])
    tb2 = jnp.concatenate([tile_block[:1], tile_block])

    def blk1(jj, s, tv1):
        return jnp.where(tv1[jj] > 0, s, nf - 1)

    def blk2(jj, s, tv2):
        return jnp.where(tv2[jj] > 0, s, jnp.where(jj == 0, 0, nf - 1))

    grid_spec = pltpu.PrefetchScalarGridSpec(
        num_scalar_prefetch=6,
        grid=(n_tiles + 1, nf),
        in_specs=[
            pl.BlockSpec((tm, d), lambda jj, s, te1, tv1, tb1, te2, tv2, tb2: (tb1[jj], 0)),
            pl.BlockSpec((None, d, tf),
                         lambda jj, s, te1, tv1, tb1, te2, tv2, tb2: (te1[jj], 0, blk1(jj, s, tv1))),
            pl.BlockSpec((None, d, tf),
                         lambda jj, s, te1, tv1, tb1, te2, tv2, tb2: (te1[jj], 0, nf + blk1(jj, s, tv1))),
            pl.BlockSpec((None, 1, tf),
                         lambda jj, s, te1, tv1, tb1, te2, tv2, tb2: (te1[jj], 0, blk1(jj, s, tv1))),
            pl.BlockSpec((None, 1, tf),
                         lambda jj, s, te1, tv1, tb1, te2, tv2, tb2: (te1[jj], 0, nf + blk1(jj, s, tv1))),
            pl.BlockSpec((None, D_FF, tf),
                         lambda jj, s, te1, tv1, tb1, te2, tv2, tb2: (te2[jj], 0, blk2(jj, s, tv2))),
            pl.BlockSpec((None, 1, tf),
                         lambda jj, s, te1, tv1, tb1, te2, tv2, tb2: (te2[jj], 0, blk2(jj, s, tv2))),
        ],
        out_specs=pl.BlockSpec((tm, tf),
                               lambda jj, s, te1, tv1, tb1, te2, tv2, tb2: (tb2[jj], blk2(jj, s, tv2))),
        scratch_shapes=[pltpu.VMEM((2, nf, tm, tf), BF16)],
    )
    return pl.pallas_call(
        functools.partial(_expert_kernel, nf=nf, blocks=blocks),
        grid_spec=grid_spec,
        out_shape=jax.ShapeDtypeStruct((n_slots, d), F32),
        compiler_params=_cparams(("arbitrary", "arbitrary")),
        name="moe_expert_ffn",
    )(te1, tv1, tb1, te2, tv2, tb2, xs, w1, w1, b1r, b1r, w2, b2r)


def _combine_kernel(dest_ref, ys_hbm, x2_ref, meta_ref, gain_ref, o_ref, buf_ref, sems):
    t = pl.program_id(0)
    nt = pl.num_programs(0)
    tt = x2_ref.shape[0]
    n_tok = dest_ref.shape[0] // TOP_K

    def issue_rolled(tile, b):
        base = tile * tt

        def body(r, carry):
            for k in range(TOP_K):
                pltpu.make_async_copy(ys_hbm.at[pl.ds(dest_ref[k * n_tok + base + r], 1)],
                                      buf_ref.at[b, k, pl.ds(r, 1)], sems.at[b]).start()
            return carry

        lax.fori_loop(0, tt, body, 0, unroll=4)

    def issue(tile):
        base = tile * tt
        for b in range(2):
            @pl.when(tile % 2 == b)
            def _(b=b):
                for r in range(tt):
                    for k in range(TOP_K):
                        pltpu.make_async_copy(ys_hbm.at[pl.ds(dest_ref[k * n_tok + base + r], 1)],
                                              buf_ref.at[b, k, pl.ds(r, 1)], sems.at[b]).start()

    @pl.when(t == 0)
    def _():
        issue_rolled(t, 0)

    @pl.when(t + 1 < nt)
    def _():
        issue(t + 1)

    b = t % 2
    for k in range(TOP_K):
        pltpu.make_async_copy(ys_hbm.at[pl.ds(0, tt)], buf_ref.at[b, k], sems.at[b]).wait()
    meta = meta_ref[...]
    y = x2_ref[...]
    for k in range(TOP_K):
        y = y + meta[:, 2 * TOP_K + k:2 * TOP_K + k + 1] * buf_ref[b, k]
    o_ref[...] = y * lax.rsqrt(jnp.mean(y * y, axis=-1, keepdims=True) + NORM_EPS) * gain_ref[...]


def combine_final(dest_flat, ys, x2, meta, final_gain, tt=128):
    n, d = x2.shape
    grid_spec = pltpu.PrefetchScalarGridSpec(
        num_scalar_prefetch=1,
        grid=(n // tt,),
        in_specs=[pl.BlockSpec(memory_space=pl.ANY),
                  pl.BlockSpec((tt, d), lambda t, dest: (t, 0)),
                  pl.BlockSpec((tt, LANES), lambda t, dest: (t, 0)),
                  pl.BlockSpec((1, d), lambda t, dest: (0, 0))],
        out_specs=pl.BlockSpec((tt, d), lambda t, dest: (t, 0)),
        scratch_shapes=[pltpu.VMEM((2, TOP_K, tt, d), F32), pltpu.SemaphoreType.DMA((2,))],
    )
    return pl.pallas_call(
        _combine_kernel,
        grid_spec=grid_spec,
        out_shape=jax.ShapeDtypeStruct((n, d), F32),
        compiler_params=pltpu.CompilerParams(dimension_semantics=("arbitrary",), vmem_limit_bytes=VMEM_LIMIT,
                                             disable_bounds_checks=True),
        name="moe_combine_final",
    )(dest_flat, ys, x2, meta, final_gain.reshape(1, d))


MOE_TILE = 1152
MOE_ROW_BLOCKS = ((0, 512), (512, 512), (1024, 128))
MOE_GATHER_TILE = 384


def moe_ffn_final(x2, xn, meta, meta_t, counts, w1, b1, w2, b2, final_gain, tm=MOE_TILE):
    n, d = x2.shape
    a = n * TOP_K
    n_tiles = -(-a // tm) + N_EXPERTS
    n_slots = n_tiles * tm
    idx = meta_t[0:TOP_K].astype(jnp.int32)
    rank = meta_t[TOP_K:2 * TOP_K].astype(jnp.int32)
    cnt = counts[0, :N_EXPERTS].astype(jnp.int32)
    tiles_e = (cnt + tm - 1) // tm
    tile_end = jnp.cumsum(tiles_e)
    tile_start = tile_end - tiles_e
    used = tile_end[-1]
    experts = jnp.arange(N_EXPERTS, dtype=jnp.int32)[:, None, None]
    first_slot = jnp.sum(jnp.where(idx[None] == experts, (tile_start * tm)[:, None, None], 0), axis=0)
    dest = first_slot + rank
    dest_flat = dest.reshape(a)
    tj = jnp.arange(n_tiles, dtype=jnp.int32)
    te = jnp.minimum(jnp.sum(tj[:, None] >= tile_end[None, :], axis=1), N_EXPERTS - 1).astype(jnp.int32)
    last_e = te[jnp.maximum(used - 1, 0)]
    tile_expert = jnp.where(tj < used, te, last_e).astype(jnp.int32)
    tile_valid = jnp.where(tj < used, jnp.clip(cnt[te] - (tj - tile_start[te]) * tm, 0, tm), 0).astype(jnp.int32)
    tile_block = jnp.maximum(jnp.minimum(tj, used - 1), 0).astype(jnp.int32)

    tr = MOE_GATHER_TILE
    per = tm // tr
    gj = jnp.arange(n_tiles * per, dtype=jnp.int32)
    gather_rows_valid = jnp.clip(tile_valid[gj // per] - (gj % per) * tr, 0, tr).astype(jnp.int32)
    used_gather = jnp.maximum(used * per, 1).astype(jnp.int32).reshape(1)
    xs = gather_rows(dest_flat, gather_rows_valid, used_gather, xn, n_slots, tr)
    ys = expert_ffn(tile_expert, tile_valid, tile_block, xs, w1, b1, w2, b2, tm, MOE_ROW_BLOCKS)
    return combine_final(dest_flat, ys, x2, meta, final_gain)


def kernel(x, norm_mix_gain, w_in, hgrn_lb_logits, hgrn_norm_gain, hy_conv_w, hy_conv_b, hy_filt_w1,
           hy_filt_b1, hy_filt_w2, hy_filt_b2, hy_filt_w3, hy_filt_b3, hy_filt_w4, hy_filt_freq, hy_bias,
           hy_norm_gain, w_out, norm_ffn_gain, w_router, b_router, moe_w1, moe_b1, moe_w2, moe_b2,
           final_norm_gain):
    b, l, d = x.shape
    n = b * l
    x2d = x.reshape(n, d)
    proj = in_projection(x2d, norm_mix_gain[0], w_in[0].astype(BF16))
    proj3 = proj.reshape(b, l, D_IN)
    hg = hgrn2_mixer(proj3, hgrn_lb_logits, hgrn_norm_gain[0])

    sym, flt = (jnp.asarray(tab).astype(BF16) for tab in _dft_tables(l))
    tap_sum, tap_diff = hyena_filter_taps(l, hy_filt_w1[0], hy_filt_b1[0], hy_filt_w2[0], hy_filt_b2[0],
                                          hy_filt_w3[0], hy_filt_b3[0], hy_filt_w4[0], hy_filt_freq[0])
    spectra = hyena_filter_spectra(flt, jnp.stack([tap_sum, tap_diff]))
    hy = hyena_mixer(proj3, hy_conv_w[0], hy_conv_b[0], sym, spectra, hy_bias[0])

    x2, xn, meta, meta_t, counts = outproj_router(hg.reshape(n, HG_WIDTH), hy.reshape(n, HY_WIDTH), x2d,
                                                  w_out[0].astype(BF16), hy_norm_gain[0], norm_ffn_gain[0],
                                                  w_router[0], b_router[0])
    out = moe_ffn_final(x2, xn, meta, meta_t, counts, moe_w1[0], moe_b1[0], moe_w2[0], moe_b2[0],
                        final_norm_gain)
    return out.reshape(b, l, d)
```

```python
import functools
import math

import numpy as np
import jax
import jax.numpy as jnp
from jax import lax
from jax.experimental import pallas as pl
from jax.experimental.pallas import tpu as pltpu

F32 = jnp.float32
BF16 = jnp.bfloat16

D_MODEL = 2048
HG_WIDTH = 1024
HG_HEAD_DIM = 128
HG_HEADS = HG_WIDTH // HG_HEAD_DIM
HY_WIDTH = 1024
HY_ORDER = 2
HY_EMB = 33
HY_BANDS = (HY_EMB - 1) // 2
HY_FILTER_HIDDEN = 64
HY_FAST_DECAY_PCT = 0.3
HY_SLOW_DECAY_PCT = 1.5
HY_DECAY_TARGET = 1e-2
N_EXPERTS = 32
TOP_K = 4
D_FF = D_MODEL
SWIGLU_ALPHA = 1.702
SWIGLU_LIMIT = 7.0
NORM_EPS = 1e-5
D_IN = 5 * HG_WIDTH + (HY_ORDER + 1) * HY_WIDTH

LANES = 128
GLA_CHUNK = 128
GLA_LEVELS = (64, 32, 16, 8, 4, 2, 1)
VMEM_LIMIT = 56 * 1024 * 1024


def _cparams(sem, vmem=VMEM_LIMIT):
    return pltpu.CompilerParams(dimension_semantics=sem, vmem_limit_bytes=vmem)


def _inproj_kernel(x_ref, g_ref, w_ref, o_ref, h_s):
    @pl.when(pl.program_id(1) == 0)
    def _():
        x = x_ref[...]
        ms = jnp.mean(x * x, axis=-1, keepdims=True)
        h_s[...] = (x * lax.rsqrt(ms + NORM_EPS) * g_ref[...]).astype(BF16)

    o_ref[...] = jnp.dot(h_s[...], w_ref[...], preferred_element_type=F32)


def in_projection(x2d, gain, w_bf16, tm=1024, tn=1024):
    n, d = x2d.shape
    dout = w_bf16.shape[1]
    return pl.pallas_call(
        _inproj_kernel,
        grid=(n // tm, dout // tn),
        in_specs=[
            pl.BlockSpec((tm, d), lambda i, j: (i, 0)),
            pl.BlockSpec((1, d), lambda i, j: (0, 0)),
            pl.BlockSpec((d, tn), lambda i, j: (0, j)),
        ],
        out_specs=pl.BlockSpec((tm, tn), lambda i, j: (i, j)),
        out_shape=jax.ShapeDtypeStruct((n, dout), F32),
        scratch_shapes=[pltpu.VMEM((tm, d), BF16)],
        compiler_params=_cparams(("arbitrary", "arbitrary")),
        name="in_projection",
    )(x2d, gain.reshape(1, d), w_bf16)


def _gla_constants():
    c = GLA_CHUNK
    t = np.arange(c)[:, None]
    r = np.arange(c)[None, :]
    fwd = [r <= t, r > t]
    bwd = [r >= t, r < t]
    for m in GLA_LEVELS:
        pos = t % (2 * m)
        mid = t - pos + m
        second = pos >= m
        fwd.append(np.where(second, (r >= mid) & (r <= t), (r > t) & (r < mid)))
        bwd.append(np.where(second, (r >= mid) & (r < t), (r >= t) & (r < mid)))
    x = t ^ r
    lv = np.full((c, c), -1, np.int32)
    for j in range(int(math.log2(c))):
        lv = np.where((x >> j) == 1, j, lv)
    mf = jnp.asarray(np.concatenate(fwd, 0).astype(np.float32), BF16)
    mb = jnp.asarray(np.concatenate(bwd, 0).astype(np.float32), BF16)
    return mf, mb, jnp.asarray(lv, jnp.int32)


def _hgrn_kernel(q_ref, ff_ref, fb_ref, i_ref, g_ref, lb_ref, gain_ref, mf_ref, mb_ref, lv_ref,
                 o_ref, acc_ref, qb_ref, kb_ref, db_ref):
    c = GLA_CHUNK
    n_chunks = q_ref.shape[0] // c
    lb = lb_ref[...]
    l0, l1 = lb[0], lb[1]
    mx = jnp.maximum(l0, l1)
    e0 = jnp.exp(l0 - mx)
    e1 = jnp.exp(l1 - mx)
    p0 = e0 / (e0 + e1)
    lb_f = p0[0:1, :]
    lb_b = p0[1:2, :]
    lv = lv_ref[...]
    row = lax.broadcasted_iota(jnp.int32, (c, LANES), 0)
    nt = (((1,), (1,)), ((), ()))

    def gates(z, lower):
        f = lower + (1.0 - lower) * jax.nn.sigmoid(z)
        return 1.0 - f, jnp.log(f)

    def exponents(m_ref, g):
        g_hi = g.astype(BF16)
        g_lo = (g - g_hi.astype(F32)).astype(BF16)
        m = m_ref[...]
        return (jnp.dot(m, g_hi, preferred_element_type=F32)
                + jnp.dot(m, g_lo, preferred_element_type=F32))

    def fwd_body(ci, st):
        rows = pl.ds(pl.multiple_of(ci * c, c), c)
        qr = q_ref[rows, :]
        q = qr * jax.nn.sigmoid(qr)
        v = i_ref[rows, :]
        kf, gf = gates(ff_ref[rows, :], lb_f)
        kb, gb = gates(fb_ref[rows, :], lb_b)
        ef_all = exponents(mf_ref, gf)
        eb_all = exponents(mb_ref, gb)
        scores = jnp.zeros((c, c), F32)
        for li, m in enumerate(GLA_LEVELS):
            ef = ef_all[(2 + li) * c:(3 + li) * c]
            eb = eb_all[(2 + li) * c:(3 + li) * c]
            second = (row & m) != 0
            a = q * jnp.exp(jnp.where(second, ef, eb))
            b = jnp.where(second, kb, kf) * jnp.exp(jnp.where(second, eb, ef))
            s = lax.dot_general(a.astype(BF16), b.astype(BF16), nt, preferred_element_type=F32)
            scores = jnp.where(lv == int(math.log2(m)), s, scores)
        v_bf = v.astype(BF16)
        o = jnp.dot(scores.astype(BF16), v_bf, preferred_element_type=F32)
        o = o + jnp.sum(q * (kf + kb), axis=-1, keepdims=True) * v
        b_inc = ef_all[0:c]
        q_dec = (q * jnp.exp(b_inc)).astype(BF16)
        o = o + lax.dot_general(q_dec, st.astype(BF16), nt, preferred_element_type=F32)
        k_dec = (kf * jnp.exp(ef_all[c:2 * c])).astype(BF16)
        vt_bf = v.T.astype(BF16)
        st = st * jnp.exp(b_inc[c - 1:c, :]) + jnp.dot(vt_bf, k_dec, preferred_element_type=F32)
        acc_ref[rows, :] = o
        bb = eb_all[0:c]
        qb_ref[rows, :] = (q * jnp.exp(bb)).astype(BF16)
        kb_ref[rows, :] = (kb * jnp.exp(eb_all[c:2 * c])).astype(BF16)
        db_ref[ci] = jnp.broadcast_to(jnp.exp(bb[0:1, :]), (8, LANES))
        return st

    lax.fori_loop(0, n_chunks, fwd_body, jnp.zeros((c, c), F32), unroll=4)

    gain = gain_ref[...]

    def bwd_body(i, st):
        ci = n_chunks - 1 - i
        rows = pl.ds(pl.multiple_of(ci * c, c), c)
        o = acc_ref[rows, :] + lax.dot_general(qb_ref[rows, :], st.astype(BF16), nt,
                                               preferred_element_type=F32)
        vt_bf = i_ref[rows, :].T.astype(BF16)
        st = st * db_ref[ci][0:1, :] + jnp.dot(vt_bf, kb_ref[rows, :], preferred_element_type=F32)
        o = o * lax.rsqrt(jnp.mean(o * o, axis=-1, keepdims=True) + NORM_EPS) * gain
        gr = g_ref[rows, :]
        o_ref[rows, :] = (o * (gr * jax.nn.sigmoid(gr))).astype(o_ref.dtype)
        return st

    lax.fori_loop(0, n_chunks, bwd_body, jnp.zeros((c, c), F32), unroll=4)


def hgrn2_mixer(proj3, lb_logits, norm_gain):
    b, l, _ = proj3.shape
    h, dh = HG_HEADS, HG_HEAD_DIM
    mf, mb, lv = _gla_constants()
    nblk = mf.shape[0]

    def col(off):
        return pl.BlockSpec((None, l, dh), lambda bi, hi, off=off: (bi, 0, off + hi))

    const2 = lambda bi, hi: (0, 0)
    return pl.pallas_call(
        _hgrn_kernel,
        grid=(b, h),
        in_specs=[col(0), col(h), col(2 * h), col(3 * h), col(4 * h),
                  pl.BlockSpec((2, 2, dh), lambda bi, hi: (0, 0, hi)),
                  pl.BlockSpec((1, dh), lambda bi, hi: (0, hi)),
                  pl.BlockSpec((nblk, GLA_CHUNK), const2),
                  pl.BlockSpec((nblk, GLA_CHUNK), const2),
                  pl.BlockSpec((GLA_CHUNK, GLA_CHUNK), const2)],
        out_specs=pl.BlockSpec((None, l, dh), lambda bi, hi: (bi, 0, hi)),
        out_shape=jax.ShapeDtypeStruct((b, l, HG_WIDTH), BF16),
        scratch_shapes=[pltpu.VMEM((l, dh), F32), pltpu.VMEM((l, dh), BF16), pltpu.VMEM((l, dh), BF16),
                        pltpu.VMEM((l // GLA_CHUNK, 8, LANES), F32)],
        compiler_params=_cparams(("arbitrary", "arbitrary")),
        name="hgrn2_mixer",
    )(proj3, proj3, proj3, proj3, proj3, lb_logits, norm_gain.reshape(1, HG_WIDTH), mf, mb, lv)


@functools.lru_cache(maxsize=None)
def _dft_tables(l):
    n = 2 * l
    k2 = 2 * np.arange(l, dtype=np.int64)[:, None] + 1
    m1 = np.arange(l, dtype=np.int64)[None, :]
    ang_s = ((k2 * (2 * m1 + 1)) % (4 * n)).astype(np.float64) * (2.0 * math.pi / (4 * n))
    ang_f = ((k2 * m1) % (2 * n)).astype(np.float64) * (2.0 * math.pi / (2 * n))
    sym = np.stack([np.cos(ang_s), np.sin(ang_s)]).astype(np.float32)
    flt = np.stack([np.cos(ang_f), np.sin(ang_f)]).astype(np.float32)
    return sym, flt


def _filter_features(l):
    pos = jnp.arange(l, dtype=F32)
    t = pos / max(l - 1, 1)
    bands = jnp.linspace(1e-4, HY_BANDS - 1, HY_BANDS, dtype=F32)
    ang = (2.0 * math.pi / l) * pos[:, None] * bands[None, :]
    z = jnp.concatenate([t[:, None], jnp.cos(ang), -jnp.sin(ang)], axis=-1)
    z = jnp.pad(z, ((0, 0), (0, LANES - HY_EMB)))
    min_decay = math.log(HY_DECAY_TARGET) / HY_FAST_DECAY_PCT
    max_decay = math.log(HY_DECAY_TARGET) / HY_SLOW_DECAY_PCT
    deltas = jnp.abs(jnp.linspace(min_decay, max_decay, HY_WIDTH, dtype=F32))
    window = jnp.exp(-t[:, None] * deltas[None, :])
    return z, window


def _filter_kernel(z_ref, win_ref, w1_ref, b1_ref, w2_ref, b2_ref, w3_ref, b3_ref, w4_ref, fr_ref,
                   sum_ref, diff_ref):
    def split(a):
        hi = a.astype(BF16)
        return hi, (a - hi.astype(F32)).astype(BF16)

    def dot3(a, b):
        a_hi, a_lo = split(a)
        b_hi, b_lo = split(b)
        return (jnp.dot(a_hi, b_hi, preferred_element_type=F32) + jnp.dot(a_hi, b_lo, preferred_element_type=F32)
                + jnp.dot(a_lo, b_hi, preferred_element_type=F32))

    fr = fr_ref[...]
    h = jnp.sin(fr * (dot3(z_ref[...], w1_ref[...]) + b1_ref[...]))
    h = jnp.sin(fr * (dot3(h, w2_ref[...]) + b2_ref[...]))
    h = jnp.sin(fr * (dot3(h, w3_ref[...]) + b3_ref[...]))
    h = dot3(h, w4_ref[...])
    win = win_ref[...]
    tl = h.shape[0]
    lag = pl.program_id(0) * tl + lax.broadcasted_iota(jnp.int32, (tl, HY_WIDTH), 0)
    for o in range(HY_ORDER):
        hf = h[:, (2 * o) * HY_WIDTH:(2 * o + 1) * HY_WIDTH] * win
        hb = h[:, (2 * o + 1) * HY_WIDTH:(2 * o + 2) * HY_WIDTH] * win
        hb = jnp.where(lag == 0, 0.0, hb)
        sum_ref[:, o * HY_WIDTH:(o + 1) * HY_WIDTH] = (hf + hb).astype(sum_ref.dtype)
        diff_ref[:, o * HY_WIDTH:(o + 1) * HY_WIDTH] = (hf - hb).astype(diff_ref.dtype)


def hyena_filter_taps(l, w1, b1, w2, b2, w3, b3, w4, freq, tl=256):
    z, window = _filter_features(l)
    w1p = jnp.pad(w1, ((0, LANES - HY_EMB), (0, 0)))
    hid = HY_FILTER_HIDDEN
    full = lambda shape: pl.BlockSpec(shape, lambda i: (0,) * len(shape))
    nout = HY_ORDER * HY_WIDTH
    return pl.pallas_call(
        _filter_kernel,
        grid=(l // tl,),
        in_specs=[pl.BlockSpec((tl, LANES), lambda i: (i, 0)),
                  pl.BlockSpec((tl, HY_WIDTH), lambda i: (i, 0)),
                  full((LANES, hid)), full((1, hid)), full((hid, hid)), full((1, hid)),
                  full((hid, hid)), full((1, hid)), full((hid, 2 * nout)), full((1, hid))],
        out_specs=[pl.BlockSpec((tl, nout), lambda i: (i, 0)), pl.BlockSpec((tl, nout), lambda i: (i, 0))],
        out_shape=[jax.ShapeDtypeStruct((l, nout), BF16), jax.ShapeDtypeStruct((l, nout), BF16)],
        compiler_params=_cparams(("arbitrary",)),
        name="hyena_filter_taps",
    )(z, window, w1p, b1.reshape(1, hid), w2, b2.reshape(1, hid), w3, b3.reshape(1, hid), w4,
      freq.reshape(1, hid))


def _spectrum_kernel(f_ref, h_ref, o_ref):
    o_ref[...] = jnp.dot(f_ref[...], h_ref[...], preferred_element_type=F32).astype(o_ref.dtype)


def hyena_filter_spectra(flt, taps, tn=512):
    _, l, nout = taps.shape
    return pl.pallas_call(
        _spectrum_kernel,
        grid=(2, nout // tn),
        in_specs=[pl.BlockSpec((None, l, l), lambda p, j: (p, 0, 0)),
                  pl.BlockSpec((None, l, tn), lambda p, j: (p, 0, j))],
        out_specs=pl.BlockSpec((None, l, tn), lambda p, j: (p, 0, j)),
        out_shape=jax.ShapeDtypeStruct((2, l, nout), BF16),
        compiler_params=_cparams(("arbitrary", "arbitrary")),
        name="hyena_filter_spectra",
    )(flt, taps)


def _hyena_kernel(xv_ref, x1_ref, x2_ref, cw_ref, cb_ref, sym_ref, sp0_ref, sp1_ref, bias_ref, o_ref,
                  z_s, g_s, zb_s, pa_s, pb_s, *, rb):
    l, tc = xv_ref.shape
    scale = 2.0 / (2 * l)
    first = lax.broadcasted_iota(jnp.int32, (l, tc), 0) == 0
    last = lax.broadcasted_iota(jnp.int32, (l, tc), 0) == l - 1

    def short_conv(x_ref, part):
        x = x_ref[...]
        w = cw_ref[part]
        prev = jnp.where(first, 0.0, pltpu.roll(x, 1, 0))
        nxt = jnp.where(last, 0.0, pltpu.roll(x, l - 1, 0))
        return prev * w[0:1, :] + x * w[1:2, :] + nxt * w[2:3, :] + cb_ref[part]

    z_s[...] = short_conv(xv_ref, 0)
    for o, (gate_ref, part, sp_ref) in enumerate(((x1_ref, 1, sp0_ref), (x2_ref, 2, sp1_ref))):
        zb_s[...] = z_s[...].astype(BF16)
        g_s[...] = short_conv(gate_ref, part)
        for r0 in range(0, l, rb):
            a = jnp.dot(sym_ref[0, r0:r0 + rb, :], zb_s[...], preferred_element_type=F32)
            b = jnp.dot(sym_ref[1, r0:r0 + rb, :], zb_s[...], preferred_element_type=F32)
            hr = sp_ref[0, r0:r0 + rb, :].astype(F32)
            hs = sp_ref[1, r0:r0 + rb, :].astype(F32)
            pa_s[r0:r0 + rb, :] = (a * hr - b * hs).astype(BF16)
            pb_s[r0:r0 + rb, :] = (a * hs + b * hr).astype(BF16)
        for r0 in range(0, l, rb):
            y = (jnp.dot(sym_ref[0, r0:r0 + rb, :], pa_s[...], preferred_element_type=F32)
                 + jnp.dot(sym_ref[1, r0:r0 + rb, :], pb_s[...], preferred_element_type=F32)) * scale
            y = y + z_s[r0:r0 + rb, :] * bias_ref[o]
            z_s[r0:r0 + rb, :] = g_s[r0:r0 + rb, :] * y
    o_ref[...] = z_s[...].astype(o_ref.dtype)


def hyena_mixer(proj3, conv_w, conv_b, sym, spectra, hy_bias, tc=256, rb=1024):
    b, l, _ = proj3.shape
    nct = HY_WIDTH // tc
    base = (5 * HG_WIDTH) // tc

    def col(part):
        return pl.BlockSpec((None, l, tc), lambda ci, bi, part=part: (bi, 0, base + part * nct + ci))

    def spec_cols(o):
        return pl.BlockSpec((2, l, tc), lambda ci, bi, o=o: (0, 0, o * nct + ci))

    cw = conv_w.reshape(3, HY_ORDER + 1, HY_WIDTH).transpose(1, 0, 2)
    cb = conv_b.reshape(HY_ORDER + 1, 1, HY_WIDTH)
    return pl.pallas_call(
        functools.partial(_hyena_kernel, rb=rb),
        grid=(nct, b),
        in_specs=[col(0), col(1), col(2),
                  pl.BlockSpec((HY_ORDER + 1, 3, tc), lambda ci, bi: (0, 0, ci)),
                  pl.BlockSpec((HY_ORDER + 1, 1, tc), lambda ci, bi: (0, 0, ci)),
                  pl.BlockSpec((2, l, l), lambda ci, bi: (0, 0, 0), pipeline_mode=pl.Buffered(1)),
                  spec_cols(0), spec_cols(1),
                  pl.BlockSpec((HY_ORDER, 1, tc), lambda ci, bi: (0, 0, ci))],
        out_specs=pl.BlockSpec((None, l, tc), lambda ci, bi: (bi, 0, ci)),
        out_shape=jax.ShapeDtypeStruct((b, l, HY_WIDTH), BF16),
        scratch_shapes=[pltpu.VMEM((l, tc), F32), pltpu.VMEM((l, tc), F32), pltpu.VMEM((l, tc), BF16),
                        pltpu.VMEM((l, tc), BF16), pltpu.VMEM((l, tc), BF16)],
        compiler_params=_cparams(("arbitrary", "arbitrary")),
        name="hyena_mixer",
    )(proj3, proj3, proj3, cw, cb, sym, spectra, spectra, hy_bias.reshape(HY_ORDER, 1, HY_WIDTH))


def _outproj_router_kernel(hg_ref, hy_ref, x_ref, w_ref, hyg_ref, fg_ref, wr_ref, br_ref, tri_ref,
                           x2_ref, xn_ref, meta_ref, metat_ref, cnt_ref, carry_ref):
    @pl.when(pl.program_id(0) == 0)
    def _():
        carry_ref[...] = jnp.zeros_like(carry_ref)

    hy = hy_ref[...].astype(F32)
    hy = hy * lax.rsqrt(jnp.mean(hy * hy, axis=-1, keepdims=True) + NORM_EPS) * hyg_ref[...]
    m = (jnp.dot(hg_ref[...], w_ref[0:HG_WIDTH, :], preferred_element_type=F32)
         + jnp.dot(hy.astype(BF16), w_ref[HG_WIDTH:, :], preferred_element_type=F32))
    x2 = x_ref[...] + m
    x2_ref[...] = x2
    xn = x2 * lax.rsqrt(jnp.mean(x2 * x2, axis=-1, keepdims=True) + NORM_EPS) * fg_ref[...]
    xn_ref[...] = xn
    xn_hi = xn.astype(BF16)
    xn_lo = (xn - xn_hi.astype(F32)).astype(BF16)
    wr = wr_ref[...]
    wr_hi = wr.astype(BF16)
    wr_lo = (wr - wr_hi.astype(F32)).astype(BF16)
    logits = (jnp.dot(xn_hi, wr_hi, preferred_element_type=F32)
              + jnp.dot(xn_hi, wr_lo, preferred_element_type=F32)
              + jnp.dot(xn_lo, wr_hi, preferred_element_type=F32)) + br_ref[...]
    tm = logits.shape[0]
    lane = lax.broadcasted_iota(jnp.int32, (tm, LANES), 1).astype(F32)
    neg = jnp.float32(-jnp.inf)
    logits = jnp.where(lane < N_EXPERTS, logits, neg)
    tops, idxs = [], []
    for _ in range(TOP_K):
        mx = jnp.max(logits, axis=-1, keepdims=True)
        idx = jnp.min(jnp.where(logits == mx, lane, float(LANES)), axis=-1, keepdims=True)
        tops.append(mx)
        idxs.append(idx)
        logits = jnp.where(lane == idx, neg, logits)
    exps = [jnp.exp(t - tops[0]) for t in tops]
    denom = exps[0] + exps[1] + exps[2] + exps[3]
    onehot = jnp.zeros((tm, LANES), F32)
    for idx in idxs:
        onehot = onehot + jnp.where(lane == idx, 1.0, 0.0)
    cum = jnp.dot(tri_ref[...], onehot.astype(BF16), preferred_element_type=F32) + carry_ref[0:1, :]
    meta = jnp.zeros((tm, LANES), F32)
    for k in range(TOP_K):
        rank = jnp.sum(jnp.where(lane == idxs[k], cum, 0.0), axis=-1, keepdims=True)
        meta = jnp.where(lane == k, idxs[k], meta)
        meta = jnp.where(lane == TOP_K + k, rank, meta)
        meta = jnp.where(lane == 2 * TOP_K + k, exps[k] / denom, meta)
    meta_ref[...] = meta
    metat_ref[...] = meta.T[0:metat_ref.shape[0], :]
    carry = carry_ref[...] + jnp.sum(onehot, axis=0, keepdims=True)
    carry_ref[...] = carry
    cnt_ref[...] = carry


def outproj_router(hg2d, hy2d, x2d, w_out_bf16, hy_gain, ffn_gain, w_router, b_router, tm=512):
    n, d = x2d.shape
    wr = jnp.pad(w_router, ((0, 0), (0, LANES - N_EXPERTS)))
    br = jnp.pad(b_router, (0, LANES - N_EXPERTS)).reshape(1, LANES)
    tri = jnp.asarray(np.tril(np.ones((tm, tm), np.float32), -1), BF16)
    row = lambda w: pl.BlockSpec((tm, w), lambda i: (i, 0))
    full = lambda shape, **kw: pl.BlockSpec(shape, lambda i: (0,) * len(shape), **kw)
    return pl.pallas_call(
        _outproj_router_kernel,
        grid=(n // tm,),
        in_specs=[row(HG_WIDTH), row(HY_WIDTH), row(d),
                  full((d, d), pipeline_mode=pl.Buffered(1)),
                  full((1, HY_WIDTH)), full((1, d)), full((d, LANES)), full((1, LANES)), full((tm, tm))],
        out_specs=[row(d), row(d), row(LANES), pl.BlockSpec((2 * TOP_K, tm), lambda i: (0, i)),
                   full((8, LANES))],
        out_shape=[jax.ShapeDtypeStruct((n, d), F32), jax.ShapeDtypeStruct((n, d), F32),
                   jax.ShapeDtypeStruct((n, LANES), F32), jax.ShapeDtypeStruct((2 * TOP_K, n), F32),
                   jax.ShapeDtypeStruct((8, LANES), F32)],
        scratch_shapes=[pltpu.VMEM((8, LANES), F32)],
        compiler_params=_cparams(("arbitrary",)),
        name="outproj_router",
    )(hg2d, hy2d, x2d, w_out_bf16, hy_gain.reshape(1, HY_WIDTH), ffn_gain.reshape(1, d), wr, br, tri)


GATHER_UNROLL = 8


def _row_gather_kernel(dest_ref, rows_ref, used_ref, x_hbm, o_ref, slot_s, buf_ref, sems, *, n_tok):
    t = pl.program_id(0)
    tr = buf_ref.shape[1]
    used = used_ref[0]

    def n_copies(tile):
        return ((rows_ref[tile] + GATHER_UNROLL - 1) // GATHER_UNROLL) * GATHER_UNROLL

    def source_row(tile, r):
        return slot_s[tile * tr + jnp.minimum(r, rows_ref[tile] - 1)]

    def issue(tile):
        nc = n_copies(tile)
        for b in range(2):
            @pl.when(tile % 2 == b)
            def _(b=b):
                for g in range(tr // GATHER_UNROLL):
                    @pl.when(g * GATHER_UNROLL < nc)
                    def _(g=g):
                        for u in range(GATHER_UNROLL):
                            r = g * GATHER_UNROLL + u
                            pltpu.make_async_copy(x_hbm.at[pl.ds(source_row(tile, r), 1)],
                                                  buf_ref.at[b, pl.ds(r, 1)], sems.at[b]).start()

    @pl.when(t == 0)
    def _():
        def scatter(a, carry):
            tok = (a & (n_tok - 1)) if n_tok & (n_tok - 1) == 0 else lax.rem(a, jnp.int32(n_tok))
            slot_s[dest_ref[a]] = tok
            return carry

        lax.fori_loop(0, dest_ref.shape[0], scatter, 0, unroll=16)
        buf_ref[...] = jnp.zeros(buf_ref.shape, buf_ref.dtype)

        def body(r, carry):
            pltpu.make_async_copy(x_hbm.at[pl.ds(source_row(0, r), 1)], buf_ref.at[0, pl.ds(r, 1)],
                                  sems.at[0]).start()
            return carry

        lax.fori_loop(0, n_copies(0), body, 0)

    @pl.when(t + 1 < used)
    def _():
        issue(t + 1)

    @pl.when(t < used)
    def _():
        b = t % 2
        nc = n_copies(t)

        @pl.when(nc > 0)
        def _():
            pltpu.make_async_copy(x_hbm.at[pl.ds(0, nc)], buf_ref.at[b, pl.ds(0, nc)], sems.at[b]).wait()

        row = lax.broadcasted_iota(jnp.int32, o_ref.shape, 0)
        o_ref[...] = jnp.where(row < rows_ref[t], buf_ref[b], 0.0).astype(o_ref.dtype)


def gather_rows(dest_flat, tile_rows, used_tiles, xn, n_slots, tr):
    n, d = xn.shape
    grid_spec = pltpu.PrefetchScalarGridSpec(
        num_scalar_prefetch=3,
        grid=(n_slots // tr,),
        in_specs=[pl.BlockSpec(memory_space=pl.ANY)],
        out_specs=pl.BlockSpec((tr, d), lambda t, dest, rows, used: (jnp.minimum(t, used[0] - 1), 0)),
        scratch_shapes=[pltpu.SMEM((n_slots,), jnp.int32), pltpu.VMEM((2, tr, d), F32),
                        pltpu.SemaphoreType.DMA((2,))],
    )
    return pl.pallas_call(
        functools.partial(_row_gather_kernel, n_tok=n),
        grid_spec=grid_spec,
        out_shape=jax.ShapeDtypeStruct((n_slots, d), BF16),
        compiler_params=pltpu.CompilerParams(dimension_semantics=("arbitrary",), vmem_limit_bytes=VMEM_LIMIT,
                                             disable_bounds_checks=True),
        name="moe_gather_rows",
    )(dest_flat, tile_rows, used_tiles, xn)


def _expert_kernel(dest_ref, te1_ref, tv1_ref, te2_ref, tv2_ref, tb2_ref,
                   xn_hbm, wg_ref, wl_ref, bg_ref, bl_ref, w2_ref, b2_ref, o_ref,
                   slot_s, x_s, stage_s, act_s, sem, *, nf, blocks, n_tok):
    jj = pl.program_id(0)
    s = pl.program_id(1)
    tf = wg_ref.shape[1]
    tm = x_s.shape[1]
    part = stage_s.shape[0]
    valid_in = tv1_ref[jj]
    valid_out = tv2_ref[jj]
    buf_in = jj % 2
    buf_out = 1 - buf_in

    def source_row(tile, r):
        return slot_s[tile * tm + jnp.minimum(r, tv1_ref[tile] - 1)]

    def convert_part(buf, c):
        pltpu.make_async_copy(xn_hbm.at[pl.ds(0, part)], stage_s, sem).wait()
        x_s[buf, pl.ds(pl.multiple_of(c * part, 16), part), :] = stage_s[...].astype(BF16)

    @pl.when(jnp.logical_and(jj == 0, s == 0))
    def _():
        def scatter(a, carry):
            tok = (a & (n_tok - 1)) if n_tok & (n_tok - 1) == 0 else lax.rem(a, jnp.int32(n_tok))
            slot_s[dest_ref[a]] = tok
            return carry

        lax.fori_loop(0, dest_ref.shape[0], scatter, 0, unroll=16)

        for c in range(nf):
            def body(r, carry, c=c):
                pltpu.make_async_copy(xn_hbm.at[pl.ds(source_row(0, c * part + r), 1)],
                                      stage_s.at[pl.ds(r, 1)], sem).start()
                return carry

            lax.fori_loop(0, part, body, 0)
            convert_part(0, c)

    @pl.when(valid_in > 0)
    def _():
        for r0, rn in blocks:
            @pl.when(r0 < valid_in)
            def _(r0=r0, rn=rn):
                if r0 == 0:
                    nxt = jnp.where(tv1_ref[jj + 1] > 0, jj + 1, jj)
                    for r in range(part):
                        pltpu.make_async_copy(xn_hbm.at[pl.ds(source_row(nxt, s * part + r), 1)],
                                              stage_s.at[pl.ds(r, 1)], sem).start()
                x = x_s[buf_in, r0:r0 + rn, :]
                glu = jnp.dot(x, wg_ref[...].astype(BF16), preferred_element_type=F32) + bg_ref[...]
                lin = jnp.dot(x, wl_ref[...].astype(BF16), preferred_element_type=F32) + bl_ref[...]
                glu = jnp.minimum(glu, SWIGLU_LIMIT)
                lin = jnp.clip(lin, -SWIGLU_LIMIT, SWIGLU_LIMIT)
                act = glu * jax.nn.sigmoid(SWIGLU_ALPHA * glu) * (lin + 1.0)
                act_s[buf_in, s, r0:r0 + rn, :] = act.astype(BF16)

    @pl.when(valid_out > 0)
    def _():
        for r0, rn in blocks:
            @pl.when(r0 < valid_out)
            def _(r0=r0, rn=rn):
                y = b2_ref[...] + jnp.dot(act_s[buf_out, 0, r0:r0 + rn, :], w2_ref[0:tf, :].astype(BF16),
                                          preferred_element_type=F32)
                for f in range(1, nf):
                    y = y + jnp.dot(act_s[buf_out, f, r0:r0 + rn, :],
                                    w2_ref[f * tf:(f + 1) * tf, :].astype(BF16),
                                    preferred_element_type=F32)
                o_ref[r0:r0 + rn, :] = y

            @pl.when(r0 >= valid_out)
            def _(r0=r0, rn=rn):
                o_ref[r0:r0 + rn, :] = jnp.zeros((rn, o_ref.shape[1]), o_ref.dtype)

    @pl.when(valid_in > 0)
    def _():
        convert_part(buf_out, s)


def expert_ffn(dest_flat, tile_expert, tile_valid, tile_block, xn, n_slots, w1, b1, w2, b2, tm, blocks, tf=512):
    n_tok, d = xn.shape
    n_tiles = n_slots // tm
    nf = D_FF // tf
    assert d // tf == nf and tm % (16 * nf) == 0
    b1r = b1.reshape(N_EXPERTS, 1, 2 * D_FF)
    b2r = b2.reshape(N_EXPERTS, 1, d)
    zero = jnp.zeros((1,), jnp.int32)
    te1 = jnp.concatenate([tile_expert, tile_expert[-1:]])
    tv1 = jnp.concatenate([tile_valid, zero, zero])
    te2 = jnp.concatenate([tile_expert[:1], tile_expert])
    tv2 = jnp.concatenate([zero, tile_valid])
    tb2 = jnp.concatenate([tile_block[:1], tile_block])

    def blk1(jj, s, tv1):
        return jnp.where(tv1[jj] > 0, s, nf - 1)

    def blk2(jj, s, tv2):
        return jnp.where(tv2[jj] > 0, s, jnp.where(jj == 0, 0, nf - 1))

    grid_spec = pltpu.PrefetchScalarGridSpec(
        num_scalar_prefetch=6,
        grid=(n_tiles + 1, nf),
        in_specs=[
            pl.BlockSpec(memory_space=pl.ANY),
            pl.BlockSpec((None, d, tf),
                         lambda jj, s, dest, te1, tv1, te2, tv2, tb2: (te1[jj], 0, blk1(jj, s, tv1))),
            pl.BlockSpec((None, d, tf),
                         lambda jj, s, dest, te1, tv1, te2, tv2, tb2: (te1[jj], 0, nf + blk1(jj, s, tv1))),
            pl.BlockSpec((None, 1, tf),
                         lambda jj, s, dest, te1, tv1, te2, tv2, tb2: (te1[jj], 0, blk1(jj, s, tv1))),
            pl.BlockSpec((None, 1, tf),
                         lambda jj, s, dest, te1, tv1, te2, tv2, tb2: (te1[jj], 0, nf + blk1(jj, s, tv1))),
            pl.BlockSpec((None, D_FF, tf),
                         lambda jj, s, dest, te1, tv1, te2, tv2, tb2: (te2[jj], 0, blk2(jj, s, tv2))),
            pl.BlockSpec((None, 1, tf),
                         lambda jj, s, dest, te1, tv1, te2, tv2, tb2: (te2[jj], 0, blk2(jj, s, tv2))),
        ],
        out_specs=pl.BlockSpec((tm, tf),
                               lambda jj, s, dest, te1, tv1, te2, tv2, tb2: (tb2[jj], blk2(jj, s, tv2))),
        scratch_shapes=[pltpu.SMEM((n_slots,), jnp.int32), pltpu.VMEM((2, tm, d), BF16),
                        pltpu.VMEM((tm // nf, d), F32), pltpu.VMEM((2, nf, tm, tf), BF16),
                        pltpu.SemaphoreType.DMA(())],
    )
    return pl.pallas_call(
        functools.partial(_expert_kernel, nf=nf, blocks=blocks, n_tok=n_tok),
        grid_spec=grid_spec,
        out_shape=jax.ShapeDtypeStruct((n_slots, d), F32),
        compiler_params=pltpu.CompilerParams(dimension_semantics=("arbitrary", "arbitrary"),
                                             vmem_limit_bytes=VMEM_LIMIT, disable_bounds_checks=True),
        name="moe_expert_ffn",
    )(dest_flat, te1, tv1, te2, tv2, tb2, xn, w1, w1, b1r, b1r, w2, b2r)


def _combine_kernel(dest_ref, ys_hbm, x2_ref, meta_ref, gain_ref, o_ref, buf_ref, sems):
    t = pl.program_id(0)
    nt = pl.num_programs(0)
    tt = x2_ref.shape[0]
    n_tok = dest_ref.shape[0] // TOP_K

    def issue_rolled(tile, b):
        base = tile * tt

        def body(r, carry):
            for k in range(TOP_K):
                pltpu.make_async_copy(ys_hbm.at[pl.ds(dest_ref[k * n_tok + base + r], 1)],
                                      buf_ref.at[b, k, pl.ds(r, 1)], sems.at[b]).start()
            return carry

        lax.fori_loop(0, tt, body, 0, unroll=4)

    def issue(tile):
        base = tile * tt
        for b in range(2):
            @pl.when(tile % 2 == b)
            def _(b=b):
                for r in range(tt):
                    for k in range(TOP_K):
                        pltpu.make_async_copy(ys_hbm.at[pl.ds(dest_ref[k * n_tok + base + r], 1)],
                                              buf_ref.at[b, k, pl.ds(r, 1)], sems.at[b]).start()

    @pl.when(t == 0)
    def _():
        issue_rolled(t, 0)

    @pl.when(t + 1 < nt)
    def _():
        issue(t + 1)

    b = t % 2
    for k in range(TOP_K):
        pltpu.make_async_copy(ys_hbm.at[pl.ds(0, tt)], buf_ref.at[b, k], sems.at[b]).wait()
    meta = meta_ref[...]
    y = x2_ref[...]
    for k in range(TOP_K):
        y = y + meta[:, 2 * TOP_K + k:2 * TOP_K + k + 1] * buf_ref[b, k]
    o_ref[...] = y * lax.rsqrt(jnp.mean(y * y, axis=-1, keepdims=True) + NORM_EPS) * gain_ref[...]


def combine_final(dest_flat, ys, x2, meta, final_gain, tt=128):
    n, d = x2.shape
    grid_spec = pltpu.PrefetchScalarGridSpec(
        num_scalar_prefetch=1,
        grid=(n // tt,),
        in_specs=[pl.BlockSpec(memory_space=pl.ANY),
                  pl.BlockSpec((tt, d), lambda t, dest: (t, 0)),
                  pl.BlockSpec((tt, LANES), lambda t, dest: (t, 0)),
                  pl.BlockSpec((1, d), lambda t, dest: (0, 0))],
        out_specs=pl.BlockSpec((tt, d), lambda t, dest: (t, 0)),
        scratch_shapes=[pltpu.VMEM((2, TOP_K, tt, d), F32), pltpu.SemaphoreType.DMA((2,))],
    )
    return pl.pallas_call(
        _combine_kernel,
        grid_spec=grid_spec,
        out_shape=jax.ShapeDtypeStruct((n, d), F32),
        compiler_params=pltpu.CompilerParams(dimension_semantics=("arbitrary",), vmem_limit_bytes=VMEM_LIMIT,
                                             disable_bounds_checks=True),
        name="moe_combine_final",
    )(dest_flat, ys, x2, meta, final_gain.reshape(1, d))


MOE_TILE = 1152
MOE_ROW_BLOCKS = ((0, 512), (512, 512), (1024, 128))
MOE_GATHER_TILE = 384


def moe_ffn_final(x2, xn, meta, meta_t, counts, w1, b1, w2, b2, final_gain, tm=MOE_TILE):
    n, d = x2.shape
    a = n * TOP_K
    n_tiles = -(-a // tm) + N_EXPERTS
    n_slots = n_tiles * tm
    idx = meta_t[0:TOP_K].astype(jnp.int32)
    rank = meta_t[TOP_K:2 * TOP_K].astype(jnp.int32)
    cnt = counts[0, :N_EXPERTS].astype(jnp.int32)
    tiles_e = (cnt + tm - 1) // tm
    tile_end = jnp.cumsum(tiles_e)
    tile_start = tile_end - tiles_e
    used = tile_end[-1]
    experts = jnp.arange(N_EXPERTS, dtype=jnp.int32)[:, None, None]
    first_slot = jnp.sum(jnp.where(idx[None] == experts, (tile_start * tm)[:, None, None], 0), axis=0)
    dest = first_slot + rank
    dest_flat = dest.reshape(a)
    tj = jnp.arange(n_tiles, dtype=jnp.int32)
    te = jnp.minimum(jnp.sum(tj[:, None] >= tile_end[None, :], axis=1), N_EXPERTS - 1).astype(jnp.int32)
    last_e = te[jnp.maximum(used - 1, 0)]
    tile_expert = jnp.where(tj < used, te, last_e).astype(jnp.int32)
    tile_valid = jnp.where(tj < used, jnp.clip(cnt[te] - (tj - tile_start[te]) * tm, 0, tm), 0).astype(jnp.int32)
    tile_block = jnp.maximum(jnp.minimum(tj, used - 1), 0).astype(jnp.int32)

    ys = expert_ffn(dest_flat, tile_expert, tile_valid, tile_block, xn, n_slots, w1, b1, w2, b2, tm,
                    MOE_ROW_BLOCKS)
    return combine_final(dest_flat, ys, x2, meta, final_gain)


def kernel(x, norm_mix_gain, w_in, hgrn_lb_logits, hgrn_norm_gain, hy_conv_w, hy_conv_b, hy_filt_w1,
           hy_filt_b1, hy_filt_w2, hy_filt_b2, hy_filt_w3, hy_filt_b3, hy_filt_w4, hy_filt_freq, hy_bias,
           hy_norm_gain, w_out, norm_ffn_gain, w_router, b_router, moe_w1, moe_b1, moe_w2, moe_b2,
           final_norm_gain):
    b, l, d = x.shape
    n = b * l
    x2d = x.reshape(n, d)
    proj = in_projection(x2d, norm_mix_gain[0], w_in[0].astype(BF16))
    proj3 = proj.reshape(b, l, D_IN)
    hg = hgrn2_mixer(proj3, hgrn_lb_logits, hgrn_norm_gain[0])

    sym, flt = (jnp.asarray(tab).astype(BF16) for tab in _dft_tables(l))
    tap_sum, tap_diff = hyena_filter_taps(l, hy_filt_w1[0], hy_filt_b1[0], hy_filt_w2[0], hy_filt_b2[0],
                                          hy_filt_w3[0], hy_filt_b3[0], hy_filt_w4[0], hy_filt_freq[0])
    spectra = hyena_filter_spectra(flt, jnp.stack([tap_sum, tap_diff]))
    hy = hyena_mixer(proj3, hy_conv_w[0], hy_conv_b[0], sym, spectra, hy_bias[0])

    x2, xn, meta, meta_t, counts = outproj_router(hg.reshape(n, HG_WIDTH), hy.reshape(n, HY_WIDTH), x2d,
                                                  w_out[0].astype(BF16), hy_norm_gain[0], norm_ffn_gain[0],
                                                  w_router[0], b_router[0])
    out = moe_ffn_final(x2, xn, meta, meta_t, counts, moe_w1[0], moe_b1[0], moe_w2[0], moe_b2[0],
                        final_norm_gain)
    return out.reshape(b, l, d)
```

```python
import functools
import math

import numpy as np
import jax
import jax.numpy as jnp
from jax import lax
from jax.experimental import pallas as pl
from jax.experimental.pallas import tpu as pltpu

F32 = jnp.float32
BF16 = jnp.bfloat16

D_MODEL = 2048
HG_WIDTH = 1024
HG_HEAD_DIM = 128
HG_HEADS = HG_WIDTH // HG_HEAD_DIM
HY_WIDTH = 1024
HY_ORDER = 2
HY_EMB = 33
HY_BANDS = (HY_EMB - 1) // 2
HY_FILTER_HIDDEN = 64
HY_FAST_DECAY_PCT = 0.3
HY_SLOW_DECAY_PCT = 1.5
HY_DECAY_TARGET = 1e-2
N_EXPERTS = 32
TOP_K = 4
D_FF = D_MODEL
SWIGLU_ALPHA = 1.702
SWIGLU_LIMIT = 7.0
NORM_EPS = 1e-5
D_IN = 5 * HG_WIDTH + (HY_ORDER + 1) * HY_WIDTH

LANES = 128
GLA_CHUNK = 128
GLA_LEVELS = (64, 32, 16, 8, 4, 2, 1)
GLA_MATRIX_LEVELS = (4, 2)
VMEM_LIMIT = 56 * 1024 * 1024


def _cparams(sem, vmem=VMEM_LIMIT):
    return pltpu.CompilerParams(dimension_semantics=sem, vmem_limit_bytes=vmem)


def _inproj_kernel(x_ref, g_ref, w_ref, o_ref, h_s):
    @pl.when(pl.program_id(1) == 0)
    def _():
        x = x_ref[...]
        ms = jnp.mean(x * x, axis=-1, keepdims=True)
        h_s[...] = (x * lax.rsqrt(ms + NORM_EPS) * g_ref[...]).astype(BF16)

    o_ref[...] = jnp.dot(h_s[...], w_ref[...], preferred_element_type=F32)


def in_projection(x2d, gain, w_bf16, tm=1024, tn=1024):
    n, d = x2d.shape
    dout = w_bf16.shape[1]
    return pl.pallas_call(
        _inproj_kernel,
        grid=(n // tm, dout // tn),
        in_specs=[
            pl.BlockSpec((tm, d), lambda i, j: (i, 0)),
            pl.BlockSpec((1, d), lambda i, j: (0, 0)),
            pl.BlockSpec((d, tn), lambda i, j: (0, j)),
        ],
        out_specs=pl.BlockSpec((tm, tn), lambda i, j: (i, j)),
        out_shape=jax.ShapeDtypeStruct((n, dout), F32),
        scratch_shapes=[pltpu.VMEM((tm, d), BF16)],
        compiler_params=_cparams(("arbitrary", "arbitrary")),
        name="in_projection",
    )(x2d, gain.reshape(1, d), w_bf16)


def _gla_constants():
    c = GLA_CHUNK
    t = np.arange(c)[:, None]
    r = np.arange(c)[None, :]
    fwd = [r <= t]
    bwd = [r >= t]
    for m in GLA_MATRIX_LEVELS:
        pos = t % (2 * m)
        mid = t - pos + m
        second = pos >= m
        fwd.append(np.where(second, (r >= mid) & (r <= t), (r > t) & (r < mid)))
        bwd.append(np.where(second, (r >= mid) & (r < t), (r >= t) & (r < mid)))
    x = t ^ r
    lv = np.full((c, c), -1, np.int32)
    for j in range(int(math.log2(c))):
        lv = np.where((x >> j) == 1, j, lv)
    mf = jnp.asarray(np.concatenate(fwd, 0).astype(np.float32), BF16)
    mb = jnp.asarray(np.concatenate(bwd, 0).astype(np.float32), BF16)
    return mf, mb, jnp.asarray(lv, jnp.int32)


def _hgrn_kernel(q_ref, ff_ref, fb_ref, i_ref, g_ref, lb_ref, gain_ref, mf_ref, mb_ref, lv_ref,
                 o_ref, acc_ref, qb_ref, kb_ref, db_ref):
    c = GLA_CHUNK
    n_chunks = q_ref.shape[0] // c
    lb = lb_ref[...]
    l0, l1 = lb[0], lb[1]
    mx = jnp.maximum(l0, l1)
    e0 = jnp.exp(l0 - mx)
    e1 = jnp.exp(l1 - mx)
    p0 = e0 / (e0 + e1)
    lb_f = p0[0:1, :]
    lb_b = p0[1:2, :]
    lv = lv_ref[...]
    row = lax.broadcasted_iota(jnp.int32, (c, LANES), 0)
    nt = (((1,), (1,)), ((), ()))

    def gates(z, lower):
        f = lower + (1.0 - lower) * jax.nn.sigmoid(z)
        return f, 1.0 - f, jnp.log(f)

    def boundary_diff(cum, m, offset):
        nb = c // (2 * m)
        c3 = cum.reshape(nb, 2 * m, LANES)
        ref = jnp.broadcast_to(c3[:, offset:offset + 1, :], (nb, 2 * m, LANES))
        return (c3 - ref).reshape(c, LANES)

    def exponents(m_ref, g):
        g_hi = g.astype(BF16)
        g_lo = (g - g_hi.astype(F32)).astype(BF16)
        m = m_ref[...]
        return (jnp.dot(m, g_hi, preferred_element_type=F32)
                + jnp.dot(m, g_lo, preferred_element_type=F32))

    def fwd_body(ci, st):
        rows = pl.ds(pl.multiple_of(ci * c, c), c)
        qr = q_ref[rows, :]
        q = qr * jax.nn.sigmoid(qr)
        v = i_ref[rows, :]
        ff, kf, gf = gates(ff_ref[rows, :], lb_f)
        fb, kb, gb = gates(fb_ref[rows, :], lb_b)
        ef_all = exponents(mf_ref, gf)
        eb_all = exponents(mb_ref, gb)
        b_inc = ef_all[0:c]
        bb = eb_all[0:c]
        scores = jnp.zeros((c, c), F32)
        for m in GLA_LEVELS:
            second = (row & m) != 0
            k_sel = jnp.where(second, kb, kf)
            if m == 1:
                a = q * jnp.where(second, ff, fb)
                b = k_sel
            else:
                if m in GLA_MATRIX_LEVELS:
                    li = 1 + GLA_MATRIX_LEVELS.index(m)
                    ef = ef_all[li * c:(li + 1) * c]
                    eb = eb_all[li * c:(li + 1) * c]
                    e_query = jnp.where(second, ef, eb)
                    e_key = jnp.where(second, eb, ef)
                else:
                    df = boundary_diff(b_inc, m, m - 1)
                    db = boundary_diff(bb, m, m)
                    e_query = jnp.where(second, df, db)
                    e_key = -jnp.where(second, db, df)
                a = q * jnp.exp(e_query)
                b = k_sel * jnp.exp(e_key)
            s = lax.dot_general(a.astype(BF16), b.astype(BF16), nt, preferred_element_type=F32)
            scores = jnp.where(lv == int(math.log2(m)), s, scores)
        v_bf = v.astype(BF16)
        o = jnp.dot(scores.astype(BF16), v_bf, preferred_element_type=F32)
        o = o + jnp.sum(q * (kf + kb), axis=-1, keepdims=True) * v
        q_dec = (q * jnp.exp(b_inc)).astype(BF16)
        o = o + lax.dot_general(q_dec, st.astype(BF16), nt, preferred_element_type=F32)
        k_dec = (kf * jnp.exp(b_inc[c - 1:c, :] - b_inc)).astype(BF16)
        vt_bf = v.T.astype(BF16)
        st = st * jnp.exp(b_inc[c - 1:c, :]) + jnp.dot(vt_bf, k_dec, preferred_element_type=F32)
        acc_ref[rows, :] = o
        qb_ref[rows, :] = (q * jnp.exp(bb)).astype(BF16)
        kb_ref[rows, :] = (kb * jnp.exp(bb[0:1, :] - bb)).astype(BF16)
        db_ref[ci] = jnp.broadcast_to(jnp.exp(bb[0:1, :]), (8, LANES))
        return st

    lax.fori_loop(0, n_chunks, fwd_body, jnp.zeros((c, c), F32), unroll=4)

    gain = gain_ref[...]

    def bwd_body(i, st):
        ci = n_chunks - 1 - i
        rows = pl.ds(pl.multiple_of(ci * c, c), c)
        o = acc_ref[rows, :] + lax.dot_general(qb_ref[rows, :], st.astype(BF16), nt,
                                               preferred_element_type=F32)
        vt_bf = i_ref[rows, :].T.astype(BF16)
        st = st * db_ref[ci][0:1, :] + jnp.dot(vt_bf, kb_ref[rows, :], preferred_element_type=F32)
        o = o * lax.rsqrt(jnp.mean(o * o, axis=-1, keepdims=True) + NORM_EPS) * gain
        gr = g_ref[rows, :]
        o_ref[rows, :] = (o * (gr * jax.nn.sigmoid(gr))).astype(o_ref.dtype)
        return st

    lax.fori_loop(0, n_chunks, bwd_body, jnp.zeros((c, c), F32), unroll=4)


def hgrn2_mixer(proj3, lb_logits, norm_gain):
    b, l, _ = proj3.shape
    h, dh = HG_HEADS, HG_HEAD_DIM
    mf, mb, lv = _gla_constants()
    nblk = mf.shape[0]

    def col(off):
        return pl.BlockSpec((None, l, dh), lambda bi, hi, off=off: (bi, 0, off + hi))

    const2 = lambda bi, hi: (0, 0)
    return pl.pallas_call(
        _hgrn_kernel,
        grid=(b, h),
        in_specs=[col(0), col(h), col(2 * h), col(3 * h), col(4 * h),
                  pl.BlockSpec((2, 2, dh), lambda bi, hi: (0, 0, hi)),
                  pl.BlockSpec((1, dh), lambda bi, hi: (0, hi)),
                  pl.BlockSpec((nblk, GLA_CHUNK), const2),
                  pl.BlockSpec((nblk, GLA_CHUNK), const2),
                  pl.BlockSpec((GLA_CHUNK, GLA_CHUNK), const2)],
        out_specs=pl.BlockSpec((None, l, dh), lambda bi, hi: (bi, 0, hi)),
        out_shape=jax.ShapeDtypeStruct((b, l, HG_WIDTH), BF16),
        scratch_shapes=[pltpu.VMEM((l, dh), F32), pltpu.VMEM((l, dh), BF16), pltpu.VMEM((l, dh), BF16),
                        pltpu.VMEM((l // GLA_CHUNK, 8, LANES), F32)],
        compiler_params=_cparams(("arbitrary", "arbitrary")),
        name="hgrn2_mixer",
    )(proj3, proj3, proj3, proj3, proj3, lb_logits, norm_gain.reshape(1, HG_WIDTH), mf, mb, lv)


@functools.lru_cache(maxsize=None)
def _dft_tables(l):
    n = 2 * l
    k2 = 2 * np.arange(l, dtype=np.int64)[:, None] + 1
    m1 = np.arange(l, dtype=np.int64)[None, :]
    ang_s = ((k2 * (2 * m1 + 1)) % (4 * n)).astype(np.float64) * (2.0 * math.pi / (4 * n))
    ang_f = ((k2 * m1) % (2 * n)).astype(np.float64) * (2.0 * math.pi / (2 * n))
    sym = np.stack([np.cos(ang_s), np.sin(ang_s)]).astype(np.float32)
    flt = np.stack([np.cos(ang_f), np.sin(ang_f)]).astype(np.float32)
    return sym, flt


def _filter_features(l):
    pos = jnp.arange(l, dtype=F32)
    t = pos / max(l - 1, 1)
    bands = jnp.linspace(1e-4, HY_BANDS - 1, HY_BANDS, dtype=F32)
    ang = (2.0 * math.pi / l) * pos[:, None] * bands[None, :]
    z = jnp.concatenate([t[:, None], jnp.cos(ang), -jnp.sin(ang)], axis=-1)
    z = jnp.pad(z, ((0, 0), (0, LANES - HY_EMB)))
    min_decay = math.log(HY_DECAY_TARGET) / HY_FAST_DECAY_PCT
    max_decay = math.log(HY_DECAY_TARGET) / HY_SLOW_DECAY_PCT
    deltas = jnp.abs(jnp.linspace(min_decay, max_decay, HY_WIDTH, dtype=F32))
    window = jnp.exp(-t[:, None] * deltas[None, :])
    return z, window


def _filter_kernel(z_ref, win_ref, w1_ref, b1_ref, w2_ref, b2_ref, w3_ref, b3_ref, w4_ref, fr_ref,
                   sum_ref, diff_ref):
    def split(a):
        hi = a.astype(BF16)
        return hi, (a - hi.astype(F32)).astype(BF16)

    def dot3(a, b):
        a_hi, a_lo = split(a)
        b_hi, b_lo = split(b)
        return (jnp.dot(a_hi, b_hi, preferred_element_type=F32) + jnp.dot(a_hi, b_lo, preferred_element_type=F32)
                + jnp.dot(a_lo, b_hi, preferred_element_type=F32))

    fr = fr_ref[...]
    h = jnp.sin(fr * (dot3(z_ref[...], w1_ref[...]) + b1_ref[...]))
    h = jnp.sin(fr * (dot3(h, w2_ref[...]) + b2_ref[...]))
    h = jnp.sin(fr * (dot3(h, w3_ref[...]) + b3_ref[...]))
    h = dot3(h, w4_ref[...])
    win = win_ref[...]
    tl = h.shape[0]
    lag = pl.program_id(0) * tl + lax.broadcasted_iota(jnp.int32, (tl, HY_WIDTH), 0)
    for o in range(HY_ORDER):
        hf = h[:, (2 * o) * HY_WIDTH:(2 * o + 1) * HY_WIDTH] * win
        hb = h[:, (2 * o + 1) * HY_WIDTH:(2 * o + 2) * HY_WIDTH] * win
        hb = jnp.where(lag == 0, 0.0, hb)
        sum_ref[:, o * HY_WIDTH:(o + 1) * HY_WIDTH] = (hf + hb).astype(sum_ref.dtype)
        diff_ref[:, o * HY_WIDTH:(o + 1) * HY_WIDTH] = (hf - hb).astype(diff_ref.dtype)


def hyena_filter_taps(l, w1, b1, w2, b2, w3, b3, w4, freq, tl=256):
    z, window = _filter_features(l)
    w1p = jnp.pad(w1, ((0, LANES - HY_EMB), (0, 0)))
    hid = HY_FILTER_HIDDEN
    full = lambda shape: pl.BlockSpec(shape, lambda i: (0,) * len(shape))
    nout = HY_ORDER * HY_WIDTH
    return pl.pallas_call(
        _filter_kernel,
        grid=(l // tl,),
        in_specs=[pl.BlockSpec((tl, LANES), lambda i: (i, 0)),
                  pl.BlockSpec((tl, HY_WIDTH), lambda i: (i, 0)),
                  full((LANES, hid)), full((1, hid)), full((hid, hid)), full((1, hid)),
                  full((hid, hid)), full((1, hid)), full((hid, 2 * nout)), full((1, hid))],
        out_specs=[pl.BlockSpec((tl, nout), lambda i: (i, 0)), pl.BlockSpec((tl, nout), lambda i: (i, 0))],
        out_shape=[jax.ShapeDtypeStruct((l, nout), BF16), jax.ShapeDtypeStruct((l, nout), BF16)],
        compiler_params=_cparams(("arbitrary",)),
        name="hyena_filter_taps",
    )(z, window, w1p, b1.reshape(1, hid), w2, b2.reshape(1, hid), w3, b3.reshape(1, hid), w4,
      freq.reshape(1, hid))


def _spectrum_kernel(f_ref, h_ref, o_ref):
    o_ref[...] = jnp.dot(f_ref[...], h_ref[...], preferred_element_type=F32).astype(o_ref.dtype)


def hyena_filter_spectra(flt, taps, tn=512):
    _, l, nout = taps.shape
    return pl.pallas_call(
        _spectrum_kernel,
        grid=(2, nout // tn),
        in_specs=[pl.BlockSpec((None, l, l), lambda p, j: (p, 0, 0)),
                  pl.BlockSpec((None, l, tn), lambda p, j: (p, 0, j))],
        out_specs=pl.BlockSpec((None, l, tn), lambda p, j: (p, 0, j)),
        out_shape=jax.ShapeDtypeStruct((2, l, nout), BF16),
        compiler_params=_cparams(("arbitrary", "arbitrary")),
        name="hyena_filter_spectra",
    )(flt, taps)


def _hyena_kernel(xv_ref, x1_ref, x2_ref, cw_ref, cb_ref, sym_ref, sp0_ref, sp1_ref, bias_ref, o_ref,
                  z_s, g_s, zb_s, pa_s, pb_s, *, rb):
    l, tc = xv_ref.shape
    scale = 2.0 / (2 * l)
    first = lax.broadcasted_iota(jnp.int32, (l, tc), 0) == 0
    last = lax.broadcasted_iota(jnp.int32, (l, tc), 0) == l - 1

    def short_conv(x_ref, part):
        x = x_ref[...]
        w = cw_ref[part]
        prev = jnp.where(first, 0.0, pltpu.roll(x, 1, 0))
        nxt = jnp.where(last, 0.0, pltpu.roll(x, l - 1, 0))
        return prev * w[0:1, :] + x * w[1:2, :] + nxt * w[2:3, :] + cb_ref[part]

    z_s[...] = short_conv(xv_ref, 0)
    for o, (gate_ref, part, sp_ref) in enumerate(((x1_ref, 1, sp0_ref), (x2_ref, 2, sp1_ref))):
        zb_s[...] = z_s[...].astype(BF16)
        g_s[...] = short_conv(gate_ref, part)
        for r0 in range(0, l, rb):
            a = jnp.dot(sym_ref[0, r0:r0 + rb, :], zb_s[...], preferred_element_type=F32)
            b = jnp.dot(sym_ref[1, r0:r0 + rb, :], zb_s[...], preferred_element_type=F32)
            hr = sp_ref[0, r0:r0 + rb, :].astype(F32)
            hs = sp_ref[1, r0:r0 + rb, :].astype(F32)
            pa_s[r0:r0 + rb, :] = (a * hr - b * hs).astype(BF16)
            pb_s[r0:r0 + rb, :] = (a * hs + b * hr).astype(BF16)
        for r0 in range(0, l, rb):
            y = (jnp.dot(sym_ref[0, r0:r0 + rb, :], pa_s[...], preferred_element_type=F32)
                 + jnp.dot(sym_ref[1, r0:r0 + rb, :], pb_s[...], preferred_element_type=F32)) * scale
            y = y + z_s[r0:r0 + rb, :] * bias_ref[o]
            z_s[r0:r0 + rb, :] = g_s[r0:r0 + rb, :] * y
    o_ref[...] = z_s[...].astype(o_ref.dtype)


def hyena_mixer(proj3, conv_w, conv_b, sym, spectra, hy_bias, tc=256, rb=1024):
    b, l, _ = proj3.shape
    nct = HY_WIDTH // tc
    base = (5 * HG_WIDTH) // tc

    def col(part):
        return pl.BlockSpec((None, l, tc), lambda ci, bi, part=part: (bi, 0, base + part * nct + ci))

    def spec_cols(o):
        return pl.BlockSpec((2, l, tc), lambda ci, bi, o=o: (0, 0, o * nct + ci))

    cw = conv_w.reshape(3, HY_ORDER + 1, HY_WIDTH).transpose(1, 0, 2)
    cb = conv_b.reshape(HY_ORDER + 1, 1, HY_WIDTH)
    return pl.pallas_call(
        functools.partial(_hyena_kernel, rb=rb),
        grid=(nct, b),
        in_specs=[col(0), col(1), col(2),
                  pl.BlockSpec((HY_ORDER + 1, 3, tc), lambda ci, bi: (0, 0, ci)),
                  pl.BlockSpec((HY_ORDER + 1, 1, tc), lambda ci, bi: (0, 0, ci)),
                  pl.BlockSpec((2, l, l), lambda ci, bi: (0, 0, 0), pipeline_mode=pl.Buffered(1)),
                  spec_cols(0), spec_cols(1),
                  pl.BlockSpec((HY_ORDER, 1, tc), lambda ci, bi: (0, 0, ci))],
        out_specs=pl.BlockSpec((None, l, tc), lambda ci, bi: (bi, 0, ci)),
        out_shape=jax.ShapeDtypeStruct((b, l, HY_WIDTH), BF16),
        scratch_shapes=[pltpu.VMEM((l, tc), F32), pltpu.VMEM((l, tc), F32), pltpu.VMEM((l, tc), BF16),
                        pltpu.VMEM((l, tc), BF16), pltpu.VMEM((l, tc), BF16)],
        compiler_params=_cparams(("arbitrary", "arbitrary")),
        name="hyena_mixer",
    )(proj3, proj3, proj3, cw, cb, sym, spectra, spectra, hy_bias.reshape(HY_ORDER, 1, HY_WIDTH))


def _outproj_router_kernel(hg_ref, hy_ref, x_ref, w_ref, hyg_ref, fg_ref, wr_ref, br_ref, tri_ref,
                           x2_ref, xn_ref, meta_ref, metat_ref, cnt_ref, carry_ref):
    @pl.when(pl.program_id(0) == 0)
    def _():
        carry_ref[...] = jnp.zeros_like(carry_ref)

    hy = hy_ref[...].astype(F32)
    hy = hy * lax.rsqrt(jnp.mean(hy * hy, axis=-1, keepdims=True) + NORM_EPS) * hyg_ref[...]
    m = (jnp.dot(hg_ref[...], w_ref[0:HG_WIDTH, :], preferred_element_type=F32)
         + jnp.dot(hy.astype(BF16), w_ref[HG_WIDTH:, :], preferred_element_type=F32))
    x2 = x_ref[...] + m
    x2_ref[...] = x2
    xn = x2 * lax.rsqrt(jnp.mean(x2 * x2, axis=-1, keepdims=True) + NORM_EPS) * fg_ref[...]
    xn_ref[...] = xn
    xn_hi = xn.astype(BF16)
    xn_lo = (xn - xn_hi.astype(F32)).astype(BF16)
    wr = wr_ref[...]
    wr_hi = wr.astype(BF16)
    wr_lo = (wr - wr_hi.astype(F32)).astype(BF16)
    logits = (jnp.dot(xn_hi, wr_hi, preferred_element_type=F32)
              + jnp.dot(xn_hi, wr_lo, preferred_element_type=F32)
              + jnp.dot(xn_lo, wr_hi, preferred_element_type=F32)) + br_ref[...]
    tm = logits.shape[0]
    lane = lax.broadcasted_iota(jnp.int32, (tm, LANES), 1).astype(F32)
    neg = jnp.float32(-jnp.inf)
    logits = jnp.where(lane < N_EXPERTS, logits, neg)
    tops, idxs = [], []
    for _ in range(TOP_K):
        mx = jnp.max(logits, axis=-1, keepdims=True)
        idx = jnp.min(jnp.where(logits == mx, lane, float(LANES)), axis=-1, keepdims=True)
        tops.append(mx)
        idxs.append(idx)
        logits = jnp.where(lane == idx, neg, logits)
    exps = [jnp.exp(t - tops[0]) for t in tops]
    denom = exps[0] + exps[1] + exps[2] + exps[3]
    onehot = jnp.zeros((tm, LANES), F32)
    for idx in idxs:
        onehot = onehot + jnp.where(lane == idx, 1.0, 0.0)
    cum = jnp.dot(tri_ref[...], onehot.astype(BF16), preferred_element_type=F32) + carry_ref[0:1, :]
    meta = jnp.zeros((tm, LANES), F32)
    for k in range(TOP_K):
        rank = jnp.sum(jnp.where(lane == idxs[k], cum, 0.0), axis=-1, keepdims=True)
        meta = jnp.where(lane == k, idxs[k], meta)
        meta = jnp.where(lane == TOP_K + k, rank, meta)
        meta = jnp.where(lane == 2 * TOP_K + k, exps[k] / denom, meta)
    meta_ref[...] = meta
    metat_ref[...] = meta.T[0:metat_ref.shape[0], :]
    carry = carry_ref[...] + jnp.sum(onehot, axis=0, keepdims=True)
    carry_ref[...] = carry
    cnt_ref[...] = carry


def outproj_router(hg2d, hy2d, x2d, w_out_bf16, hy_gain, ffn_gain, w_router, b_router, tm=512):
    n, d = x2d.shape
    wr = jnp.pad(w_router, ((0, 0), (0, LANES - N_EXPERTS)))
    br = jnp.pad(b_router, (0, LANES - N_EXPERTS)).reshape(1, LANES)
    tri = jnp.asarray(np.tril(np.ones((tm, tm), np.float32), -1), BF16)
    row = lambda w: pl.BlockSpec((tm, w), lambda i: (i, 0))
    full = lambda shape, **kw: pl.BlockSpec(shape, lambda i: (0,) * len(shape), **kw)
    return pl.pallas_call(
        _outproj_router_kernel,
        grid=(n // tm,),
        in_specs=[row(HG_WIDTH), row(HY_WIDTH), row(d),
                  full((d, d), pipeline_mode=pl.Buffered(1)),
                  full((1, HY_WIDTH)), full((1, d)), full((d, LANES)), full((1, LANES)), full((tm, tm))],
        out_specs=[row(d), row(d), row(LANES), pl.BlockSpec((2 * TOP_K, tm), lambda i: (0, i)),
                   full((8, LANES))],
        out_shape=[jax.ShapeDtypeStruct((n, d), F32), jax.ShapeDtypeStruct((n, d), F32),
                   jax.ShapeDtypeStruct((n, LANES), F32), jax.ShapeDtypeStruct((2 * TOP_K, n), F32),
                   jax.ShapeDtypeStruct((8, LANES), F32)],
        scratch_shapes=[pltpu.VMEM((8, LANES), F32)],
        compiler_params=_cparams(("arbitrary",)),
        name="outproj_router",
    )(hg2d, hy2d, x2d, w_out_bf16, hy_gain.reshape(1, HY_WIDTH), ffn_gain.reshape(1, d), wr, br, tri)


GATHER_UNROLL = 8


def _row_gather_kernel(dest_ref, rows_ref, used_ref, x_hbm, o_ref, slot_s, buf_ref, sems, *, n_tok):
    t = pl.program_id(0)
    tr = buf_ref.shape[1]
    used = used_ref[0]

    def n_copies(tile):
        return ((rows_ref[tile] + GATHER_UNROLL - 1) // GATHER_UNROLL) * GATHER_UNROLL

    def source_row(tile, r):
        return slot_s[tile * tr + jnp.minimum(r, rows_ref[tile] - 1)]

    def issue(tile):
        nc = n_copies(tile)
        for b in range(2):
            @pl.when(tile % 2 == b)
            def _(b=b):
                for g in range(tr // GATHER_UNROLL):
                    @pl.when(g * GATHER_UNROLL < nc)
                    def _(g=g):
                        for u in range(GATHER_UNROLL):
                            r = g * GATHER_UNROLL + u
                            pltpu.make_async_copy(x_hbm.at[pl.ds(source_row(tile, r), 1)],
                                                  buf_ref.at[b, pl.ds(r, 1)], sems.at[b]).start()

    @pl.when(t == 0)
    def _():
        def scatter(a, carry):
            tok = (a & (n_tok - 1)) if n_tok & (n_tok - 1) == 0 else lax.rem(a, jnp.int32(n_tok))
            slot_s[dest_ref[a]] = tok
            return carry

        lax.fori_loop(0, dest_ref.shape[0], scatter, 0, unroll=16)
        buf_ref[...] = jnp.zeros(buf_ref.shape, buf_ref.dtype)

        def body(r, carry):
            pltpu.make_async_copy(x_hbm.at[pl.ds(source_row(0, r), 1)], buf_ref.at[0, pl.ds(r, 1)],
                                  sems.at[0]).start()
            return carry

        lax.fori_loop(0, n_copies(0), body, 0)

    @pl.when(t + 1 < used)
    def _():
        issue(t + 1)

    @pl.when(t < used)
    def _():
        b = t % 2
        nc = n_copies(t)

        @pl.when(nc > 0)
        def _():
            pltpu.make_async_copy(x_hbm.at[pl.ds(0, nc)], buf_ref.at[b, pl.ds(0, nc)], sems.at[b]).wait()

        row = lax.broadcasted_iota(jnp.int32, o_ref.shape, 0)
        o_ref[...] = jnp.where(row < rows_ref[t], buf_ref[b], 0.0).astype(o_ref.dtype)


def gather_rows(dest_flat, tile_rows, used_tiles, xn, n_slots, tr):
    n, d = xn.shape
    grid_spec = pltpu.PrefetchScalarGridSpec(
        num_scalar_prefetch=3,
        grid=(n_slots // tr,),
        in_specs=[pl.BlockSpec(memory_space=pl.ANY)],
        out_specs=pl.BlockSpec((tr, d), lambda t, dest, rows, used: (jnp.minimum(t, used[0] - 1), 0)),
        scratch_shapes=[pltpu.SMEM((n_slots,), jnp.int32), pltpu.VMEM((2, tr, d), F32),
                        pltpu.SemaphoreType.DMA((2,))],
    )
    return pl.pallas_call(
        functools.partial(_row_gather_kernel, n_tok=n),
        grid_spec=grid_spec,
        out_shape=jax.ShapeDtypeStruct((n_slots, d), BF16),
        compiler_params=pltpu.CompilerParams(dimension_semantics=("arbitrary",), vmem_limit_bytes=VMEM_LIMIT,
                                             disable_bounds_checks=True),
        name="moe_gather_rows",
    )(dest_flat, tile_rows, used_tiles, xn)


def _expert_kernel(te1_ref, tv1_ref, tb1_ref, te2_ref, tv2_ref, tb2_ref,
                   x_ref, wg_ref, wl_ref, bg_ref, bl_ref, w2_ref, b2_ref, o_ref, act_s, *, nf, blocks):
    jj = pl.program_id(0)
    s = pl.program_id(1)
    tf = wg_ref.shape[1]
    valid_in = tv1_ref[jj]
    valid_out = tv2_ref[jj]
    buf_in = jj % 2
    buf_out = 1 - buf_in

    @pl.when(valid_in > 0)
    def _():
        for r0, rn in blocks:
            @pl.when(r0 < valid_in)
            def _(r0=r0, rn=rn):
                x = x_ref[r0:r0 + rn, :]
                glu = jnp.dot(x, wg_ref[...].astype(BF16), preferred_element_type=F32) + bg_ref[...]
                lin = jnp.dot(x, wl_ref[...].astype(BF16), preferred_element_type=F32) + bl_ref[...]
                glu = jnp.minimum(glu, SWIGLU_LIMIT)
                lin = jnp.clip(lin, -SWIGLU_LIMIT, SWIGLU_LIMIT)
                act = glu * jax.nn.sigmoid(SWIGLU_ALPHA * glu) * (lin + 1.0)
                act_s[buf_in, s, r0:r0 + rn, :] = act.astype(BF16)

    @pl.when(valid_out > 0)
    def _():
        for r0, rn in blocks:
            @pl.when(r0 < valid_out)
            def _(r0=r0, rn=rn):
                y = b2_ref[...] + jnp.dot(act_s[buf_out, 0, r0:r0 + rn, :], w2_ref[0:tf, :].astype(BF16),
                                          preferred_element_type=F32)
                for f in range(1, nf):
                    y = y + jnp.dot(act_s[buf_out, f, r0:r0 + rn, :],
                                    w2_ref[f * tf:(f + 1) * tf, :].astype(BF16),
                                    preferred_element_type=F32)
                o_ref[r0:r0 + rn, :] = y

            @pl.when(r0 >= valid_out)
            def _(r0=r0, rn=rn):
                o_ref[r0:r0 + rn, :] = jnp.zeros((rn, o_ref.shape[1]), o_ref.dtype)


def expert_ffn(tile_expert, tile_valid, tile_block, xs, w1, b1, w2, b2, tm, blocks, tf=512):
    n_slots, d = xs.shape
    n_tiles = n_slots // tm
    nf = D_FF // tf
    assert d // tf == nf
    b1r = b1.reshape(N_EXPERTS, 1, 2 * D_FF)
    b2r = b2.reshape(N_EXPERTS, 1, d)
    zero = jnp.zeros((1,), jnp.int32)
    te1 = jnp.concatenate([tile_expert, tile_expert[-1:]])
    tv1 = jnp.concatenate([tile_valid, zero])
    tb1 = jnp.concatenate([tile_block, tile_block[-1:]])
    te2 = jnp.concatenate([tile_expert[:1], tile_expert])
    tv2 = jnp.concatenate([zero, tile_valid])
    tb2 = jnp.concatenate([tile_block[:1], tile_block])

    def blk1(jj, s, tv1):
        return jnp.where(tv1[jj] > 0, s, nf - 1)

    def blk2(jj, s, tv2):
        return jnp.where(tv2[jj] > 0, s, jnp.where(jj == 0, 0, nf - 1))

    grid_spec = pltpu.PrefetchScalarGridSpec(
        num_scalar_prefetch=6,
        grid=(n_tiles + 1, nf),
        in_specs=[
            pl.BlockSpec((tm, d), lambda jj, s, te1, tv1, tb1, te2, tv2, tb2: (tb1[jj], 0)),
            pl.BlockSpec((None, d, tf),
                         lambda jj, s, te1, tv1, tb1, te2, tv2, tb2: (te1[jj], 0, blk1(jj, s, tv1))),
            pl.BlockSpec((None, d, tf),
                         lambda jj, s, te1, tv1, tb1, te2, tv2, tb2: (te1[jj], 0, nf + blk1(jj, s, tv1))),
            pl.BlockSpec((None, 1, tf),
                         lambda jj, s, te1, tv1, tb1, te2, tv2, tb2: (te1[jj], 0, blk1(jj, s, tv1))),
            pl.BlockSpec((None, 1, tf),
                         lambda jj, s, te1, tv1, tb1, te2, tv2, tb2: (te1[jj], 0, nf + blk1(jj, s, tv1))),
            pl.BlockSpec((None, D_FF, tf),
                         lambda jj, s, te1, tv1, tb1, te2, tv2, tb2: (te2[jj], 0, blk2(jj, s, tv2))),
            pl.BlockSpec((None, 1, tf),
                         lambda jj, s, te1, tv1, tb1, te2, tv2, tb2: (te2[jj], 0, blk2(jj, s, tv2))),
        ],
        out_specs=pl.BlockSpec((tm, tf),
                               lambda jj, s, te1, tv1, tb1, te2, tv2, tb2: (tb2[jj], blk2(jj, s, tv2))),
        scratch_shapes=[pltpu.VMEM((2, nf, tm, tf), BF16)],
    )
    return pl.pallas_call(
        functools.partial(_expert_kernel, nf=nf, blocks=blocks),
        grid_spec=grid_spec,
        out_shape=jax.ShapeDtypeStruct((n_slots, d), F32),
        compiler_params=_cparams(("arbitrary", "arbitrary")),
        name="moe_expert_ffn",
    )(te1, tv1, tb1, te2, tv2, tb2, xs, w1, w1, b1r, b1r, w2, b2r)


def _combine_kernel(dest_ref, ys_hbm, x2_ref, meta_ref, gain_ref, o_ref, buf_ref, sems):
    t = pl.program_id(0)
    nt = pl.num_programs(0)
    tt = x2_ref.shape[0]
    n_tok = dest_ref.shape[0] // TOP_K

    def issue_rolled(tile, b):
        base = tile * tt

        def body(r, carry):
            for k in range(TOP_K):
                pltpu.make_async_copy(ys_hbm.at[pl.ds(dest_ref[k * n_tok + base + r], 1)],
                                      buf_ref.at[b, k, pl.ds(r, 1)], sems.at[b]).start()
            return carry

        lax.fori_loop(0, tt, body, 0, unroll=4)

    def issue(tile):
        base = tile * tt
        for b in range(2):
            @pl.when(tile % 2 == b)
            def _(b=b):
                for r in range(tt):
                    for k in range(TOP_K):
                        pltpu.make_async_copy(ys_hbm.at[pl.ds(dest_ref[k * n_tok + base + r], 1)],
                                              buf_ref.at[b, k, pl.ds(r, 1)], sems.at[b]).start()

    @pl.when(t == 0)
    def _():
        issue_rolled(t, 0)

    @pl.when(t + 1 < nt)
    def _():
        issue(t + 1)

    b = t % 2
    for k in range(TOP_K):
        pltpu.make_async_copy(ys_hbm.at[pl.ds(0, tt)], buf_ref.at[b, k], sems.at[b]).wait()
    meta = meta_ref[...]
    y = x2_ref[...]
    for k in range(TOP_K):
        y = y + meta[:, 2 * TOP_K + k:2 * TOP_K + k + 1] * buf_ref[b, k]
    o_ref[...] = y * lax.rsqrt(jnp.mean(y * y, axis=-1, keepdims=True) + NORM_EPS) * gain_ref[...]


def combine_final(dest_flat, ys, x2, meta, final_gain, tt=128):
    n, d = x2.shape
    grid_spec = pltpu.PrefetchScalarGridSpec(
        num_scalar_prefetch=1,
        grid=(n // tt,),
        in_specs=[pl.BlockSpec(memory_space=pl.ANY),
                  pl.BlockSpec((tt, d), lambda t, dest: (t, 0)),
                  pl.BlockSpec((tt, LANES), lambda t, dest: (t, 0)),
                  pl.BlockSpec((1, d), lambda t, dest: (0, 0))],
        out_specs=pl.BlockSpec((tt, d), lambda t, dest: (t, 0)),
        scratch_shapes=[pltpu.VMEM((2, TOP_K, tt, d), F32), pltpu.SemaphoreType.DMA((2,))],
    )
    return pl.pallas_call(
        _combine_kernel,
        grid_spec=grid_spec,
        out_shape=jax.ShapeDtypeStruct((n, d), F32),
        compiler_params=pltpu.CompilerParams(dimension_semantics=("arbitrary",), vmem_limit_bytes=VMEM_LIMIT,
                                             disable_bounds_checks=True),
        name="moe_combine_final",
    )(dest_flat, ys, x2, meta, final_gain.reshape(1, d))


MOE_TILE = 1152
MOE_ROW_BLOCKS = ((0, 512), (512, 512), (1024, 128))
MOE_GATHER_TILE = 384


def moe_ffn_final(x2, xn, meta, meta_t, counts, w1, b1, w2, b2, final_gain, tm=MOE_TILE):
    n, d = x2.shape
    a = n * TOP_K
    n_tiles = -(-a // tm) + N_EXPERTS
    n_slots = n_tiles * tm
    idx = meta_t[0:TOP_K].astype(jnp.int32)
    rank = meta_t[TOP_K:2 * TOP_K].astype(jnp.int32)
    cnt = counts[0, :N_EXPERTS].astype(jnp.int32)
    tiles_e = (cnt + tm - 1) // tm
    tile_end = jnp.cumsum(tiles_e)
    tile_start = tile_end - tiles_e
    used = tile_end[-1]
    experts = jnp.arange(N_EXPERTS, dtype=jnp.int32)[:, None, None]
    first_slot = jnp.sum(jnp.where(idx[None] == experts, (tile_start * tm)[:, None, None], 0), axis=0)
    dest = first_slot + rank
    dest_flat = dest.reshape(a)
    tj = jnp.arange(n_tiles, dtype=jnp.int32)
    te = jnp.minimum(jnp.sum(tj[:, None] >= tile_end[None, :], axis=1), N_EXPERTS - 1).astype(jnp.int32)
    last_e = te[jnp.maximum(used - 1, 0)]
    tile_expert = jnp.where(tj < used, te, last_e).astype(jnp.int32)
    tile_valid = jnp.where(tj < used, jnp.clip(cnt[te] - (tj - tile_start[te]) * tm, 0, tm), 0).astype(jnp.int32)
    tile_block = jnp.maximum(jnp.minimum(tj, used - 1), 0).astype(jnp.int32)

    tr = MOE_GATHER_TILE
    per = tm // tr
    gj = jnp.arange(n_tiles * per, dtype=jnp.int32)
    gather_rows_valid = jnp.clip(tile_valid[gj // per] - (gj % per) * tr, 0, tr).astype(jnp.int32)
    used_gather = jnp.maximum(used * per, 1).astype(jnp.int32).reshape(1)
    xs = gather_rows(dest_flat, gather_rows_valid, used_gather, xn, n_slots, tr)
    ys = expert_ffn(tile_expert, tile_valid, tile_block, xs, w1, b1, w2, b2, tm, MOE_ROW_BLOCKS)
    return combine_final(dest_flat, ys, x2, meta, final_gain)


def kernel(x, norm_mix_gain, w_in, hgrn_lb_logits, hgrn_norm_gain, hy_conv_w, hy_conv_b, hy_filt_w1,
           hy_filt_b1, hy_filt_w2, hy_filt_b2, hy_filt_w3, hy_filt_b3, hy_filt_w4, hy_filt_freq, hy_bias,
           hy_norm_gain, w_out, norm_ffn_gain, w_router, b_router, moe_w1, moe_b1, moe_w2, moe_b2,
           final_norm_gain):
    b, l, d = x.shape
    n = b * l
    x2d = x.reshape(n, d)
    proj = in_projection(x2d, norm_mix_gain[0], w_in[0].astype(BF16))
    proj3 = proj.reshape(b, l, D_IN)
    hg = hgrn2_mixer(proj3, hgrn_lb_logits, hgrn_norm_gain[0])

    sym, flt = (jnp.asarray(tab).astype(BF16) for tab in _dft_tables(l))
    tap_sum, tap_diff = hyena_filter_taps(l, hy_filt_w1[0], hy_filt_b1[0], hy_filt_w2[0], hy_filt_b2[0],
                                          hy_filt_w3[0], hy_filt_b3[0], hy_filt_w4[0], hy_filt_freq[0])
    spectra = hyena_filter_spectra(flt, jnp.stack([tap_sum, tap_diff]))
    hy = hyena_mixer(proj3, hy_conv_w[0], hy_conv_b[0], sym, spectra, hy_bias[0])

    x2, xn, meta, meta_t, counts = outproj_router(hg.reshape(n, HG_WIDTH), hy.reshape(n, HY_WIDTH), x2d,
                                                  w_out[0].astype(BF16), hy_norm_gain[0], norm_ffn_gain[0],
                                                  w_router[0], b_router[0])
    out = moe_ffn_final(x2, xn, meta, meta_t, counts, moe_w1[0], moe_b1[0], moe_w2[0], moe_b2[0],
                        final_norm_gain)
    return out.reshape(b, l, d)
```

```python
import functools
import math

import numpy as np
import jax
import jax.numpy as jnp
from jax import lax
from jax.experimental import pallas as pl
from jax.experimental.pallas import tpu as pltpu

F32 = jnp.float32
BF16 = jnp.bfloat16

D_MODEL = 2048
HG_WIDTH = 1024
HG_HEAD_DIM = 128
HG_HEADS = HG_WIDTH // HG_HEAD_DIM
HY_WIDTH = 1024
HY_ORDER = 2
HY_EMB = 33
HY_BANDS = (HY_EMB - 1) // 2
HY_FILTER_HIDDEN = 64
HY_FAST_DECAY_PCT = 0.3
HY_SLOW_DECAY_PCT = 1.5
HY_DECAY_TARGET = 1e-2
N_EXPERTS = 32
TOP_K = 4
D_FF = D_MODEL
SWIGLU_ALPHA = 1.702
SWIGLU_LIMIT = 7.0
NORM_EPS = 1e-5
D_IN = 5 * HG_WIDTH + (HY_ORDER + 1) * HY_WIDTH

LANES = 128
GLA_CHUNK = 128
GLA_LEVELS = (64, 32, 16, 8, 4, 2, 1)
GLA_MATRIX_LEVELS = (4, 2)
VMEM_LIMIT = 56 * 1024 * 1024


def _cparams(sem, vmem=VMEM_LIMIT):
    return pltpu.CompilerParams(dimension_semantics=sem, vmem_limit_bytes=vmem)


def _inproj_kernel(x_ref, g_ref, w_ref, o_ref, h_s):
    @pl.when(pl.program_id(1) == 0)
    def _():
        x = x_ref[...]
        ms = jnp.mean(x * x, axis=-1, keepdims=True)
        h_s[...] = (x * lax.rsqrt(ms + NORM_EPS) * g_ref[...]).astype(BF16)

    o_ref[...] = jnp.dot(h_s[...], w_ref[...].astype(BF16), preferred_element_type=F32)


def in_projection(x2d, gain, w_in, tm=1024, tn=1024):
    n, d = x2d.shape
    dout = w_in.shape[1]
    return pl.pallas_call(
        _inproj_kernel,
        grid=(n // tm, dout // tn),
        in_specs=[
            pl.BlockSpec((tm, d), lambda i, j: (i, 0)),
            pl.BlockSpec((1, d), lambda i, j: (0, 0)),
            pl.BlockSpec((d, tn), lambda i, j: (0, j)),
        ],
        out_specs=pl.BlockSpec((tm, tn), lambda i, j: (i, j)),
        out_shape=jax.ShapeDtypeStruct((n, dout), F32),
        scratch_shapes=[pltpu.VMEM((tm, d), BF16)],
        compiler_params=_cparams(("arbitrary", "arbitrary")),
        name="in_projection",
    )(x2d, gain.reshape(1, d), w_in)


def _gla_constants():
    c = GLA_CHUNK
    t = np.arange(c)[:, None]
    r = np.arange(c)[None, :]
    fwd = [r <= t]
    bwd = [r >= t]
    for m in GLA_MATRIX_LEVELS:
        pos = t % (2 * m)
        mid = t - pos + m
        second = pos >= m
        fwd.append(np.where(second, (r >= mid) & (r <= t), (r > t) & (r < mid)))
        bwd.append(np.where(second, (r >= mid) & (r < t), (r >= t) & (r < mid)))
    x = t ^ r
    lv = np.full((c, c), -1, np.int32)
    for j in range(int(math.log2(c))):
        lv = np.where((x >> j) == 1, j, lv)
    mf = jnp.asarray(np.concatenate(fwd, 0).astype(np.float32), BF16)
    mb = jnp.asarray(np.concatenate(bwd, 0).astype(np.float32), BF16)
    return mf, mb, jnp.asarray(lv, jnp.int32)


def _hgrn_kernel(q_ref, ff_ref, fb_ref, i_ref, g_ref, lb_ref, gain_ref, mf_ref, mb_ref, lv_ref,
                 o_ref, acc_ref, qb_ref, kb_ref, db_ref):
    c = GLA_CHUNK
    n_chunks = q_ref.shape[0] // c
    lb = lb_ref[...]
    l0, l1 = lb[0], lb[1]
    mx = jnp.maximum(l0, l1)
    e0 = jnp.exp(l0 - mx)
    e1 = jnp.exp(l1 - mx)
    p0 = e0 / (e0 + e1)
    lb_f = p0[0:1, :]
    lb_b = p0[1:2, :]
    lv = lv_ref[...]
    row = lax.broadcasted_iota(jnp.int32, (c, LANES), 0)
    nt = (((1,), (1,)), ((), ()))

    def gates(z, lower):
        f = lower + (1.0 - lower) * jax.nn.sigmoid(z)
        return f, 1.0 - f, jnp.log(f)

    def boundary_diff(cum, m, offset):
        nb = c // (2 * m)
        c3 = cum.reshape(nb, 2 * m, LANES)
        ref = jnp.broadcast_to(c3[:, offset:offset + 1, :], (nb, 2 * m, LANES))
        return (c3 - ref).reshape(c, LANES)

    def exponents(m_ref, g):
        g_hi = g.astype(BF16)
        g_lo = (g - g_hi.astype(F32)).astype(BF16)
        m = m_ref[...]
        return (jnp.dot(m, g_hi, preferred_element_type=F32)
                + jnp.dot(m, g_lo, preferred_element_type=F32))

    def fwd_body(ci, st):
        rows = pl.ds(pl.multiple_of(ci * c, c), c)
        qr = q_ref[rows, :]
        q = qr * jax.nn.sigmoid(qr)
        v = i_ref[rows, :]
        ff, kf, gf = gates(ff_ref[rows, :], lb_f)
        fb, kb, gb = gates(fb_ref[rows, :], lb_b)
        ef_all = exponents(mf_ref, gf)
        eb_all = exponents(mb_ref, gb)
        b_inc = ef_all[0:c]
        bb = eb_all[0:c]
        scores = jnp.zeros((c, c), F32)
        for m in GLA_LEVELS:
            second = (row & m) != 0
            k_sel = jnp.where(second, kb, kf)
            if m == 1:
                a = q * jnp.where(second, ff, fb)
                b = k_sel
            else:
                if m in GLA_MATRIX_LEVELS:
                    li = 1 + GLA_MATRIX_LEVELS.index(m)
                    ef = ef_all[li * c:(li + 1) * c]
                    eb = eb_all[li * c:(li + 1) * c]
                    e_query = jnp.where(second, ef, eb)
                    e_key = jnp.where(second, eb, ef)
                else:
                    df = boundary_diff(b_inc, m, m - 1)
                    db = boundary_diff(bb, m, m)
                    e_query = jnp.where(second, df, db)
                    e_key = -jnp.where(second, db, df)
                a = q * jnp.exp(e_query)
                b = k_sel * jnp.exp(e_key)
            s = lax.dot_general(a.astype(BF16), b.astype(BF16), nt, preferred_element_type=F32)
            scores = jnp.where(lv == int(math.log2(m)), s, scores)
        v_bf = v.astype(BF16)
        o = jnp.dot(scores.astype(BF16), v_bf, preferred_element_type=F32)
        o = o + jnp.sum(q * (kf + kb), axis=-1, keepdims=True) * v
        q_dec = (q * jnp.exp(b_inc)).astype(BF16)
        o = o + lax.dot_general(q_dec, st.astype(BF16), nt, preferred_element_type=F32)
        k_dec = (kf * jnp.exp(b_inc[c - 1:c, :] - b_inc)).astype(BF16)
        vt_bf = v.T.astype(BF16)
        st = st * jnp.exp(b_inc[c - 1:c, :]) + jnp.dot(vt_bf, k_dec, preferred_element_type=F32)
        acc_ref[rows, :] = o
        qb_ref[rows, :] = (q * jnp.exp(bb)).astype(BF16)
        kb_ref[rows, :] = (kb * jnp.exp(bb[0:1, :] - bb)).astype(BF16)
        db_ref[ci] = jnp.broadcast_to(jnp.exp(bb[0:1, :]), (8, LANES))
        return st

    lax.fori_loop(0, n_chunks, fwd_body, jnp.zeros((c, c), F32), unroll=4)

    gain = gain_ref[...]

    def bwd_body(i, st):
        ci = n_chunks - 1 - i
        rows = pl.ds(pl.multiple_of(ci * c, c), c)
        o = acc_ref[rows, :] + lax.dot_general(qb_ref[rows, :], st.astype(BF16), nt,
                                               preferred_element_type=F32)
        vt_bf = i_ref[rows, :].T.astype(BF16)
        st = st * db_ref[ci][0:1, :] + jnp.dot(vt_bf, kb_ref[rows, :], preferred_element_type=F32)
        o = o * lax.rsqrt(jnp.mean(o * o, axis=-1, keepdims=True) + NORM_EPS) * gain
        gr = g_ref[rows, :]
        o_ref[rows, :] = (o * (gr * jax.nn.sigmoid(gr))).astype(o_ref.dtype)
        return st

    lax.fori_loop(0, n_chunks, bwd_body, jnp.zeros((c, c), F32), unroll=4)


def hgrn2_mixer(proj3, lb_logits, norm_gain):
    b, l, _ = proj3.shape
    h, dh = HG_HEADS, HG_HEAD_DIM
    mf, mb, lv = _gla_constants()
    nblk = mf.shape[0]

    def col(off):
        return pl.BlockSpec((None, l, dh), lambda bi, hi, off=off: (bi, 0, off + hi))

    const2 = lambda bi, hi: (0, 0)
    return pl.pallas_call(
        _hgrn_kernel,
        grid=(b, h),
        in_specs=[col(0), col(h), col(2 * h), col(3 * h), col(4 * h),
                  pl.BlockSpec((2, 2, dh), lambda bi, hi: (0, 0, hi)),
                  pl.BlockSpec((1, dh), lambda bi, hi: (0, hi)),
                  pl.BlockSpec((nblk, GLA_CHUNK), const2),
                  pl.BlockSpec((nblk, GLA_CHUNK), const2),
                  pl.BlockSpec((GLA_CHUNK, GLA_CHUNK), const2)],
        out_specs=pl.BlockSpec((None, l, dh), lambda bi, hi: (bi, 0, hi)),
        out_shape=jax.ShapeDtypeStruct((b, l, HG_WIDTH), BF16),
        scratch_shapes=[pltpu.VMEM((l, dh), F32), pltpu.VMEM((l, dh), BF16), pltpu.VMEM((l, dh), BF16),
                        pltpu.VMEM((l // GLA_CHUNK, 8, LANES), F32)],
        compiler_params=_cparams(("arbitrary", "arbitrary")),
        name="hgrn2_mixer",
    )(proj3, proj3, proj3, proj3, proj3, lb_logits, norm_gain.reshape(1, HG_WIDTH), mf, mb, lv)


@functools.lru_cache(maxsize=None)
def _dft_tables(l):
    n = 2 * l
    k2 = 2 * np.arange(l, dtype=np.int64)[:, None] + 1
    m1 = np.arange(l, dtype=np.int64)[None, :]
    ang_s = ((k2 * (2 * m1 + 1)) % (4 * n)).astype(np.float64) * (2.0 * math.pi / (4 * n))
    ang_f = ((k2 * m1) % (2 * n)).astype(np.float64) * (2.0 * math.pi / (2 * n))
    sym = np.stack([np.cos(ang_s), np.sin(ang_s)]).astype(np.float32)
    flt = np.stack([np.cos(ang_f), np.sin(ang_f)]).astype(np.float32)
    return sym, flt


def _filter_features(l):
    pos = jnp.arange(l, dtype=F32)
    t = pos / max(l - 1, 1)
    bands = jnp.linspace(1e-4, HY_BANDS - 1, HY_BANDS, dtype=F32)
    ang = (2.0 * math.pi / l) * pos[:, None] * bands[None, :]
    z = jnp.concatenate([t[:, None], jnp.cos(ang), -jnp.sin(ang)], axis=-1)
    z = jnp.pad(z, ((0, 0), (0, LANES - HY_EMB)))
    min_decay = math.log(HY_DECAY_TARGET) / HY_FAST_DECAY_PCT
    max_decay = math.log(HY_DECAY_TARGET) / HY_SLOW_DECAY_PCT
    deltas = jnp.abs(jnp.linspace(min_decay, max_decay, HY_WIDTH, dtype=F32))
    window = jnp.exp(-t[:, None] * deltas[None, :])
    return z, window


def _filter_kernel(z_ref, win_ref, w1_ref, b1_ref, w2_ref, b2_ref, w3_ref, b3_ref, w4_ref, fr_ref,
                   sum_ref, diff_ref):
    def split(a):
        hi = a.astype(BF16)
        return hi, (a - hi.astype(F32)).astype(BF16)

    def dot3(a, b):
        a_hi, a_lo = split(a)
        b_hi, b_lo = split(b)
        return (jnp.dot(a_hi, b_hi, preferred_element_type=F32) + jnp.dot(a_hi, b_lo, preferred_element_type=F32)
                + jnp.dot(a_lo, b_hi, preferred_element_type=F32))

    fr = fr_ref[...]
    h = jnp.sin(fr * (dot3(z_ref[...], w1_ref[...]) + b1_ref[...]))
    h = jnp.sin(fr * (dot3(h, w2_ref[...]) + b2_ref[...]))
    h = jnp.sin(fr * (dot3(h, w3_ref[...]) + b3_ref[...]))
    h = dot3(h, w4_ref[...])
    win = win_ref[...]
    tl = h.shape[0]
    lag = pl.program_id(0) * tl + lax.broadcasted_iota(jnp.int32, (tl, HY_WIDTH), 0)
    for o in range(HY_ORDER):
        hf = h[:, (2 * o) * HY_WIDTH:(2 * o + 1) * HY_WIDTH] * win
        hb = h[:, (2 * o + 1) * HY_WIDTH:(2 * o + 2) * HY_WIDTH] * win
        hb = jnp.where(lag == 0, 0.0, hb)
        sum_ref[:, o * HY_WIDTH:(o + 1) * HY_WIDTH] = (hf + hb).astype(sum_ref.dtype)
        diff_ref[:, o * HY_WIDTH:(o + 1) * HY_WIDTH] = (hf - hb).astype(diff_ref.dtype)


def hyena_filter_taps(l, w1, b1, w2, b2, w3, b3, w4, freq, tl=256):
    z, window = _filter_features(l)
    w1p = jnp.pad(w1, ((0, LANES - HY_EMB), (0, 0)))
    hid = HY_FILTER_HIDDEN
    full = lambda shape: pl.BlockSpec(shape, lambda i: (0,) * len(shape))
    nout = HY_ORDER * HY_WIDTH
    return pl.pallas_call(
        _filter_kernel,
        grid=(l // tl,),
        in_specs=[pl.BlockSpec((tl, LANES), lambda i: (i, 0)),
                  pl.BlockSpec((tl, HY_WIDTH), lambda i: (i, 0)),
                  full((LANES, hid)), full((1, hid)), full((hid, hid)), full((1, hid)),
                  full((hid, hid)), full((1, hid)), full((hid, 2 * nout)), full((1, hid))],
        out_specs=[pl.BlockSpec((tl, nout), lambda i: (i, 0)), pl.BlockSpec((tl, nout), lambda i: (i, 0))],
        out_shape=[jax.ShapeDtypeStruct((l, nout), BF16), jax.ShapeDtypeStruct((l, nout), BF16)],
        compiler_params=_cparams(("arbitrary",)),
        name="hyena_filter_taps",
    )(z, window, w1p, b1.reshape(1, hid), w2, b2.reshape(1, hid), w3, b3.reshape(1, hid), w4,
      freq.reshape(1, hid))


def _spectrum_kernel(f_ref, h_ref, o_ref):
    o_ref[...] = jnp.dot(f_ref[...], h_ref[...], preferred_element_type=F32).astype(o_ref.dtype)


def hyena_filter_spectra(flt, taps, tn=512):
    _, l, nout = taps.shape
    return pl.pallas_call(
        _spectrum_kernel,
        grid=(2, nout // tn),
        in_specs=[pl.BlockSpec((None, l, l), lambda p, j: (p, 0, 0)),
                  pl.BlockSpec((None, l, tn), lambda p, j: (p, 0, j))],
        out_specs=pl.BlockSpec((None, l, tn), lambda p, j: (p, 0, j)),
        out_shape=jax.ShapeDtypeStruct((2, l, nout), BF16),
        compiler_params=_cparams(("arbitrary", "arbitrary")),
        name="hyena_filter_spectra",
    )(flt, taps)


def _hyena_kernel(xv_ref, x1_ref, x2_ref, cw_ref, cb_ref, sym_ref, sp0_ref, sp1_ref, bias_ref, o_ref,
                  z_s, g_s, zb_s, pa_s, pb_s, *, rb):
    l, tc = xv_ref.shape
    scale = 2.0 / (2 * l)
    first = lax.broadcasted_iota(jnp.int32, (l, tc), 0) == 0
    last = lax.broadcasted_iota(jnp.int32, (l, tc), 0) == l - 1

    def short_conv(x_ref, part):
        x = x_ref[...]
        w = cw_ref[part]
        prev = jnp.where(first, 0.0, pltpu.roll(x, 1, 0))
        nxt = jnp.where(last, 0.0, pltpu.roll(x, l - 1, 0))
        return prev * w[0:1, :] + x * w[1:2, :] + nxt * w[2:3, :] + cb_ref[part]

    z_s[...] = short_conv(xv_ref, 0)
    for o, (gate_ref, part, sp_ref) in enumerate(((x1_ref, 1, sp0_ref), (x2_ref, 2, sp1_ref))):
        zb_s[...] = z_s[...].astype(BF16)
        g_s[...] = short_conv(gate_ref, part)
        for r0 in range(0, l, rb):
            a = jnp.dot(sym_ref[0, r0:r0 + rb, :], zb_s[...], preferred_element_type=F32)
            b = jnp.dot(sym_ref[1, r0:r0 + rb, :], zb_s[...], preferred_element_type=F32)
            hr = sp_ref[0, r0:r0 + rb, :].astype(F32)
            hs = sp_ref[1, r0:r0 + rb, :].astype(F32)
            pa_s[r0:r0 + rb, :] = (a * hr - b * hs).astype(BF16)
            pb_s[r0:r0 + rb, :] = (a * hs + b * hr).astype(BF16)
        for r0 in range(0, l, rb):
            y = (jnp.dot(sym_ref[0, r0:r0 + rb, :], pa_s[...], preferred_element_type=F32)
                 + jnp.dot(sym_ref[1, r0:r0 + rb, :], pb_s[...], preferred_element_type=F32)) * scale
            y = y + z_s[r0:r0 + rb, :] * bias_ref[o]
            z_s[r0:r0 + rb, :] = g_s[r0:r0 + rb, :] * y
    o_ref[...] = z_s[...].astype(o_ref.dtype)


def hyena_mixer(proj3, conv_w, conv_b, sym, spectra, hy_bias, tc=256, rb=1024):
    b, l, _ = proj3.shape
    nct = HY_WIDTH // tc
    base = (5 * HG_WIDTH) // tc

    def col(part):
        return pl.BlockSpec((None, l, tc), lambda ci, bi, part=part: (bi, 0, base + part * nct + ci))

    def spec_cols(o):
        return pl.BlockSpec((2, l, tc), lambda ci, bi, o=o: (0, 0, o * nct + ci))

    cw = conv_w.reshape(3, HY_ORDER + 1, HY_WIDTH).transpose(1, 0, 2)
    cb = conv_b.reshape(HY_ORDER + 1, 1, HY_WIDTH)
    return pl.pallas_call(
        functools.partial(_hyena_kernel, rb=rb),
        grid=(nct, b),
        in_specs=[col(0), col(1), col(2),
                  pl.BlockSpec((HY_ORDER + 1, 3, tc), lambda ci, bi: (0, 0, ci)),
                  pl.BlockSpec((HY_ORDER + 1, 1, tc), lambda ci, bi: (0, 0, ci)),
                  pl.BlockSpec((2, l, l), lambda ci, bi: (0, 0, 0), pipeline_mode=pl.Buffered(1)),
                  spec_cols(0), spec_cols(1),
                  pl.BlockSpec((HY_ORDER, 1, tc), lambda ci, bi: (0, 0, ci))],
        out_specs=pl.BlockSpec((None, l, tc), lambda ci, bi: (bi, 0, ci)),
        out_shape=jax.ShapeDtypeStruct((b, l, HY_WIDTH), BF16),
        scratch_shapes=[pltpu.VMEM((l, tc), F32), pltpu.VMEM((l, tc), F32), pltpu.VMEM((l, tc), BF16),
                        pltpu.VMEM((l, tc), BF16), pltpu.VMEM((l, tc), BF16)],
        compiler_params=_cparams(("arbitrary", "arbitrary")),
        name="hyena_mixer",
    )(proj3, proj3, proj3, cw, cb, sym, spectra, spectra, hy_bias.reshape(HY_ORDER, 1, HY_WIDTH))


def _outproj_router_kernel(hg_ref, hy_ref, x_ref, w_ref, hyg_ref, fg_ref, wr_ref, br_ref, tri_ref,
                           x2_ref, xn_ref, meta_ref, metat_ref, cnt_ref, carry_ref):
    @pl.when(pl.program_id(0) == 0)
    def _():
        carry_ref[...] = jnp.zeros_like(carry_ref)

    hy = hy_ref[...].astype(F32)
    hy = hy * lax.rsqrt(jnp.mean(hy * hy, axis=-1, keepdims=True) + NORM_EPS) * hyg_ref[...]
    m = (jnp.dot(hg_ref[...], w_ref[0:HG_WIDTH, :].astype(BF16), preferred_element_type=F32)
         + jnp.dot(hy.astype(BF16), w_ref[HG_WIDTH:, :].astype(BF16), preferred_element_type=F32))
    x2 = x_ref[...] + m
    x2_ref[...] = x2
    xn = x2 * lax.rsqrt(jnp.mean(x2 * x2, axis=-1, keepdims=True) + NORM_EPS) * fg_ref[...]
    xn_ref[...] = xn
    xn_hi = xn.astype(BF16)
    xn_lo = (xn - xn_hi.astype(F32)).astype(BF16)
    wr = wr_ref[...]
    wr_hi = wr.astype(BF16)
    wr_lo = (wr - wr_hi.astype(F32)).astype(BF16)
    logits = (jnp.dot(xn_hi, wr_hi, preferred_element_type=F32)
              + jnp.dot(xn_hi, wr_lo, preferred_element_type=F32)
              + jnp.dot(xn_lo, wr_hi, preferred_element_type=F32)) + br_ref[...]
    tm = logits.shape[0]
    lane = lax.broadcasted_iota(jnp.int32, (tm, LANES), 1).astype(F32)
    neg = jnp.float32(-jnp.inf)
    logits = jnp.where(lane < N_EXPERTS, logits, neg)
    tops, idxs = [], []
    for _ in range(TOP_K):
        mx = jnp.max(logits, axis=-1, keepdims=True)
        idx = jnp.min(jnp.where(logits == mx, lane, float(LANES)), axis=-1, keepdims=True)
        tops.append(mx)
        idxs.append(idx)
        logits = jnp.where(lane == idx, neg, logits)
    exps = [jnp.exp(t - tops[0]) for t in tops]
    denom = exps[0] + exps[1] + exps[2] + exps[3]
    onehot = jnp.zeros((tm, LANES), F32)
    for idx in idxs:
        onehot = onehot + jnp.where(lane == idx, 1.0, 0.0)
    cum = jnp.dot(tri_ref[...], onehot.astype(BF16), preferred_element_type=F32) + carry_ref[0:1, :]
    meta = jnp.zeros((tm, LANES), F32)
    for k in range(TOP_K):
        rank = jnp.sum(jnp.where(lane == idxs[k], cum, 0.0), axis=-1, keepdims=True)
        meta = jnp.where(lane == k, idxs[k], meta)
        meta = jnp.where(lane == TOP_K + k, rank, meta)
        meta = jnp.where(lane == 2 * TOP_K + k, exps[k] / denom, meta)
    meta_ref[...] = meta
    metat_ref[...] = meta.T[0:metat_ref.shape[0], :]
    carry = carry_ref[...] + jnp.sum(onehot, axis=0, keepdims=True)
    carry_ref[...] = carry
    cnt_ref[...] = carry


def outproj_router(hg2d, hy2d, x2d, w_out, hy_gain, ffn_gain, w_router, b_router, tm=512):
    n, d = x2d.shape
    wr = jnp.pad(w_router, ((0, 0), (0, LANES - N_EXPERTS)))
    br = jnp.pad(b_router, (0, LANES - N_EXPERTS)).reshape(1, LANES)
    tri = jnp.asarray(np.tril(np.ones((tm, tm), np.float32), -1), BF16)
    row = lambda w: pl.BlockSpec((tm, w), lambda i: (i, 0))
    full = lambda shape, **kw: pl.BlockSpec(shape, lambda i: (0,) * len(shape), **kw)
    return pl.pallas_call(
        _outproj_router_kernel,
        grid=(n // tm,),
        in_specs=[row(HG_WIDTH), row(HY_WIDTH), row(d),
                  full((d, d), pipeline_mode=pl.Buffered(1)),
                  full((1, HY_WIDTH)), full((1, d)), full((d, LANES)), full((1, LANES)), full((tm, tm))],
        out_specs=[row(d), row(d), row(LANES), pl.BlockSpec((2 * TOP_K, tm), lambda i: (0, i)),
                   full((8, LANES))],
        out_shape=[jax.ShapeDtypeStruct((n, d), F32), jax.ShapeDtypeStruct((n, d), F32),
                   jax.ShapeDtypeStruct((n, LANES), F32), jax.ShapeDtypeStruct((2 * TOP_K, n), F32),
                   jax.ShapeDtypeStruct((8, LANES), F32)],
        scratch_shapes=[pltpu.VMEM((8, LANES), F32)],
        compiler_params=_cparams(("arbitrary",)),
        name="outproj_router",
    )(hg2d, hy2d, x2d, w_out, hy_gain.reshape(1, HY_WIDTH), ffn_gain.reshape(1, d), wr, br, tri)


GATHER_UNROLL = 8


def _row_gather_kernel(dest_ref, rows_ref, used_ref, x_hbm, o_ref, slot_s, buf_ref, sems, *, n_tok):
    t = pl.program_id(0)
    tr = buf_ref.shape[1]
    used = used_ref[0]

    def n_copies(tile):
        return ((rows_ref[tile] + GATHER_UNROLL - 1) // GATHER_UNROLL) * GATHER_UNROLL

    def source_row(tile, r):
        return slot_s[tile * tr + jnp.minimum(r, rows_ref[tile] - 1)]

    def issue(tile):
        nc = n_copies(tile)
        for b in range(2):
            @pl.when(tile % 2 == b)
            def _(b=b):
                for g in range(tr // GATHER_UNROLL):
                    @pl.when(g * GATHER_UNROLL < nc)
                    def _(g=g):
                        for u in range(GATHER_UNROLL):
                            r = g * GATHER_UNROLL + u
                            pltpu.make_async_copy(x_hbm.at[pl.ds(source_row(tile, r), 1)],
                                                  buf_ref.at[b, pl.ds(r, 1)], sems.at[b]).start()

    @pl.when(t == 0)
    def _():
        def scatter(a, carry):
            tok = (a & (n_tok - 1)) if n_tok & (n_tok - 1) == 0 else lax.rem(a, jnp.int32(n_tok))
            slot_s[dest_ref[a]] = tok
            return carry

        lax.fori_loop(0, dest_ref.shape[0], scatter, 0, unroll=16)
        buf_ref[...] = jnp.zeros(buf_ref.shape, buf_ref.dtype)

        def body(r, carry):
            pltpu.make_async_copy(x_hbm.at[pl.ds(source_row(0, r), 1)], buf_ref.at[0, pl.ds(r, 1)],
                                  sems.at[0]).start()
            return carry

        lax.fori_loop(0, n_copies(0), body, 0)

    @pl.when(t + 1 < used)
    def _():
        issue(t + 1)

    @pl.when(t < used)
    def _():
        b = t % 2
        nc = n_copies(t)

        @pl.when(nc > 0)
        def _():
            pltpu.make_async_copy(x_hbm.at[pl.ds(0, nc)], buf_ref.at[b, pl.ds(0, nc)], sems.at[b]).wait()

        row = lax.broadcasted_iota(jnp.int32, o_ref.shape, 0)
        o_ref[...] = jnp.where(row < rows_ref[t], buf_ref[b], 0.0).astype(o_ref.dtype)


def gather_rows(dest_flat, tile_rows, used_tiles, xn, n_slots, tr):
    n, d = xn.shape
    grid_spec = pltpu.PrefetchScalarGridSpec(
        num_scalar_prefetch=3,
        grid=(n_slots // tr,),
        in_specs=[pl.BlockSpec(memory_space=pl.ANY)],
        out_specs=pl.BlockSpec((tr, d), lambda t, dest, rows, used: (jnp.minimum(t, used[0] - 1), 0)),
        scratch_shapes=[pltpu.SMEM((n_slots,), jnp.int32), pltpu.VMEM((2, tr, d), F32),
                        pltpu.SemaphoreType.DMA((2,))],
    )
    return pl.pallas_call(
        functools.partial(_row_gather_kernel, n_tok=n),
        grid_spec=grid_spec,
        out_shape=jax.ShapeDtypeStruct((n_slots, d), BF16),
        compiler_params=pltpu.CompilerParams(dimension_semantics=("arbitrary",), vmem_limit_bytes=VMEM_LIMIT,
                                             disable_bounds_checks=True),
        name="moe_gather_rows",
    )(dest_flat, tile_rows, used_tiles, xn)


def _expert_kernel(te1_ref, tv1_ref, tb1_ref, te2_ref, tv2_ref, tb2_ref,
                   x_ref, wg_ref, wl_ref, bg_ref, bl_ref, w2_ref, b2_ref, o_ref, act_s, *, nf, blocks):
    jj = pl.program_id(0)
    s = pl.program_id(1)
    tf = wg_ref.shape[1]
    valid_in = tv1_ref[jj]
    valid_out = tv2_ref[jj]
    buf_in = jj % 2
    buf_out = 1 - buf_in

    @pl.when(valid_in > 0)
    def _():
        for r0, rn in blocks:
            @pl.when(r0 < valid_in)
            def _(r0=r0, rn=rn):
                x = x_ref[r0:r0 + rn, :]
                glu = jnp.dot(x, wg_ref[...].astype(BF16), preferred_element_type=F32) + bg_ref[...]
                lin = jnp.dot(x, wl_ref[...].astype(BF16), preferred_element_type=F32) + bl_ref[...]
                glu = jnp.minimum(glu, SWIGLU_LIMIT)
                lin = jnp.clip(lin, -SWIGLU_LIMIT, SWIGLU_LIMIT)
                act = glu * jax.nn.sigmoid(SWIGLU_ALPHA * glu) * (lin + 1.0)
                act_s[buf_in, s, r0:r0 + rn, :] = act.astype(BF16)

    @pl.when(valid_out > 0)
    def _():
        for r0, rn in blocks:
            @pl.when(r0 < valid_out)
            def _(r0=r0, rn=rn):
                y = b2_ref[...] + jnp.dot(act_s[buf_out, 0, r0:r0 + rn, :], w2_ref[0:tf, :].astype(BF16),
                                          preferred_element_type=F32)
                for f in range(1, nf):
                    y = y + jnp.dot(act_s[buf_out, f, r0:r0 + rn, :],
                                    w2_ref[f * tf:(f + 1) * tf, :].astype(BF16),
                                    preferred_element_type=F32)
                o_ref[r0:r0 + rn, :] = y

            @pl.when(r0 >= valid_out)
            def _(r0=r0, rn=rn):
                o_ref[r0:r0 + rn, :] = jnp.zeros((rn, o_ref.shape[1]), o_ref.dtype)


def expert_ffn(tile_expert, tile_valid, tile_block, xs, w1, b1, w2, b2, tm, blocks, tf=512):
    n_slots, d = xs.shape
    n_tiles = n_slots // tm
    nf = D_FF // tf
    assert d // tf == nf
    b1r = b1.reshape(N_EXPERTS, 1, 2 * D_FF)
    b2r = b2.reshape(N_EXPERTS, 1, d)
    zero = jnp.zeros((1,), jnp.int32)
    te1 = jnp.concatenate([tile_expert, tile_expert[-1:]])
    tv1 = jnp.concatenate([tile_valid, zero])
    tb1 = jnp.concatenate([tile_block, tile_block[-1:]])
    te2 = jnp.concatenate([tile_expert[:1], tile_expert])
    tv2 = jnp.concatenate([zero, tile_valid])
    tb2 = jnp.concatenate([tile_block[:1], tile_block])

    def blk1(jj, s, tv1):
        return jnp.where(tv1[jj] > 0, s, nf - 1)

    def blk2(jj, s, tv2):
        return jnp.where(tv2[jj] > 0, s, jnp.where(jj == 0, 0, nf - 1))

    grid_spec = pltpu.PrefetchScalarGridSpec(
        num_scalar_prefetch=6,
        grid=(n_tiles + 1, nf),
        in_specs=[
            pl.BlockSpec((tm, d), lambda jj, s, te1, tv1, tb1, te2, tv2, tb2: (tb1[jj], 0)),
            pl.BlockSpec((None, d, tf),
                         lambda jj, s, te1, tv1, tb1, te2, tv2, tb2: (te1[jj], 0, blk1(jj, s, tv1))),
            pl.BlockSpec((None, d, tf),
                         lambda jj, s, te1, tv1, tb1, te2, tv2, tb2: (te1[jj], 0, nf + blk1(jj, s, tv1))),
            pl.BlockSpec((None, 1, tf),
                         lambda jj, s, te1, tv1, tb1, te2, tv2, tb2: (te1[jj], 0, blk1(jj, s, tv1))),
            pl.BlockSpec((None, 1, tf),
                         lambda jj, s, te1, tv1, tb1, te2, tv2, tb2: (te1[jj], 0, nf + blk1(jj, s, tv1))),
            pl.BlockSpec((None, D_FF, tf),
                         lambda jj, s, te1, tv1, tb1, te2, tv2, tb2: (te2[jj], 0, blk2(jj, s, tv2))),
            pl.BlockSpec((None, 1, tf),
                         lambda jj, s, te1, tv1, tb1, te2, tv2, tb2: (te2[jj], 0, blk2(jj, s, tv2))),
        ],
        out_specs=pl.BlockSpec((tm, tf),
                               lambda jj, s, te1, tv1, tb1, te2, tv2, tb2: (tb2[jj], blk2(jj, s, tv2))),
        scratch_shapes=[pltpu.VMEM((2, nf, tm, tf), BF16)],
    )
    return pl.pallas_call(
        functools.partial(_expert_kernel, nf=nf, blocks=blocks),
        grid_spec=grid_spec,
        out_shape=jax.ShapeDtypeStruct((n_slots, d), F32),
        compiler_params=_cparams(("arbitrary", "arbitrary")),
        name="moe_expert_ffn",
    )(te1, tv1, tb1, te2, tv2, tb2, xs, w1, w1, b1r, b1r, w2, b2r)


def _combine_kernel(dest_ref, ys_hbm, x2_ref, meta_ref, gain_ref, o_ref, buf_ref, sems):
    t = pl.program_id(0)
    nt = pl.num_programs(0)
    tt = x2_ref.shape[0]
    n_tok = dest_ref.shape[0] // TOP_K

    def issue_rolled(tile, b):
        base = tile * tt

        def body(r, carry):
            for k in range(TOP_K):
                pltpu.make_async_copy(ys_hbm.at[pl.ds(dest_ref[k * n_tok + base + r], 1)],
                                      buf_ref.at[b, k, pl.ds(r, 1)], sems.at[b]).start()
            return carry

        lax.fori_loop(0, tt, body, 0, unroll=4)

    def issue(tile):
        base = tile * tt
        for b in range(2):
            @pl.when(tile % 2 == b)
            def _(b=b):
                for r in range(tt):
                    for k in range(TOP_K):
                        pltpu.make_async_copy(ys_hbm.at[pl.ds(dest_ref[k * n_tok + base + r], 1)],
                                              buf_ref.at[b, k, pl.ds(r, 1)], sems.at[b]).start()

    @pl.when(t == 0)
    def _():
        issue_rolled(t, 0)

    @pl.when(t + 1 < nt)
    def _():
        issue(t + 1)

    b = t % 2
    for k in range(TOP_K):
        pltpu.make_async_copy(ys_hbm.at[pl.ds(0, tt)], buf_ref.at[b, k], sems.at[b]).wait()
    meta = meta_ref[...]
    y = x2_ref[...]
    for k in range(TOP_K):
        y = y + meta[:, 2 * TOP_K + k:2 * TOP_K + k + 1] * buf_ref[b, k]
    o_ref[...] = y * lax.rsqrt(jnp.mean(y * y, axis=-1, keepdims=True) + NORM_EPS) * gain_ref[...]


def combine_final(dest_flat, ys, x2, meta, final_gain, tt=128):
    n, d = x2.shape
    grid_spec = pltpu.PrefetchScalarGridSpec(
        num_scalar_prefetch=1,
        grid=(n // tt,),
        in_specs=[pl.BlockSpec(memory_space=pl.ANY),
                  pl.BlockSpec((tt, d), lambda t, dest: (t, 0)),
                  pl.BlockSpec((tt, LANES), lambda t, dest: (t, 0)),
                  pl.BlockSpec((1, d), lambda t, dest: (0, 0))],
        out_specs=pl.BlockSpec((tt, d), lambda t, dest: (t, 0)),
        scratch_shapes=[pltpu.VMEM((2, TOP_K, tt, d), F32), pltpu.SemaphoreType.DMA((2,))],
    )
    return pl.pallas_call(
        _combine_kernel,
        grid_spec=grid_spec,
        out_shape=jax.ShapeDtypeStruct((n, d), F32),
        compiler_params=pltpu.CompilerParams(dimension_semantics=("arbitrary",), vmem_limit_bytes=VMEM_LIMIT,
                                             disable_bounds_checks=True),
        name="moe_combine_final",
    )(dest_flat, ys, x2, meta, final_gain.reshape(1, d))


MOE_TILE = 1152
MOE_ROW_BLOCKS = ((0, 1024), (1024, 128))
MOE_GATHER_TILE = 384


def moe_ffn_final(x2, xn, meta, meta_t, counts, w1, b1, w2, b2, final_gain, tm=MOE_TILE):
    n, d = x2.shape
    a = n * TOP_K
    n_tiles = -(-a // tm) + N_EXPERTS
    n_slots = n_tiles * tm
    idx = meta_t[0:TOP_K].astype(jnp.int32)
    rank = meta_t[TOP_K:2 * TOP_K].astype(jnp.int32)
    cnt = counts[0, :N_EXPERTS].astype(jnp.int32)
    tiles_e = (cnt + tm - 1) // tm
    tile_end = jnp.cumsum(tiles_e)
    tile_start = tile_end - tiles_e
    used = tile_end[-1]
    experts = jnp.arange(N_EXPERTS, dtype=jnp.int32)[:, None, None]
    first_slot = jnp.sum(jnp.where(idx[None] == experts, (tile_start * tm)[:, None, None], 0), axis=0)
    dest = first_slot + rank
    dest_flat = dest.reshape(a)
    tj = jnp.arange(n_tiles, dtype=jnp.int32)
    te = jnp.minimum(jnp.sum(tj[:, None] >= tile_end[None, :], axis=1), N_EXPERTS - 1).astype(jnp.int32)
    last_e = te[jnp.maximum(used - 1, 0)]
    tile_expert = jnp.where(tj < used, te, last_e).astype(jnp.int32)
    tile_valid = jnp.where(tj < used, jnp.clip(cnt[te] - (tj - tile_start[te]) * tm, 0, tm), 0).astype(jnp.int32)
    tile_block = jnp.maximum(jnp.minimum(tj, used - 1), 0).astype(jnp.int32)

    tr = MOE_GATHER_TILE
    per = tm // tr
    gj = jnp.arange(n_tiles * per, dtype=jnp.int32)
    gather_rows_valid = jnp.clip(tile_valid[gj // per] - (gj % per) * tr, 0, tr).astype(jnp.int32)
    used_gather = jnp.maximum(used * per, 1).astype(jnp.int32).reshape(1)
    xs = gather_rows(dest_flat, gather_rows_valid, used_gather, xn, n_slots, tr)
    ys = expert_ffn(tile_expert, tile_valid, tile_block, xs, w1, b1, w2, b2, tm, MOE_ROW_BLOCKS)
    return combine_final(dest_flat, ys, x2, meta, final_gain)


def kernel(x, norm_mix_gain, w_in, hgrn_lb_logits, hgrn_norm_gain, hy_conv_w, hy_conv_b, hy_filt_w1,
           hy_filt_b1, hy_filt_w2, hy_filt_b2, hy_filt_w3, hy_filt_b3, hy_filt_w4, hy_filt_freq, hy_bias,
           hy_norm_gain, w_out, norm_ffn_gain, w_router, b_router, moe_w1, moe_b1, moe_w2, moe_b2,
           final_norm_gain):
    b, l, d = x.shape
    n = b * l
    x2d = x.reshape(n, d)
    proj = in_projection(x2d, norm_mix_gain[0], w_in[0])
    proj3 = proj.reshape(b, l, D_IN)
    hg = hgrn2_mixer(proj3, hgrn_lb_logits, hgrn_norm_gain[0])

    sym, flt = (jnp.asarray(tab).astype(BF16) for tab in _dft_tables(l))
    tap_sum, tap_diff = hyena_filter_taps(l, hy_filt_w1[0], hy_filt_b1[0], hy_filt_w2[0], hy_filt_b2[0],
                                          hy_filt_w3[0], hy_filt_b3[0], hy_filt_w4[0], hy_filt_freq[0])
    spectra = hyena_filter_spectra(flt, jnp.stack([tap_sum, tap_diff]))
    hy = hyena_mixer(proj3, hy_conv_w[0], hy_conv_b[0], sym, spectra, hy_bias[0])

    x2, xn, meta, meta_t, counts = outproj_router(hg.reshape(n, HG_WIDTH), hy.reshape(n, HY_WIDTH), x2d,
                                                  w_out[0], hy_norm_gain[0], norm_ffn_gain[0],
                                                  w_router[0], b_router[0])
    out = moe_ffn_final(x2, xn, meta, meta_t, counts, moe_w1[0], moe_b1[0], moe_w2[0], moe_b2[0],
                        final_norm_gain)
    return out.reshape(b, l, d)
```

```python
import functools
import math

import numpy as np
import jax
import jax.numpy as jnp
from jax import lax
from jax.experimental import pallas as pl
from jax.experimental.pallas import tpu as pltpu

F32 = jnp.float32
BF16 = jnp.bfloat16

D_MODEL = 2048
HG_WIDTH = 1024
HG_HEAD_DIM = 128
HG_HEADS = HG_WIDTH // HG_HEAD_DIM
HY_WIDTH = 1024
HY_ORDER = 2
HY_EMB = 33
HY_BANDS = (HY_EMB - 1) // 2
HY_FILTER_HIDDEN = 64
HY_FAST_DECAY_PCT = 0.3
HY_SLOW_DECAY_PCT = 1.5
HY_DECAY_TARGET = 1e-2
N_EXPERTS = 32
TOP_K = 4
D_FF = D_MODEL
SWIGLU_ALPHA = 1.702
SWIGLU_LIMIT = 7.0
NORM_EPS = 1e-5
D_IN = 5 * HG_WIDTH + (HY_ORDER + 1) * HY_WIDTH

LANES = 128
GLA_CHUNK = 128
GLA_LEVELS = (64, 32, 16, 8, 4, 2, 1)
GLA_MATRIX_LEVELS = (4, 2)
VMEM_LIMIT = 56 * 1024 * 1024


def _cparams(sem, vmem=VMEM_LIMIT):
    return pltpu.CompilerParams(dimension_semantics=sem, vmem_limit_bytes=vmem)


def _inproj_kernel(x_ref, g_ref, w_ref, o_ref, h_s):
    @pl.when(pl.program_id(1) == 0)
    def _():
        x = x_ref[...]
        ms = jnp.mean(x * x, axis=-1, keepdims=True)
        h_s[...] = (x * lax.rsqrt(ms + NORM_EPS) * g_ref[...]).astype(BF16)

    o_ref[...] = jnp.dot(h_s[...], w_ref[...].astype(BF16), preferred_element_type=F32)


def in_projection(x2d, gain, w_in, tm=1024, tn=1024):
    n, d = x2d.shape
    dout = w_in.shape[1]
    return pl.pallas_call(
        _inproj_kernel,
        grid=(n // tm, dout // tn),
        in_specs=[
            pl.BlockSpec((tm, d), lambda i, j: (i, 0)),
            pl.BlockSpec((1, d), lambda i, j: (0, 0)),
            pl.BlockSpec((d, tn), lambda i, j: (0, j)),
        ],
        out_specs=pl.BlockSpec((tm, tn), lambda i, j: (i, j)),
        out_shape=jax.ShapeDtypeStruct((n, dout), F32),
        scratch_shapes=[pltpu.VMEM((tm, d), BF16)],
        compiler_params=_cparams(("arbitrary", "arbitrary")),
        name="in_projection",
    )(x2d, gain.reshape(1, d), w_in)


def _gla_constants():
    c = GLA_CHUNK
    t = np.arange(c)[:, None]
    r = np.arange(c)[None, :]
    fwd = [r <= t]
    bwd = [r >= t]
    for m in GLA_MATRIX_LEVELS:
        pos = t % (2 * m)
        mid = t - pos + m
        second = pos >= m
        fwd.append(np.where(second, (r >= mid) & (r <= t), (r > t) & (r < mid)))
        bwd.append(np.where(second, (r >= mid) & (r < t), (r >= t) & (r < mid)))
    x = t ^ r
    lv = np.full((c, c), -1, np.int32)
    for j in range(int(math.log2(c))):
        lv = np.where((x >> j) == 1, j, lv)
    mf = jnp.asarray(np.concatenate(fwd, 0).astype(np.float32), BF16)
    mb = jnp.asarray(np.concatenate(bwd, 0).astype(np.float32), BF16)
    return mf, mb, jnp.asarray(lv, jnp.int32)


def _hgrn_kernel(q_ref, ff_ref, fb_ref, i_ref, g_ref, lb_ref, gain_ref, mf_ref, mb_ref, lv_ref,
                 o_ref, acc_ref, qb_ref, kb_ref, db_ref):
    c = GLA_CHUNK
    n_chunks = q_ref.shape[0] // c
    lb = lb_ref[...]
    l0, l1 = lb[0], lb[1]
    mx = jnp.maximum(l0, l1)
    e0 = jnp.exp(l0 - mx)
    e1 = jnp.exp(l1 - mx)
    p0 = e0 / (e0 + e1)
    lb_f = p0[0:1, :]
    lb_b = p0[1:2, :]
    lv = lv_ref[...]
    row = lax.broadcasted_iota(jnp.int32, (c, LANES), 0)
    nt = (((1,), (1,)), ((), ()))

    def gates(z, lower):
        f = lower + (1.0 - lower) * jax.nn.sigmoid(z)
        return f, 1.0 - f, jnp.log(f)

    def boundary_diff(cum, m, offset):
        nb = c // (2 * m)
        c3 = cum.reshape(nb, 2 * m, LANES)
        ref = jnp.broadcast_to(c3[:, offset:offset + 1, :], (nb, 2 * m, LANES))
        return (c3 - ref).reshape(c, LANES)

    def exponents(m_ref, g):
        g_hi = g.astype(BF16)
        g_lo = (g - g_hi.astype(F32)).astype(BF16)
        m = m_ref[...]
        return (jnp.dot(m, g_hi, preferred_element_type=F32)
                + jnp.dot(m, g_lo, preferred_element_type=F32))

    def fwd_body(ci, st):
        rows = pl.ds(pl.multiple_of(ci * c, c), c)
        qr = q_ref[rows, :]
        q = qr * jax.nn.sigmoid(qr)
        v = i_ref[rows, :]
        ff, kf, gf = gates(ff_ref[rows, :], lb_f)
        fb, kb, gb = gates(fb_ref[rows, :], lb_b)
        ef_all = exponents(mf_ref, gf)
        eb_all = exponents(mb_ref, gb)
        b_inc = ef_all[0:c]
        bb = eb_all[0:c]
        scores = jnp.zeros((c, c), F32)
        for m in GLA_LEVELS:
            second = (row & m) != 0
            k_sel = jnp.where(second, kb, kf)
            if m == 1:
                a = q * jnp.where(second, ff, fb)
                b = k_sel
            else:
                if m in GLA_MATRIX_LEVELS:
                    li = 1 + GLA_MATRIX_LEVELS.index(m)
                    ef = ef_all[li * c:(li + 1) * c]
                    eb = eb_all[li * c:(li + 1) * c]
                    e_query = jnp.where(second, ef, eb)
                    e_key = jnp.where(second, eb, ef)
                else:
                    df = boundary_diff(b_inc, m, m - 1)
                    db = boundary_diff(bb, m, m)
                    e_query = jnp.where(second, df, db)
                    e_key = -jnp.where(second, db, df)
                a = q * jnp.exp(e_query)
                b = k_sel * jnp.exp(e_key)
            s = lax.dot_general(a.astype(BF16), b.astype(BF16), nt, preferred_element_type=F32)
            scores = jnp.where(lv == int(math.log2(m)), s, scores)
        v_bf = v.astype(BF16)
        o = jnp.dot(scores.astype(BF16), v_bf, preferred_element_type=F32)
        o = o + jnp.sum(q * (kf + kb), axis=-1, keepdims=True) * v
        q_dec = (q * jnp.exp(b_inc)).astype(BF16)
        o = o + lax.dot_general(q_dec, st.astype(BF16), nt, preferred_element_type=F32)
        k_dec = (kf * jnp.exp(b_inc[c - 1:c, :] - b_inc)).astype(BF16)
        vt_bf = v.T.astype(BF16)
        st = st * jnp.exp(b_inc[c - 1:c, :]) + jnp.dot(vt_bf, k_dec, preferred_element_type=F32)
        acc_ref[rows, :] = o
        qb_ref[rows, :] = (q * jnp.exp(bb)).astype(BF16)
        kb_ref[rows, :] = (kb * jnp.exp(bb[0:1, :] - bb)).astype(BF16)
        db_ref[ci] = jnp.broadcast_to(jnp.exp(bb[0:1, :]), (8, LANES))
        return st

    lax.fori_loop(0, n_chunks, fwd_body, jnp.zeros((c, c), F32), unroll=4)

    gain = gain_ref[...]

    def bwd_body(i, st):
        ci = n_chunks - 1 - i
        rows = pl.ds(pl.multiple_of(ci * c, c), c)
        o = acc_ref[rows, :] + lax.dot_general(qb_ref[rows, :], st.astype(BF16), nt,
                                               preferred_element_type=F32)
        vt_bf = i_ref[rows, :].T.astype(BF16)
        st = st * db_ref[ci][0:1, :] + jnp.dot(vt_bf, kb_ref[rows, :], preferred_element_type=F32)
        o = o * lax.rsqrt(jnp.mean(o * o, axis=-1, keepdims=True) + NORM_EPS) * gain
        gr = g_ref[rows, :]
        o_ref[rows, :] = (o * (gr * jax.nn.sigmoid(gr))).astype(o_ref.dtype)
        return st

    lax.fori_loop(0, n_chunks, bwd_body, jnp.zeros((c, c), F32), unroll=4)


def hgrn2_mixer(proj3, lb_logits, norm_gain):
    b, l, _ = proj3.shape
    h, dh = HG_HEADS, HG_HEAD_DIM
    mf, mb, lv = _gla_constants()
    nblk = mf.shape[0]

    def col(off):
        return pl.BlockSpec((None, l, dh), lambda bi, hi, off=off: (bi, 0, off + hi))

    const2 = lambda bi, hi: (0, 0)
    return pl.pallas_call(
        _hgrn_kernel,
        grid=(b, h),
        in_specs=[col(0), col(h), col(2 * h), col(3 * h), col(4 * h),
                  pl.BlockSpec((2, 2, dh), lambda bi, hi: (0, 0, hi)),
                  pl.BlockSpec((1, dh), lambda bi, hi: (0, hi)),
                  pl.BlockSpec((nblk, GLA_CHUNK), const2),
                  pl.BlockSpec((nblk, GLA_CHUNK), const2),
                  pl.BlockSpec((GLA_CHUNK, GLA_CHUNK), const2)],
        out_specs=pl.BlockSpec((None, l, dh), lambda bi, hi: (bi, 0, hi)),
        out_shape=jax.ShapeDtypeStruct((b, l, HG_WIDTH), BF16),
        scratch_shapes=[pltpu.VMEM((l, dh), F32), pltpu.VMEM((l, dh), BF16), pltpu.VMEM((l, dh), BF16),
                        pltpu.VMEM((l // GLA_CHUNK, 8, LANES), F32)],
        compiler_params=_cparams(("arbitrary", "arbitrary")),
        name="hgrn2_mixer",
    )(proj3, proj3, proj3, proj3, proj3, lb_logits, norm_gain.reshape(1, HG_WIDTH), mf, mb, lv)


@functools.lru_cache(maxsize=None)
def _dft_tables(l):
    n = 2 * l
    k2 = 2 * np.arange(l, dtype=np.int64)[:, None] + 1
    m1 = np.arange(l, dtype=np.int64)[None, :]
    ang_s = ((k2 * (2 * m1 + 1)) % (4 * n)).astype(np.float64) * (2.0 * math.pi / (4 * n))
    ang_f = ((k2 * m1) % (2 * n)).astype(np.float64) * (2.0 * math.pi / (2 * n))
    sym = np.stack([np.cos(ang_s), np.sin(ang_s)]).astype(np.float32)
    flt = np.stack([np.cos(ang_f), np.sin(ang_f)]).astype(np.float32)
    return sym, flt


def _filter_features(l):
    pos = jnp.arange(l, dtype=F32)
    t = pos / max(l - 1, 1)
    bands = jnp.linspace(1e-4, HY_BANDS - 1, HY_BANDS, dtype=F32)
    ang = (2.0 * math.pi / l) * pos[:, None] * bands[None, :]
    z = jnp.concatenate([t[:, None], jnp.cos(ang), -jnp.sin(ang)], axis=-1)
    z = jnp.pad(z, ((0, 0), (0, LANES - HY_EMB)))
    min_decay = math.log(HY_DECAY_TARGET) / HY_FAST_DECAY_PCT
    max_decay = math.log(HY_DECAY_TARGET) / HY_SLOW_DECAY_PCT
    deltas = jnp.abs(jnp.linspace(min_decay, max_decay, HY_WIDTH, dtype=F32))
    window = jnp.exp(-t[:, None] * deltas[None, :])
    return z, window


def _filter_kernel(z_ref, win_ref, w1_ref, b1_ref, w2_ref, b2_ref, w3_ref, b3_ref, w4_ref, fr_ref,
                   sum_ref, diff_ref):
    def split(a):
        hi = a.astype(BF16)
        return hi, (a - hi.astype(F32)).astype(BF16)

    def dot3(a, b):
        a_hi, a_lo = split(a)
        b_hi, b_lo = split(b)
        return (jnp.dot(a_hi, b_hi, preferred_element_type=F32) + jnp.dot(a_hi, b_lo, preferred_element_type=F32)
                + jnp.dot(a_lo, b_hi, preferred_element_type=F32))

    fr = fr_ref[...]
    h = jnp.sin(fr * (dot3(z_ref[...], w1_ref[...]) + b1_ref[...]))
    h = jnp.sin(fr * (dot3(h, w2_ref[...]) + b2_ref[...]))
    h = jnp.sin(fr * (dot3(h, w3_ref[...]) + b3_ref[...]))
    h = dot3(h, w4_ref[...])
    win = win_ref[...]
    tl = h.shape[0]
    lag = pl.program_id(0) * tl + lax.broadcasted_iota(jnp.int32, (tl, HY_WIDTH), 0)
    for o in range(HY_ORDER):
        hf = h[:, (2 * o) * HY_WIDTH:(2 * o + 1) * HY_WIDTH] * win
        hb = h[:, (2 * o + 1) * HY_WIDTH:(2 * o + 2) * HY_WIDTH] * win
        hb = jnp.where(lag == 0, 0.0, hb)
        sum_ref[:, o * HY_WIDTH:(o + 1) * HY_WIDTH] = (hf + hb).astype(sum_ref.dtype)
        diff_ref[:, o * HY_WIDTH:(o + 1) * HY_WIDTH] = (hf - hb).astype(diff_ref.dtype)


def hyena_filter_taps(l, w1, b1, w2, b2, w3, b3, w4, freq, tl=256):
    z, window = _filter_features(l)
    w1p = jnp.pad(w1, ((0, LANES - HY_EMB), (0, 0)))
    hid = HY_FILTER_HIDDEN
    full = lambda shape: pl.BlockSpec(shape, lambda i: (0,) * len(shape))
    nout = HY_ORDER * HY_WIDTH
    return pl.pallas_call(
        _filter_kernel,
        grid=(l // tl,),
        in_specs=[pl.BlockSpec((tl, LANES), lambda i: (i, 0)),
                  pl.BlockSpec((tl, HY_WIDTH), lambda i: (i, 0)),
                  full((LANES, hid)), full((1, hid)), full((hid, hid)), full((1, hid)),
                  full((hid, hid)), full((1, hid)), full((hid, 2 * nout)), full((1, hid))],
        out_specs=[pl.BlockSpec((tl, nout), lambda i: (i, 0)), pl.BlockSpec((tl, nout), lambda i: (i, 0))],
        out_shape=[jax.ShapeDtypeStruct((l, nout), BF16), jax.ShapeDtypeStruct((l, nout), BF16)],
        compiler_params=_cparams(("arbitrary",)),
        name="hyena_filter_taps",
    )(z, window, w1p, b1.reshape(1, hid), w2, b2.reshape(1, hid), w3, b3.reshape(1, hid), w4,
      freq.reshape(1, hid))


def _spectrum_kernel(f_ref, h_ref, o_ref):
    o_ref[...] = jnp.dot(f_ref[...], h_ref[...], preferred_element_type=F32).astype(o_ref.dtype)


def hyena_filter_spectra(flt, taps, tn=512):
    _, l, nout = taps.shape
    return pl.pallas_call(
        _spectrum_kernel,
        grid=(2, nout // tn),
        in_specs=[pl.BlockSpec((None, l, l), lambda p, j: (p, 0, 0)),
                  pl.BlockSpec((None, l, tn), lambda p, j: (p, 0, j))],
        out_specs=pl.BlockSpec((None, l, tn), lambda p, j: (p, 0, j)),
        out_shape=jax.ShapeDtypeStruct((2, l, nout), BF16),
        compiler_params=_cparams(("arbitrary", "arbitrary")),
        name="hyena_filter_spectra",
    )(flt, taps)


def _hyena_kernel(xv_ref, x1_ref, x2_ref, cw_ref, cb_ref, sym_ref, sp0_ref, sp1_ref, bias_ref, o_ref,
                  z_s, g_s, zb_s, pa_s, pb_s, *, rb):
    l, tc = xv_ref.shape
    scale = 2.0 / (2 * l)
    first = lax.broadcasted_iota(jnp.int32, (l, tc), 0) == 0
    last = lax.broadcasted_iota(jnp.int32, (l, tc), 0) == l - 1

    def short_conv(x_ref, part):
        x = x_ref[...]
        w = cw_ref[part]
        prev = jnp.where(first, 0.0, pltpu.roll(x, 1, 0))
        nxt = jnp.where(last, 0.0, pltpu.roll(x, l - 1, 0))
        return prev * w[0:1, :] + x * w[1:2, :] + nxt * w[2:3, :] + cb_ref[part]

    z_s[...] = short_conv(xv_ref, 0)
    for o, (gate_ref, part, sp_ref) in enumerate(((x1_ref, 1, sp0_ref), (x2_ref, 2, sp1_ref))):
        zb_s[...] = z_s[...].astype(BF16)
        g_s[...] = short_conv(gate_ref, part)
        for r0 in range(0, l, rb):
            a = jnp.dot(sym_ref[0, r0:r0 + rb, :], zb_s[...], preferred_element_type=F32)
            b = jnp.dot(sym_ref[1, r0:r0 + rb, :], zb_s[...], preferred_element_type=F32)
            hr = sp_ref[0, r0:r0 + rb, :].astype(F32)
            hs = sp_ref[1, r0:r0 + rb, :].astype(F32)
            pa_s[r0:r0 + rb, :] = (a * hr - b * hs).astype(BF16)
            pb_s[r0:r0 + rb, :] = (a * hs + b * hr).astype(BF16)
        for r0 in range(0, l, rb):
            y = (jnp.dot(sym_ref[0, r0:r0 + rb, :], pa_s[...], preferred_element_type=F32)
                 + jnp.dot(sym_ref[1, r0:r0 + rb, :], pb_s[...], preferred_element_type=F32)) * scale
            y = y + z_s[r0:r0 + rb, :] * bias_ref[o]
            z_s[r0:r0 + rb, :] = g_s[r0:r0 + rb, :] * y
    o_ref[...] = z_s[...].astype(o_ref.dtype)


def hyena_mixer(proj3, conv_w, conv_b, sym, spectra, hy_bias, tc=256, rb=1024):
    b, l, _ = proj3.shape
    nct = HY_WIDTH // tc
    base = (5 * HG_WIDTH) // tc

    def col(part):
        return pl.BlockSpec((None, l, tc), lambda ci, bi, part=part: (bi, 0, base + part * nct + ci))

    def spec_cols(o):
        return pl.BlockSpec((2, l, tc), lambda ci, bi, o=o: (0, 0, o * nct + ci))

    cw = conv_w.reshape(3, HY_ORDER + 1, HY_WIDTH).transpose(1, 0, 2)
    cb = conv_b.reshape(HY_ORDER + 1, 1, HY_WIDTH)
    return pl.pallas_call(
        functools.partial(_hyena_kernel, rb=rb),
        grid=(nct, b),
        in_specs=[col(0), col(1), col(2),
                  pl.BlockSpec((HY_ORDER + 1, 3, tc), lambda ci, bi: (0, 0, ci)),
                  pl.BlockSpec((HY_ORDER + 1, 1, tc), lambda ci, bi: (0, 0, ci)),
                  pl.BlockSpec((2, l, l), lambda ci, bi: (0, 0, 0), pipeline_mode=pl.Buffered(1)),
                  spec_cols(0), spec_cols(1),
                  pl.BlockSpec((HY_ORDER, 1, tc), lambda ci, bi: (0, 0, ci))],
        out_specs=pl.BlockSpec((None, l, tc), lambda ci, bi: (bi, 0, ci)),
        out_shape=jax.ShapeDtypeStruct((b, l, HY_WIDTH), BF16),
        scratch_shapes=[pltpu.VMEM((l, tc), F32), pltpu.VMEM((l, tc), F32), pltpu.VMEM((l, tc), BF16),
                        pltpu.VMEM((l, tc), BF16), pltpu.VMEM((l, tc), BF16)],
        compiler_params=_cparams(("arbitrary", "arbitrary")),
        name="hyena_mixer",
    )(proj3, proj3, proj3, cw, cb, sym, spectra, spectra, hy_bias.reshape(HY_ORDER, 1, HY_WIDTH))


def _outproj_router_kernel(hg_ref, hy_ref, x_ref, w_ref, hyg_ref, fg_ref, wr_ref, br_ref, tri_ref,
                           x2_ref, xn_ref, meta_ref, metat_ref, cnt_ref, carry_ref):
    @pl.when(pl.program_id(0) == 0)
    def _():
        carry_ref[...] = jnp.zeros_like(carry_ref)

    hy = hy_ref[...].astype(F32)
    hy = hy * lax.rsqrt(jnp.mean(hy * hy, axis=-1, keepdims=True) + NORM_EPS) * hyg_ref[...]
    m = (jnp.dot(hg_ref[...], w_ref[0:HG_WIDTH, :].astype(BF16), preferred_element_type=F32)
         + jnp.dot(hy.astype(BF16), w_ref[HG_WIDTH:, :].astype(BF16), preferred_element_type=F32))
    x2 = x_ref[...] + m
    x2_ref[...] = x2
    xn = x2 * lax.rsqrt(jnp.mean(x2 * x2, axis=-1, keepdims=True) + NORM_EPS) * fg_ref[...]
    xn_ref[...] = xn
    xn_hi = xn.astype(BF16)
    xn_lo = (xn - xn_hi.astype(F32)).astype(BF16)
    wr = wr_ref[...]
    wr_hi = wr.astype(BF16)
    wr_lo = (wr - wr_hi.astype(F32)).astype(BF16)
    logits = (jnp.dot(xn_hi, wr_hi, preferred_element_type=F32)
              + jnp.dot(xn_hi, wr_lo, preferred_element_type=F32)
              + jnp.dot(xn_lo, wr_hi, preferred_element_type=F32)) + br_ref[...]
    tm = logits.shape[0]
    lane = lax.broadcasted_iota(jnp.int32, (tm, LANES), 1).astype(F32)
    neg = jnp.float32(-jnp.inf)
    logits = jnp.where(lane < N_EXPERTS, logits, neg)
    tops, idxs = [], []
    for _ in range(TOP_K):
        mx = jnp.max(logits, axis=-1, keepdims=True)
        idx = jnp.min(jnp.where(logits == mx, lane, float(LANES)), axis=-1, keepdims=True)
        tops.append(mx)
        idxs.append(idx)
        logits = jnp.where(lane == idx, neg, logits)
    exps = [jnp.exp(t - tops[0]) for t in tops]
    denom = exps[0] + exps[1] + exps[2] + exps[3]
    onehot = jnp.zeros((tm, LANES), F32)
    for idx in idxs:
        onehot = onehot + jnp.where(lane == idx, 1.0, 0.0)
    cum = jnp.dot(tri_ref[...], onehot.astype(BF16), preferred_element_type=F32) + carry_ref[0:1, :]
    meta = jnp.zeros((tm, LANES), F32)
    for k in range(TOP_K):
        rank = jnp.sum(jnp.where(lane == idxs[k], cum, 0.0), axis=-1, keepdims=True)
        meta = jnp.where(lane == k, idxs[k], meta)
        meta = jnp.where(lane == TOP_K + k, rank, meta)
        meta = jnp.where(lane == 2 * TOP_K + k, exps[k] / denom, meta)
    meta_ref[...] = meta
    metat_ref[...] = meta.T[0:metat_ref.shape[0], :]
    carry = carry_ref[...] + jnp.sum(onehot, axis=0, keepdims=True)
    carry_ref[...] = carry
    cnt_ref[...] = carry


def outproj_router(hg2d, hy2d, x2d, w_out, hy_gain, ffn_gain, w_router, b_router, tm=512):
    n, d = x2d.shape
    wr = jnp.pad(w_router, ((0, 0), (0, LANES - N_EXPERTS)))
    br = jnp.pad(b_router, (0, LANES - N_EXPERTS)).reshape(1, LANES)
    tri = jnp.asarray(np.tril(np.ones((tm, tm), np.float32), -1), BF16)
    row = lambda w: pl.BlockSpec((tm, w), lambda i: (i, 0))
    full = lambda shape, **kw: pl.BlockSpec(shape, lambda i: (0,) * len(shape), **kw)
    return pl.pallas_call(
        _outproj_router_kernel,
        grid=(n // tm,),
        in_specs=[row(HG_WIDTH), row(HY_WIDTH), row(d),
                  full((d, d), pipeline_mode=pl.Buffered(1)),
                  full((1, HY_WIDTH)), full((1, d)), full((d, LANES)), full((1, LANES)), full((tm, tm))],
        out_specs=[row(d), row(d), row(LANES), pl.BlockSpec((2 * TOP_K, tm), lambda i: (0, i)),
                   full((8, LANES))],
        out_shape=[jax.ShapeDtypeStruct((n, d), F32), jax.ShapeDtypeStruct((n, d), F32),
                   jax.ShapeDtypeStruct((n, LANES), F32), jax.ShapeDtypeStruct((2 * TOP_K, n), F32),
                   jax.ShapeDtypeStruct((8, LANES), F32)],
        scratch_shapes=[pltpu.VMEM((8, LANES), F32)],
        compiler_params=_cparams(("arbitrary",)),
        name="outproj_router",
    )(hg2d, hy2d, x2d, w_out, hy_gain.reshape(1, HY_WIDTH), ffn_gain.reshape(1, d), wr, br, tri)


GATHER_UNROLL = 8


def _row_gather_kernel(dest_ref, rows_ref, used_ref, x_hbm, o_ref, slot_s, buf_ref, sems, *, n_tok):
    t = pl.program_id(0)
    tr = buf_ref.shape[1]
    used = used_ref[0]

    def n_copies(tile):
        return ((rows_ref[tile] + GATHER_UNROLL - 1) // GATHER_UNROLL) * GATHER_UNROLL

    def source_row(tile, r):
        return slot_s[tile * tr + jnp.minimum(r, rows_ref[tile] - 1)]

    def issue(tile):
        nc = n_copies(tile)
        for b in range(2):
            @pl.when(tile % 2 == b)
            def _(b=b):
                for g in range(tr // GATHER_UNROLL):
                    @pl.when(g * GATHER_UNROLL < nc)
                    def _(g=g):
                        for u in range(GATHER_UNROLL):
                            r = g * GATHER_UNROLL + u
                            pltpu.make_async_copy(x_hbm.at[pl.ds(source_row(tile, r), 1)],
                                                  buf_ref.at[b, pl.ds(r, 1)], sems.at[b]).start()

    @pl.when(t == 0)
    def _():
        def scatter(a, carry):
            tok = (a & (n_tok - 1)) if n_tok & (n_tok - 1) == 0 else lax.rem(a, jnp.int32(n_tok))
            slot_s[dest_ref[a]] = tok
            return carry

        lax.fori_loop(0, dest_ref.shape[0], scatter, 0, unroll=16)
        buf_ref[...] = jnp.zeros(buf_ref.shape, buf_ref.dtype)

        def body(r, carry):
            pltpu.make_async_copy(x_hbm.at[pl.ds(source_row(0, r), 1)], buf_ref.at[0, pl.ds(r, 1)],
                                  sems.at[0]).start()
            return carry

        lax.fori_loop(0, n_copies(0), body, 0)

    @pl.when(t + 1 < used)
    def _():
        issue(t + 1)

    @pl.when(t < used)
    def _():
        b = t % 2
        nc = n_copies(t)

        @pl.when(nc > 0)
        def _():
            pltpu.make_async_copy(x_hbm.at[pl.ds(0, nc)], buf_ref.at[b, pl.ds(0, nc)], sems.at[b]).wait()

        o_ref[...] = buf_ref[b].astype(o_ref.dtype)


def gather_rows(dest_flat, tile_rows, used_tiles, xn, n_slots, tr):
    n, d = xn.shape
    grid_spec = pltpu.PrefetchScalarGridSpec(
        num_scalar_prefetch=3,
        grid=(n_slots // tr,),
        in_specs=[pl.BlockSpec(memory_space=pl.ANY)],
        out_specs=pl.BlockSpec((tr, d), lambda t, dest, rows, used: (jnp.minimum(t, used[0] - 1), 0)),
        scratch_shapes=[pltpu.SMEM((n_slots,), jnp.int32), pltpu.VMEM((2, tr, d), F32),
                        pltpu.SemaphoreType.DMA((2,))],
    )
    return pl.pallas_call(
        functools.partial(_row_gather_kernel, n_tok=n),
        grid_spec=grid_spec,
        out_shape=jax.ShapeDtypeStruct((n_slots, d), BF16),
        compiler_params=pltpu.CompilerParams(dimension_semantics=("arbitrary",), vmem_limit_bytes=VMEM_LIMIT,
                                             disable_bounds_checks=True),
        name="moe_gather_rows",
    )(dest_flat, tile_rows, used_tiles, xn)


def _expert_kernel(te1_ref, tv1_ref, tb1_ref, te2_ref, tv2_ref, tb2_ref,
                   x_ref, wg_ref, wl_ref, bg_ref, bl_ref, w2_ref, b2_ref, o_ref, act_s, *, nf, row_counts):
    jj = pl.program_id(0)
    s = pl.program_id(1)
    tf = wg_ref.shape[1]
    valid_in = tv1_ref[jj]
    valid_out = tv2_ref[jj]
    buf_in = jj % 2
    buf_out = 1 - buf_in

    tm = x_ref.shape[0]
    bounds = tuple(zip((0,) + tuple(row_counts[:-1]), row_counts))

    for lo, rn in bounds:
        @pl.when(jnp.logical_and(valid_in > lo, valid_in <= rn))
        def _(rn=rn):
            x = x_ref[0:rn, :]
            glu = jnp.dot(x, wg_ref[...].astype(BF16), preferred_element_type=F32) + bg_ref[...]
            lin = jnp.dot(x, wl_ref[...].astype(BF16), preferred_element_type=F32) + bl_ref[...]
            glu = jnp.minimum(glu, SWIGLU_LIMIT)
            lin = jnp.clip(lin, -SWIGLU_LIMIT, SWIGLU_LIMIT)
            act = glu * jax.nn.sigmoid(SWIGLU_ALPHA * glu) * (lin + 1.0)
            act_s[buf_in, s, 0:rn, :] = act.astype(BF16)

    for lo, rn in bounds:
        @pl.when(jnp.logical_and(valid_out > lo, valid_out <= rn))
        def _(rn=rn):
            y = b2_ref[...] + jnp.dot(act_s[buf_out, 0, 0:rn, :], w2_ref[0:tf, :].astype(BF16),
                                      preferred_element_type=F32)
            for f in range(1, nf):
                y = y + jnp.dot(act_s[buf_out, f, 0:rn, :], w2_ref[f * tf:(f + 1) * tf, :].astype(BF16),
                                preferred_element_type=F32)
            o_ref[0:rn, :] = y
            if rn < tm:
                o_ref[rn:tm, :] = jnp.zeros((tm - rn, o_ref.shape[1]), o_ref.dtype)


def expert_ffn(tile_expert, tile_valid, tile_block, xs, w1, b1, w2, b2, tm, row_counts, tf=512):
    n_slots, d = xs.shape
    n_tiles = n_slots // tm
    nf = D_FF // tf
    assert d // tf == nf and row_counts[-1] == tm
    b1r = b1.reshape(N_EXPERTS, 1, 2 * D_FF)
    b2r = b2.reshape(N_EXPERTS, 1, d)
    zero = jnp.zeros((1,), jnp.int32)
    te1 = jnp.concatenate([tile_expert, tile_expert[-1:]])
    tv1 = jnp.concatenate([tile_valid, zero])
    tb1 = jnp.concatenate([tile_block, tile_block[-1:]])
    te2 = jnp.concatenate([tile_expert[:1], tile_expert])
    tv2 = jnp.concatenate([zero, tile_valid])
    tb2 = jnp.concatenate([tile_block[:1], tile_block])

    def blk1(jj, s, tv1):
        return jnp.where(tv1[jj] > 0, s, nf - 1)

    def blk2(jj, s, tv2):
        return jnp.where(tv2[jj] > 0, s, jnp.where(jj == 0, 0, nf - 1))

    grid_spec = pltpu.PrefetchScalarGridSpec(
        num_scalar_prefetch=6,
        grid=(n_tiles + 1, nf),
        in_specs=[
            pl.BlockSpec((tm, d), lambda jj, s, te1, tv1, tb1, te2, tv2, tb2: (tb1[jj], 0)),
            pl.BlockSpec((None, d, tf),
                         lambda jj, s, te1, tv1, tb1, te2, tv2, tb2: (te1[jj], 0, blk1(jj, s, tv1))),
            pl.BlockSpec((None, d, tf),
                         lambda jj, s, te1, tv1, tb1, te2, tv2, tb2: (te1[jj], 0, nf + blk1(jj, s, tv1))),
            pl.BlockSpec((None, 1, tf),
                         lambda jj, s, te1, tv1, tb1, te2, tv2, tb2: (te1[jj], 0, blk1(jj, s, tv1))),
            pl.BlockSpec((None, 1, tf),
                         lambda jj, s, te1, tv1, tb1, te2, tv2, tb2: (te1[jj], 0, nf + blk1(jj, s, tv1))),
            pl.BlockSpec((None, D_FF, tf),
                         lambda jj, s, te1, tv1, tb1, te2, tv2, tb2: (te2[jj], 0, blk2(jj, s, tv2))),
            pl.BlockSpec((None, 1, tf),
                         lambda jj, s, te1, tv1, tb1, te2, tv2, tb2: (te2[jj], 0, blk2(jj, s, tv2))),
        ],
        out_specs=pl.BlockSpec((tm, tf),
                               lambda jj, s, te1, tv1, tb1, te2, tv2, tb2: (tb2[jj], blk2(jj, s, tv2))),
        scratch_shapes=[pltpu.VMEM((2, nf, tm, tf), BF16)],
    )
    return pl.pallas_call(
        functools.partial(_expert_kernel, nf=nf, row_counts=row_counts),
        grid_spec=grid_spec,
        out_shape=jax.ShapeDtypeStruct((n_slots, d), F32),
        compiler_params=_cparams(("arbitrary", "arbitrary")),
        name="moe_expert_ffn",
    )(te1, tv1, tb1, te2, tv2, tb2, xs, w1, w1, b1r, b1r, w2, b2r)


def _combine_kernel(dest_ref, ys_hbm, x2_ref, meta_ref, gain_ref, o_ref, buf_ref, sems):
    t = pl.program_id(0)
    nt = pl.num_programs(0)
    tt = x2_ref.shape[0]
    n_tok = dest_ref.shape[0] // TOP_K

    def issue_rolled(tile, b):
        base = tile * tt

        def body(r, carry):
            for k in range(TOP_K):
                pltpu.make_async_copy(ys_hbm.at[pl.ds(dest_ref[k * n_tok + base + r], 1)],
                                      buf_ref.at[b, k, pl.ds(r, 1)], sems.at[b]).start()
            return carry

        lax.fori_loop(0, tt, body, 0, unroll=4)

    def issue(tile):
        base = tile * tt
        for b in range(2):
            @pl.when(tile % 2 == b)
            def _(b=b):
                for r in range(tt):
                    for k in range(TOP_K):
                        pltpu.make_async_copy(ys_hbm.at[pl.ds(dest_ref[k * n_tok + base + r], 1)],
                                              buf_ref.at[b, k, pl.ds(r, 1)], sems.at[b]).start()

    @pl.when(t == 0)
    def _():
        issue_rolled(t, 0)

    @pl.when(t + 1 < nt)
    def _():
        issue(t + 1)

    b = t % 2
    for k in range(TOP_K):
        pltpu.make_async_copy(ys_hbm.at[pl.ds(0, tt)], buf_ref.at[b, k], sems.at[b]).wait()
    meta = meta_ref[...]
    y = x2_ref[...]
    for k in range(TOP_K):
        y = y + meta[:, 2 * TOP_K + k:2 * TOP_K + k + 1] * buf_ref[b, k]
    o_ref[...] = y * lax.rsqrt(jnp.mean(y * y, axis=-1, keepdims=True) + NORM_EPS) * gain_ref[...]


def combine_final(dest_flat, ys, x2, meta, final_gain, tt=128):
    n, d = x2.shape
    grid_spec = pltpu.PrefetchScalarGridSpec(
        num_scalar_prefetch=1,
        grid=(n // tt,),
        in_specs=[pl.BlockSpec(memory_space=pl.ANY),
                  pl.BlockSpec((tt, d), lambda t, dest: (t, 0)),
                  pl.BlockSpec((tt, LANES), lambda t, dest: (t, 0)),
                  pl.BlockSpec((1, d), lambda t, dest: (0, 0))],
        out_specs=pl.BlockSpec((tt, d), lambda t, dest: (t, 0)),
        scratch_shapes=[pltpu.VMEM((2, TOP_K, tt, d), F32), pltpu.SemaphoreType.DMA((2,))],
    )
    return pl.pallas_call(
        _combine_kernel,
        grid_spec=grid_spec,
        out_shape=jax.ShapeDtypeStruct((n, d), F32),
        compiler_params=pltpu.CompilerParams(dimension_semantics=("arbitrary",), vmem_limit_bytes=VMEM_LIMIT,
                                             disable_bounds_checks=True),
        name="moe_combine_final",
    )(dest_flat, ys, x2, meta, final_gain.reshape(1, d))


MOE_TILE = 1152
MOE_ROW_COUNTS = (1024, 1152)
MOE_GATHER_TILE = 384


def moe_ffn_final(x2, xn, meta, meta_t, counts, w1, b1, w2, b2, final_gain, tm=MOE_TILE):
    n, d = x2.shape
    a = n * TOP_K
    n_tiles = -(-a // tm) + N_EXPERTS
    n_slots = n_tiles * tm
    idx = meta_t[0:TOP_K].astype(jnp.int32)
    rank = meta_t[TOP_K:2 * TOP_K].astype(jnp.int32)
    cnt = counts[0, :N_EXPERTS].astype(jnp.int32)
    tiles_e = (cnt + tm - 1) // tm
    tile_end = jnp.cumsum(tiles_e)
    tile_start = tile_end - tiles_e
    used = tile_end[-1]
    experts = jnp.arange(N_EXPERTS, dtype=jnp.int32)[:, None, None]
    first_slot = jnp.sum(jnp.where(idx[None] == experts, (tile_start * tm)[:, None, None], 0), axis=0)
    dest = first_slot + rank
    dest_flat = dest.reshape(a)
    tj = jnp.arange(n_tiles, dtype=jnp.int32)
    te = jnp.minimum(jnp.sum(tj[:, None] >= tile_end[None, :], axis=1), N_EXPERTS - 1).astype(jnp.int32)
    last_e = te[jnp.maximum(used - 1, 0)]
    tile_expert = jnp.where(tj < used, te, last_e).astype(jnp.int32)
    tile_valid = jnp.where(tj < used, jnp.clip(cnt[te] - (tj - tile_start[te]) * tm, 0, tm), 0).astype(jnp.int32)
    tile_block = jnp.maximum(jnp.minimum(tj, used - 1), 0).astype(jnp.int32)

    tr = MOE_GATHER_TILE
    per = tm // tr
    gj = jnp.arange(n_tiles * per, dtype=jnp.int32)
    gather_rows_valid = jnp.clip(tile_valid[gj // per] - (gj % per) * tr, 0, tr).astype(jnp.int32)
    used_gather = jnp.maximum(used * per, 1).astype(jnp.int32).reshape(1)
    xs = gather_rows(dest_flat, gather_rows_valid, used_gather, xn, n_slots, tr)
    ys = expert_ffn(tile_expert, tile_valid, tile_block, xs, w1, b1, w2, b2, tm, MOE_ROW_COUNTS)
    return combine_final(dest_flat, ys, x2, meta, final_gain)


def kernel(x, norm_mix_gain, w_in, hgrn_lb_logits, hgrn_norm_gain, hy_conv_w, hy_conv_b, hy_filt_w1,
           hy_filt_b1, hy_filt_w2, hy_filt_b2, hy_filt_w3, hy_filt_b3, hy_filt_w4, hy_filt_freq, hy_bias,
           hy_norm_gain, w_out, norm_ffn_gain, w_router, b_router, moe_w1, moe_b1, moe_w2, moe_b2,
           final_norm_gain):
    b, l, d = x.shape
    n = b * l
    x2d = x.reshape(n, d)
    proj = in_projection(x2d, norm_mix_gain[0], w_in[0])
    proj3 = proj.reshape(b, l, D_IN)
    hg = hgrn2_mixer(proj3, hgrn_lb_logits, hgrn_norm_gain[0])

    sym, flt = (jnp.asarray(tab).astype(BF16) for tab in _dft_tables(l))
    tap_sum, tap_diff = hyena_filter_taps(l, hy_filt_w1[0], hy_filt_b1[0], hy_filt_w2[0], hy_filt_b2[0],
                                          hy_filt_w3[0], hy_filt_b3[0], hy_filt_w4[0], hy_filt_freq[0])
    spectra = hyena_filter_spectra(flt, jnp.stack([tap_sum, tap_diff]))
    hy = hyena_mixer(proj3, hy_conv_w[0], hy_conv_b[0], sym, spectra, hy_bias[0])

    x2, xn, meta, meta_t, counts = outproj_router(hg.reshape(n, HG_WIDTH), hy.reshape(n, HY_WIDTH), x2d,
                                                  w_out[0], hy_norm_gain[0], norm_ffn_gain[0],
                                                  w_router[0], b_router[0])
    out = moe_ffn_final(x2, xn, meta, meta_t, counts, moe_w1[0], moe_b1[0], moe_w2[0], moe_b2[0],
                        final_norm_gain)
    return out.reshape(b, l, d)
```

```python
import functools
import math

import numpy as np
import jax
import jax.numpy as jnp
from jax import lax
from jax.experimental import pallas as pl
from jax.experimental.pallas import tpu as pltpu

F32 = jnp.float32
BF16 = jnp.bfloat16

D_MODEL = 2048
HG_WIDTH = 1024
HG_HEAD_DIM = 128
HG_HEADS = HG_WIDTH // HG_HEAD_DIM
HY_WIDTH = 1024
HY_ORDER = 2
HY_EMB = 33
HY_BANDS = (HY_EMB - 1) // 2
HY_FILTER_HIDDEN = 64
HY_FAST_DECAY_PCT = 0.3
HY_SLOW_DECAY_PCT = 1.5
HY_DECAY_TARGET = 1e-2
N_EXPERTS = 32
TOP_K = 4
D_FF = D_MODEL
SWIGLU_ALPHA = 1.702
SWIGLU_LIMIT = 7.0
NORM_EPS = 1e-5
D_IN = 5 * HG_WIDTH + (HY_ORDER + 1) * HY_WIDTH

LANES = 128
GLA_CHUNK = 128
GLA_LEVELS = (64, 32, 16, 8, 4, 2, 1)
GLA_MATRIX_LEVELS = (4, 2)
VMEM_LIMIT = 56 * 1024 * 1024


def _cparams(sem, vmem=VMEM_LIMIT):
    return pltpu.CompilerParams(dimension_semantics=sem, vmem_limit_bytes=vmem)


def _inproj_kernel(x_ref, g_ref, w_ref, o_ref, h_s):
    @pl.when(pl.program_id(1) == 0)
    def _():
        x = x_ref[...]
        ms = jnp.mean(x * x, axis=-1, keepdims=True)
        h_s[...] = (x * lax.rsqrt(ms + NORM_EPS) * g_ref[...]).astype(BF16)

    o_ref[...] = jnp.dot(h_s[...], w_ref[...].astype(BF16), preferred_element_type=F32)


def in_projection(x2d, gain, w_in, tm=1024, tn=1024):
    n, d = x2d.shape
    dout = w_in.shape[1]
    return pl.pallas_call(
        _inproj_kernel,
        grid=(n // tm, dout // tn),
        in_specs=[
            pl.BlockSpec((tm, d), lambda i, j: (i, 0)),
            pl.BlockSpec((1, d), lambda i, j: (0, 0)),
            pl.BlockSpec((d, tn), lambda i, j: (0, j)),
        ],
        out_specs=pl.BlockSpec((tm, tn), lambda i, j: (i, j)),
        out_shape=jax.ShapeDtypeStruct((n, dout), F32),
        scratch_shapes=[pltpu.VMEM((tm, d), BF16)],
        compiler_params=_cparams(("arbitrary", "arbitrary")),
        name="in_projection",
    )(x2d, gain.reshape(1, d), w_in)


def _gla_constants():
    c = GLA_CHUNK
    t = np.arange(c)[:, None]
    r = np.arange(c)[None, :]
    fwd = [r <= t]
    bwd = [r >= t]
    for m in GLA_MATRIX_LEVELS:
        pos = t % (2 * m)
        mid = t - pos + m
        second = pos >= m
        fwd.append(np.where(second, (r >= mid) & (r <= t), (r > t) & (r < mid)))
        bwd.append(np.where(second, (r >= mid) & (r < t), (r >= t) & (r < mid)))
    x = t ^ r
    lv = np.full((c, c), -1, np.int32)
    for j in range(int(math.log2(c))):
        lv = np.where((x >> j) == 1, j, lv)
    mf = jnp.asarray(np.concatenate(fwd, 0).astype(np.float32), BF16)
    mb = jnp.asarray(np.concatenate(bwd, 0).astype(np.float32), BF16)
    return mf, mb, jnp.asarray(lv, jnp.int32)


def _hgrn_kernel(q_ref, ff_ref, fb_ref, i_ref, g_ref, lb_ref, gain_ref, mf_ref, mb_ref, lv_ref,
                 o_ref, acc_ref, qb_ref, kb_ref, db_ref):
    c = GLA_CHUNK
    n_chunks = q_ref.shape[0] // c
    lb = lb_ref[...]
    l0, l1 = lb[0], lb[1]
    mx = jnp.maximum(l0, l1)
    e0 = jnp.exp(l0 - mx)
    e1 = jnp.exp(l1 - mx)
    p0 = e0 / (e0 + e1)
    lb_f = p0[0:1, :]
    lb_b = p0[1:2, :]
    lv = lv_ref[...]
    row = lax.broadcasted_iota(jnp.int32, (c, LANES), 0)
    nt = (((1,), (1,)), ((), ()))

    def gates(z, lower):
        f = lower + (1.0 - lower) * jax.nn.sigmoid(z)
        return f, 1.0 - f, jnp.log(f)

    def boundary_diff(cum, m, offset):
        nb = c // (2 * m)
        c3 = cum.reshape(nb, 2 * m, LANES)
        ref = jnp.broadcast_to(c3[:, offset:offset + 1, :], (nb, 2 * m, LANES))
        return (c3 - ref).reshape(c, LANES)

    def exponents(m_ref, g):
        g_hi = g.astype(BF16)
        g_lo = (g - g_hi.astype(F32)).astype(BF16)
        m = m_ref[...]
        return (jnp.dot(m, g_hi, preferred_element_type=F32)
                + jnp.dot(m, g_lo, preferred_element_type=F32))

    def fwd_body(ci, st):
        rows = pl.ds(pl.multiple_of(ci * c, c), c)
        qr = q_ref[rows, :]
        q = qr * jax.nn.sigmoid(qr)
        v = i_ref[rows, :]
        ff, kf, gf = gates(ff_ref[rows, :], lb_f)
        fb, kb, gb = gates(fb_ref[rows, :], lb_b)
        ef_all = exponents(mf_ref, gf)
        eb_all = exponents(mb_ref, gb)
        b_inc = ef_all[0:c]
        bb = eb_all[0:c]
        scores = jnp.zeros((c, c), F32)
        for m in GLA_LEVELS:
            second = (row & m) != 0
            k_sel = jnp.where(second, kb, kf)
            if m == 1:
                a = q * jnp.where(second, ff, fb)
                b = k_sel
            else:
                if m in GLA_MATRIX_LEVELS:
                    li = 1 + GLA_MATRIX_LEVELS.index(m)
                    ef = ef_all[li * c:(li + 1) * c]
                    eb = eb_all[li * c:(li + 1) * c]
                    e_query = jnp.where(second, ef, eb)
                    e_key = jnp.where(second, eb, ef)
                else:
                    df = boundary_diff(b_inc, m, m - 1)
                    db = boundary_diff(bb, m, m)
                    e_query = jnp.where(second, df, db)
                    e_key = -jnp.where(second, db, df)
                a = q * jnp.exp(e_query)
                b = k_sel * jnp.exp(e_key)
            s = lax.dot_general(a.astype(BF16), b.astype(BF16), nt, preferred_element_type=F32)
            scores = jnp.where(lv == int(math.log2(m)), s, scores)
        v_bf = v.astype(BF16)
        o = jnp.dot(scores.astype(BF16), v_bf, preferred_element_type=F32)
        o = o + jnp.sum(q * (kf + kb), axis=-1, keepdims=True) * v
        q_dec = (q * jnp.exp(b_inc)).astype(BF16)
        o = o + lax.dot_general(q_dec, st.astype(BF16), nt, preferred_element_type=F32)
        k_dec = (kf * jnp.exp(b_inc[c - 1:c, :] - b_inc)).astype(BF16)
        vt_bf = v.T.astype(BF16)
        st = st * jnp.exp(b_inc[c - 1:c, :]) + jnp.dot(vt_bf, k_dec, preferred_element_type=F32)
        acc_ref[rows, :] = o
        qb_ref[rows, :] = (q * jnp.exp(bb)).astype(BF16)
        kb_ref[rows, :] = (kb * jnp.exp(bb[0:1, :] - bb)).astype(BF16)
        db_ref[ci] = jnp.broadcast_to(jnp.exp(bb[0:1, :]), (8, LANES))
        return st

    lax.fori_loop(0, n_chunks, fwd_body, jnp.zeros((c, c), F32), unroll=4)

    gain = gain_ref[...]

    def bwd_body(i, st):
        ci = n_chunks - 1 - i
        rows = pl.ds(pl.multiple_of(ci * c, c), c)
        o = acc_ref[rows, :] + lax.dot_general(qb_ref[rows, :], st.astype(BF16), nt,
                                               preferred_element_type=F32)
        vt_bf = i_ref[rows, :].T.astype(BF16)
        st = st * db_ref[ci][0:1, :] + jnp.dot(vt_bf, kb_ref[rows, :], preferred_element_type=F32)
        o = o * lax.rsqrt(jnp.mean(o * o, axis=-1, keepdims=True) + NORM_EPS) * gain
        gr = g_ref[rows, :]
        o_ref[rows, :] = (o * (gr * jax.nn.sigmoid(gr))).astype(o_ref.dtype)
        return st

    lax.fori_loop(0, n_chunks, bwd_body, jnp.zeros((c, c), F32), unroll=4)


def hgrn2_mixer(proj3, lb_logits, norm_gain):
    b, l, _ = proj3.shape
    h, dh = HG_HEADS, HG_HEAD_DIM
    mf, mb, lv = _gla_constants()
    nblk = mf.shape[0]

    def col(off):
        return pl.BlockSpec((None, l, dh), lambda bi, hi, off=off: (bi, 0, off + hi))

    const2 = lambda bi, hi: (0, 0)
    return pl.pallas_call(
        _hgrn_kernel,
        grid=(b, h),
        in_specs=[col(0), col(h), col(2 * h), col(3 * h), col(4 * h),
                  pl.BlockSpec((2, 2, dh), lambda bi, hi: (0, 0, hi)),
                  pl.BlockSpec((1, dh), lambda bi, hi: (0, hi)),
                  pl.BlockSpec((nblk, GLA_CHUNK), const2),
                  pl.BlockSpec((nblk, GLA_CHUNK), const2),
                  pl.BlockSpec((GLA_CHUNK, GLA_CHUNK), const2)],
        out_specs=pl.BlockSpec((None, l, dh), lambda bi, hi: (bi, 0, hi)),
        out_shape=jax.ShapeDtypeStruct((b, l, HG_WIDTH), BF16),
        scratch_shapes=[pltpu.VMEM((l, dh), F32), pltpu.VMEM((l, dh), BF16), pltpu.VMEM((l, dh), BF16),
                        pltpu.VMEM((l // GLA_CHUNK, 8, LANES), F32)],
        compiler_params=_cparams(("arbitrary", "arbitrary")),
        name="hgrn2_mixer",
    )(proj3, proj3, proj3, proj3, proj3, lb_logits, norm_gain.reshape(1, HG_WIDTH), mf, mb, lv)


@functools.lru_cache(maxsize=None)
def _dft_tables(l):
    n = 2 * l
    k2 = 2 * np.arange(l, dtype=np.int64)[:, None] + 1
    m1 = np.arange(l, dtype=np.int64)[None, :]
    ang_s = ((k2 * (2 * m1 + 1)) % (4 * n)).astype(np.float64) * (2.0 * math.pi / (4 * n))
    ang_f = ((k2 * m1) % (2 * n)).astype(np.float64) * (2.0 * math.pi / (2 * n))
    sym = np.stack([np.cos(ang_s), np.sin(ang_s)]).astype(np.float32)
    flt = np.stack([np.cos(ang_f), np.sin(ang_f)]).astype(np.float32)
    return sym, flt


def _filter_features(l):
    pos = jnp.arange(l, dtype=F32)
    t = pos / max(l - 1, 1)
    bands = jnp.linspace(1e-4, HY_BANDS - 1, HY_BANDS, dtype=F32)
    ang = (2.0 * math.pi / l) * pos[:, None] * bands[None, :]
    z = jnp.concatenate([t[:, None], jnp.cos(ang), -jnp.sin(ang)], axis=-1)
    z = jnp.pad(z, ((0, 0), (0, LANES - HY_EMB)))
    min_decay = math.log(HY_DECAY_TARGET) / HY_FAST_DECAY_PCT
    max_decay = math.log(HY_DECAY_TARGET) / HY_SLOW_DECAY_PCT
    deltas = jnp.abs(jnp.linspace(min_decay, max_decay, HY_WIDTH, dtype=F32))
    window = jnp.exp(-t[:, None] * deltas[None, :])
    return z, window


def _filter_kernel(z_ref, win_ref, w1_ref, b1_ref, w2_ref, b2_ref, w3_ref, b3_ref, w4_ref, fr_ref,
                   sum_ref, diff_ref):
    def split(a):
        hi = a.astype(BF16)
        return hi, (a - hi.astype(F32)).astype(BF16)

    def dot3(a, b):
        a_hi, a_lo = split(a)
        b_hi, b_lo = split(b)
        return (jnp.dot(a_hi, b_hi, preferred_element_type=F32) + jnp.dot(a_hi, b_lo, preferred_element_type=F32)
                + jnp.dot(a_lo, b_hi, preferred_element_type=F32))

    fr = fr_ref[...]
    h = jnp.sin(fr * (dot3(z_ref[...], w1_ref[...]) + b1_ref[...]))
    h = jnp.sin(fr * (dot3(h, w2_ref[...]) + b2_ref[...]))
    h = jnp.sin(fr * (dot3(h, w3_ref[...]) + b3_ref[...]))
    h = dot3(h, w4_ref[...])
    win = win_ref[...]
    tl = h.shape[0]
    lag = pl.program_id(0) * tl + lax.broadcasted_iota(jnp.int32, (tl, HY_WIDTH), 0)
    for o in range(HY_ORDER):
        hf = h[:, (2 * o) * HY_WIDTH:(2 * o + 1) * HY_WIDTH] * win
        hb = h[:, (2 * o + 1) * HY_WIDTH:(2 * o + 2) * HY_WIDTH] * win
        hb = jnp.where(lag == 0, 0.0, hb)
        sum_ref[:, o * HY_WIDTH:(o + 1) * HY_WIDTH] = (hf + hb).astype(sum_ref.dtype)
        diff_ref[:, o * HY_WIDTH:(o + 1) * HY_WIDTH] = (hf - hb).astype(diff_ref.dtype)


def hyena_filter_taps(l, w1, b1, w2, b2, w3, b3, w4, freq, tl=256):
    z, window = _filter_features(l)
    w1p = jnp.pad(w1, ((0, LANES - HY_EMB), (0, 0)))
    hid = HY_FILTER_HIDDEN
    full = lambda shape: pl.BlockSpec(shape, lambda i: (0,) * len(shape))
    nout = HY_ORDER * HY_WIDTH
    return pl.pallas_call(
        _filter_kernel,
        grid=(l // tl,),
        in_specs=[pl.BlockSpec((tl, LANES), lambda i: (i, 0)),
                  pl.BlockSpec((tl, HY_WIDTH), lambda i: (i, 0)),
                  full((LANES, hid)), full((1, hid)), full((hid, hid)), full((1, hid)),
                  full((hid, hid)), full((1, hid)), full((hid, 2 * nout)), full((1, hid))],
        out_specs=[pl.BlockSpec((tl, nout), lambda i: (i, 0)), pl.BlockSpec((tl, nout), lambda i: (i, 0))],
        out_shape=[jax.ShapeDtypeStruct((l, nout), BF16), jax.ShapeDtypeStruct((l, nout), BF16)],
        compiler_params=_cparams(("arbitrary",)),
        name="hyena_filter_taps",
    )(z, window, w1p, b1.reshape(1, hid), w2, b2.reshape(1, hid), w3, b3.reshape(1, hid), w4,
      freq.reshape(1, hid))


def _spectrum_kernel(f_ref, h_ref, o_ref):
    o_ref[...] = jnp.dot(f_ref[...], h_ref[...], preferred_element_type=F32).astype(o_ref.dtype)


def hyena_filter_spectra(flt, taps, tn=512):
    _, l, nout = taps.shape
    return pl.pallas_call(
        _spectrum_kernel,
        grid=(2, nout // tn),
        in_specs=[pl.BlockSpec((None, l, l), lambda p, j: (p, 0, 0)),
                  pl.BlockSpec((None, l, tn), lambda p, j: (p, 0, j))],
        out_specs=pl.BlockSpec((None, l, tn), lambda p, j: (p, 0, j)),
        out_shape=jax.ShapeDtypeStruct((2, l, nout), BF16),
        compiler_params=_cparams(("arbitrary", "arbitrary")),
        name="hyena_filter_spectra",
    )(flt, taps)


def _hyena_kernel(xv_ref, x1_ref, x2_ref, cw_ref, cb_ref, sym_ref, sp0_ref, sp1_ref, bias_ref, o_ref,
                  z_s, g_s, zb_s, pa_s, pb_s, *, rb):
    l, tc = xv_ref.shape
    scale = 2.0 / (2 * l)
    first = lax.broadcasted_iota(jnp.int32, (l, tc), 0) == 0
    last = lax.broadcasted_iota(jnp.int32, (l, tc), 0) == l - 1

    def short_conv(x_ref, part):
        x = x_ref[...]
        w = cw_ref[part]
        prev = jnp.where(first, 0.0, pltpu.roll(x, 1, 0))
        nxt = jnp.where(last, 0.0, pltpu.roll(x, l - 1, 0))
        return prev * w[0:1, :] + x * w[1:2, :] + nxt * w[2:3, :] + cb_ref[part]

    z_s[...] = short_conv(xv_ref, 0)
    for o, (gate_ref, part, sp_ref) in enumerate(((x1_ref, 1, sp0_ref), (x2_ref, 2, sp1_ref))):
        zb_s[...] = z_s[...].astype(BF16)
        g_s[...] = short_conv(gate_ref, part)
        for r0 in range(0, l, rb):
            a = jnp.dot(sym_ref[0, r0:r0 + rb, :], zb_s[...], preferred_element_type=F32)
            b = jnp.dot(sym_ref[1, r0:r0 + rb, :], zb_s[...], preferred_element_type=F32)
            hr = sp_ref[0, r0:r0 + rb, :].astype(F32)
            hs = sp_ref[1, r0:r0 + rb, :].astype(F32)
            pa_s[r0:r0 + rb, :] = (a * hr - b * hs).astype(BF16)
            pb_s[r0:r0 + rb, :] = (a * hs + b * hr).astype(BF16)
        for r0 in range(0, l, rb):
            y = (jnp.dot(sym_ref[0, r0:r0 + rb, :], pa_s[...], preferred_element_type=F32)
                 + jnp.dot(sym_ref[1, r0:r0 + rb, :], pb_s[...], preferred_element_type=F32)) * scale
            y = y + z_s[r0:r0 + rb, :] * bias_ref[o]
            z_s[r0:r0 + rb, :] = g_s[r0:r0 + rb, :] * y
    o_ref[...] = z_s[...].astype(o_ref.dtype)


def hyena_mixer(proj3, conv_w, conv_b, sym, spectra, hy_bias, tc=256, rb=1024):
    b, l, _ = proj3.shape
    nct = HY_WIDTH // tc
    base = (5 * HG_WIDTH) // tc

    def col(part):
        return pl.BlockSpec((None, l, tc), lambda ci, bi, part=part: (bi, 0, base + part * nct + ci))

    def spec_cols(o):
        return pl.BlockSpec((2, l, tc), lambda ci, bi, o=o: (0, 0, o * nct + ci))

    cw = conv_w.reshape(3, HY_ORDER + 1, HY_WIDTH).transpose(1, 0, 2)
    cb = conv_b.reshape(HY_ORDER + 1, 1, HY_WIDTH)
    return pl.pallas_call(
        functools.partial(_hyena_kernel, rb=rb),
        grid=(nct, b),
        in_specs=[col(0), col(1), col(2),
                  pl.BlockSpec((HY_ORDER + 1, 3, tc), lambda ci, bi: (0, 0, ci)),
                  pl.BlockSpec((HY_ORDER + 1, 1, tc), lambda ci, bi: (0, 0, ci)),
                  pl.BlockSpec((2, l, l), lambda ci, bi: (0, 0, 0), pipeline_mode=pl.Buffered(1)),
                  spec_cols(0), spec_cols(1),
                  pl.BlockSpec((HY_ORDER, 1, tc), lambda ci, bi: (0, 0, ci))],
        out_specs=pl.BlockSpec((None, l, tc), lambda ci, bi: (bi, 0, ci)),
        out_shape=jax.ShapeDtypeStruct((b, l, HY_WIDTH), BF16),
        scratch_shapes=[pltpu.VMEM((l, tc), F32), pltpu.VMEM((l, tc), F32), pltpu.VMEM((l, tc), BF16),
                        pltpu.VMEM((l, tc), BF16), pltpu.VMEM((l, tc), BF16)],
        compiler_params=_cparams(("arbitrary", "arbitrary")),
        name="hyena_mixer",
    )(proj3, proj3, proj3, cw, cb, sym, spectra, spectra, hy_bias.reshape(HY_ORDER, 1, HY_WIDTH))


def _outproj_router_kernel(hg_ref, hy_ref, x_ref, w_ref, hyg_ref, fg_ref, wr_ref, br_ref, tri_ref,
                           x2_ref, xn_ref, meta_ref, metat_ref, cnt_ref, carry_ref):
    @pl.when(pl.program_id(0) == 0)
    def _():
        carry_ref[...] = jnp.zeros_like(carry_ref)

    hy = hy_ref[...].astype(F32)
    hy = hy * lax.rsqrt(jnp.mean(hy * hy, axis=-1, keepdims=True) + NORM_EPS) * hyg_ref[...]
    m = (jnp.dot(hg_ref[...], w_ref[0:HG_WIDTH, :].astype(BF16), preferred_element_type=F32)
         + jnp.dot(hy.astype(BF16), w_ref[HG_WIDTH:, :].astype(BF16), preferred_element_type=F32))
    x2 = x_ref[...] + m
    x2_ref[...] = x2
    xn = x2 * lax.rsqrt(jnp.mean(x2 * x2, axis=-1, keepdims=True) + NORM_EPS) * fg_ref[...]
    xn_ref[...] = xn
    xn_hi = xn.astype(BF16)
    xn_lo = (xn - xn_hi.astype(F32)).astype(BF16)
    wr = wr_ref[...]
    wr_hi = wr.astype(BF16)
    wr_lo = (wr - wr_hi.astype(F32)).astype(BF16)
    logits = (jnp.dot(xn_hi, wr_hi, preferred_element_type=F32)
              + jnp.dot(xn_hi, wr_lo, preferred_element_type=F32)
              + jnp.dot(xn_lo, wr_hi, preferred_element_type=F32)) + br_ref[...]
    tm = logits.shape[0]
    lane = lax.broadcasted_iota(jnp.int32, (tm, LANES), 1).astype(F32)
    neg = jnp.float32(-jnp.inf)
    logits = jnp.where(lane < N_EXPERTS, logits, neg)
    tops, idxs = [], []
    for _ in range(TOP_K):
        mx = jnp.max(logits, axis=-1, keepdims=True)
        idx = jnp.min(jnp.where(logits == mx, lane, float(LANES)), axis=-1, keepdims=True)
        tops.append(mx)
        idxs.append(idx)
        logits = jnp.where(lane == idx, neg, logits)
    exps = [jnp.exp(t - tops[0]) for t in tops]
    denom = exps[0] + exps[1] + exps[2] + exps[3]
    onehot = jnp.zeros((tm, LANES), F32)
    for idx in idxs:
        onehot = onehot + jnp.where(lane == idx, 1.0, 0.0)
    cum = jnp.dot(tri_ref[...], onehot.astype(BF16), preferred_element_type=F32) + carry_ref[0:1, :]
    meta = jnp.zeros((tm, LANES), F32)
    for k in range(TOP_K):
        rank = jnp.sum(jnp.where(lane == idxs[k], cum, 0.0), axis=-1, keepdims=True)
        meta = jnp.where(lane == k, idxs[k], meta)
        meta = jnp.where(lane == TOP_K + k, rank, meta)
        meta = jnp.where(lane == 2 * TOP_K + k, exps[k] / denom, meta)
    meta_ref[...] = meta
    metat_ref[...] = meta.T[0:metat_ref.shape[0], :]
    carry = carry_ref[...] + jnp.sum(onehot, axis=0, keepdims=True)
    carry_ref[...] = carry
    cnt_ref[...] = carry


def outproj_router(hg2d, hy2d, x2d, w_out, hy_gain, ffn_gain, w_router, b_router, tm=512):
    n, d = x2d.shape
    wr = jnp.pad(w_router, ((0, 0), (0, LANES - N_EXPERTS)))
    br = jnp.pad(b_router, (0, LANES - N_EXPERTS)).reshape(1, LANES)
    tri = jnp.asarray(np.tril(np.ones((tm, tm), np.float32), -1), BF16)
    row = lambda w: pl.BlockSpec((tm, w), lambda i: (i, 0))
    full = lambda shape, **kw: pl.BlockSpec(shape, lambda i: (0,) * len(shape), **kw)
    return pl.pallas_call(
        _outproj_router_kernel,
        grid=(n // tm,),
        in_specs=[row(HG_WIDTH), row(HY_WIDTH), row(d),
                  full((d, d), pipeline_mode=pl.Buffered(1)),
                  full((1, HY_WIDTH)), full((1, d)), full((d, LANES)), full((1, LANES)), full((tm, tm))],
        out_specs=[row(d), row(d), row(LANES), pl.BlockSpec((2 * TOP_K, tm), lambda i: (0, i)),
                   full((8, LANES))],
        out_shape=[jax.ShapeDtypeStruct((n, d), F32), jax.ShapeDtypeStruct((n, d), F32),
                   jax.ShapeDtypeStruct((n, LANES), F32), jax.ShapeDtypeStruct((2 * TOP_K, n), F32),
                   jax.ShapeDtypeStruct((8, LANES), F32)],
        scratch_shapes=[pltpu.VMEM((8, LANES), F32)],
        compiler_params=_cparams(("arbitrary",)),
        name="outproj_router",
    )(hg2d, hy2d, x2d, w_out, hy_gain.reshape(1, HY_WIDTH), ffn_gain.reshape(1, d), wr, br, tri)


GATHER_UNROLL = 8


def _row_gather_kernel(dest_ref, rows_ref, used_ref, x_hbm, o_ref, slot_s, buf_ref, sems, *, n_tok):
    t = pl.program_id(0)
    tr = buf_ref.shape[1]
    used = used_ref[0]

    def n_copies(tile):
        return ((rows_ref[tile] + GATHER_UNROLL - 1) // GATHER_UNROLL) * GATHER_UNROLL

    def source_row(tile, r):
        return slot_s[tile * tr + jnp.minimum(r, rows_ref[tile] - 1)]

    def issue(tile):
        nc = n_copies(tile)
        for b in range(2):
            @pl.when(tile % 2 == b)
            def _(b=b):
                for g in range(tr // GATHER_UNROLL):
                    @pl.when(g * GATHER_UNROLL < nc)
                    def _(g=g):
                        for u in range(GATHER_UNROLL):
                            r = g * GATHER_UNROLL + u
                            pltpu.make_async_copy(x_hbm.at[pl.ds(source_row(tile, r), 1)],
                                                  buf_ref.at[b, pl.ds(r, 1)], sems.at[b]).start(priority=r % 2)

    @pl.when(t == 0)
    def _():
        def scatter(a, carry):
            tok = (a & (n_tok - 1)) if n_tok & (n_tok - 1) == 0 else lax.rem(a, jnp.int32(n_tok))
            slot_s[dest_ref[a]] = tok
            return carry

        lax.fori_loop(0, dest_ref.shape[0], scatter, 0, unroll=16)
        buf_ref[...] = jnp.zeros(buf_ref.shape, buf_ref.dtype)

        def body(r, carry):
            pltpu.make_async_copy(x_hbm.at[pl.ds(source_row(0, r), 1)], buf_ref.at[0, pl.ds(r, 1)],
                                  sems.at[0]).start()
            return carry

        lax.fori_loop(0, n_copies(0), body, 0)

    @pl.when(t + 1 < used)
    def _():
        issue(t + 1)

    @pl.when(t < used)
    def _():
        b = t % 2
        nc = n_copies(t)

        @pl.when(nc > 0)
        def _():
            pltpu.make_async_copy(x_hbm.at[pl.ds(0, nc)], buf_ref.at[b, pl.ds(0, nc)], sems.at[b]).wait()

        o_ref[...] = buf_ref[b].astype(o_ref.dtype)


def gather_rows(dest_flat, tile_rows, used_tiles, xn, n_slots, tr):
    n, d = xn.shape
    grid_spec = pltpu.PrefetchScalarGridSpec(
        num_scalar_prefetch=3,
        grid=(n_slots // tr,),
        in_specs=[pl.BlockSpec(memory_space=pl.ANY)],
        out_specs=pl.BlockSpec((tr, d), lambda t, dest, rows, used: (jnp.minimum(t, used[0] - 1), 0)),
        scratch_shapes=[pltpu.SMEM((n_slots,), jnp.int32), pltpu.VMEM((2, tr, d), F32),
                        pltpu.SemaphoreType.DMA((2,))],
    )
    return pl.pallas_call(
        functools.partial(_row_gather_kernel, n_tok=n),
        grid_spec=grid_spec,
        out_shape=jax.ShapeDtypeStruct((n_slots, d), BF16),
        compiler_params=pltpu.CompilerParams(dimension_semantics=("arbitrary",), vmem_limit_bytes=VMEM_LIMIT,
                                             disable_bounds_checks=True),
        name="moe_gather_rows",
    )(dest_flat, tile_rows, used_tiles, xn)


def _expert_kernel(te1_ref, tv1_ref, tb1_ref, te2_ref, tv2_ref, tb2_ref,
                   x_ref, wg_ref, wl_ref, bg_ref, bl_ref, w2_ref, b2_ref, o_ref, act_s, *, nf, row_counts):
    jj = pl.program_id(0)
    s = pl.program_id(1)
    tf = wg_ref.shape[1]
    valid_in = tv1_ref[jj]
    valid_out = tv2_ref[jj]
    buf_in = jj % 2
    buf_out = 1 - buf_in

    tm = x_ref.shape[0]
    bounds = tuple(zip((0,) + tuple(row_counts[:-1]), row_counts))

    for lo, rn in bounds:
        @pl.when(jnp.logical_and(valid_in > lo, valid_in <= rn))
        def _(rn=rn):
            x = x_ref[0:rn, :]
            glu = jnp.dot(x, wg_ref[...].astype(BF16), preferred_element_type=F32) + bg_ref[...]
            lin = jnp.dot(x, wl_ref[...].astype(BF16), preferred_element_type=F32) + bl_ref[...]
            glu = jnp.minimum(glu, SWIGLU_LIMIT)
            lin = jnp.clip(lin, -SWIGLU_LIMIT, SWIGLU_LIMIT)
            act = glu * jax.nn.sigmoid(SWIGLU_ALPHA * glu) * (lin + 1.0)
            act_s[buf_in, s, 0:rn, :] = act.astype(BF16)

    for lo, rn in bounds:
        @pl.when(jnp.logical_and(valid_out > lo, valid_out <= rn))
        def _(rn=rn):
            y = b2_ref[...] + jnp.dot(act_s[buf_out, 0, 0:rn, :], w2_ref[0:tf, :].astype(BF16),
                                      preferred_element_type=F32)
            for f in range(1, nf):
                y = y + jnp.dot(act_s[buf_out, f, 0:rn, :], w2_ref[f * tf:(f + 1) * tf, :].astype(BF16),
                                preferred_element_type=F32)
            o_ref[0:rn, :] = y
            if rn < tm:
                o_ref[rn:tm, :] = jnp.zeros((tm - rn, o_ref.shape[1]), o_ref.dtype)


def expert_ffn(tile_expert, tile_valid, tile_block, xs, w1, b1, w2, b2, tm, row_counts, tf=512):
    n_slots, d = xs.shape
    n_tiles = n_slots // tm
    nf = D_FF // tf
    assert d // tf == nf and row_counts[-1] == tm
    b1r = b1.reshape(N_EXPERTS, 1, 2 * D_FF)
    b2r = b2.reshape(N_EXPERTS, 1, d)
    zero = jnp.zeros((1,), jnp.int32)
    te1 = jnp.concatenate([tile_expert, tile_expert[-1:]])
    tv1 = jnp.concatenate([tile_valid, zero])
    tb1 = jnp.concatenate([tile_block, tile_block[-1:]])
    te2 = jnp.concatenate([tile_expert[:1], tile_expert])
    tv2 = jnp.concatenate([zero, tile_valid])
    tb2 = jnp.concatenate([tile_block[:1], tile_block])

    def blk1(jj, s, tv1):
        return jnp.where(tv1[jj] > 0, s, nf - 1)

    def blk2(jj, s, tv2):
        return jnp.where(tv2[jj] > 0, s, jnp.where(jj == 0, 0, nf - 1))

    grid_spec = pltpu.PrefetchScalarGridSpec(
        num_scalar_prefetch=6,
        grid=(n_tiles + 1, nf),
        in_specs=[
            pl.BlockSpec((tm, d), lambda jj, s, te1, tv1, tb1, te2, tv2, tb2: (tb1[jj], 0)),
            pl.BlockSpec((None, d, tf),
                         lambda jj, s, te1, tv1, tb1, te2, tv2, tb2: (te1[jj], 0, blk1(jj, s, tv1))),
            pl.BlockSpec((None, d, tf),
                         lambda jj, s, te1, tv1, tb1, te2, tv2, tb2: (te1[jj], 0, nf + blk1(jj, s, tv1))),
            pl.BlockSpec((None, 1, tf),
                         lambda jj, s, te1, tv1, tb1, te2, tv2, tb2: (te1[jj], 0, blk1(jj, s, tv1))),
            pl.BlockSpec((None, 1, tf),
                         lambda jj, s, te1, tv1, tb1, te2, tv2, tb2: (te1[jj], 0, nf + blk1(jj, s, tv1))),
            pl.BlockSpec((None, D_FF, tf),
                         lambda jj, s, te1, tv1, tb1, te2, tv2, tb2: (te2[jj], 0, blk2(jj, s, tv2))),
            pl.BlockSpec((None, 1, tf),
                         lambda jj, s, te1, tv1, tb1, te2, tv2, tb2: (te2[jj], 0, blk2(jj, s, tv2))),
        ],
        out_specs=pl.BlockSpec((tm, tf),
                               lambda jj, s, te1, tv1, tb1, te2, tv2, tb2: (tb2[jj], blk2(jj, s, tv2))),
        scratch_shapes=[pltpu.VMEM((2, nf, tm, tf), BF16)],
    )
    return pl.pallas_call(
        functools.partial(_expert_kernel, nf=nf, row_counts=row_counts),
        grid_spec=grid_spec,
        out_shape=jax.ShapeDtypeStruct((n_slots, d), F32),
        compiler_params=_cparams(("arbitrary", "arbitrary")),
        name="moe_expert_ffn",
    )(te1, tv1, tb1, te2, tv2, tb2, xs, w1, w1, b1r, b1r, w2, b2r)


def _combine_kernel(dest_ref, ys_hbm, x2_ref, meta_ref, gain_ref, o_ref, buf_ref, sems):
    t = pl.program_id(0)
    nt = pl.num_programs(0)
    tt = x2_ref.shape[0]
    n_tok = dest_ref.shape[0] // TOP_K

    def issue_rolled(tile, b):
        base = tile * tt

        def body(r, carry):
            for k in range(TOP_K):
                pltpu.make_async_copy(ys_hbm.at[pl.ds(dest_ref[k * n_tok + base + r], 1)],
                                      buf_ref.at[b, k, pl.ds(r, 1)], sems.at[b]).start()
            return carry

        lax.fori_loop(0, tt, body, 0, unroll=4)

    def issue(tile):
        base = tile * tt
        for b in range(2):
            @pl.when(tile % 2 == b)
            def _(b=b):
                for r in range(tt):
                    for k in range(TOP_K):
                        pltpu.make_async_copy(ys_hbm.at[pl.ds(dest_ref[k * n_tok + base + r], 1)],
                                              buf_ref.at[b, k, pl.ds(r, 1)], sems.at[b]).start(priority=k % 2)

    @pl.when(t == 0)
    def _():
        issue_rolled(t, 0)

    @pl.when(t + 1 < nt)
    def _():
        issue(t + 1)

    b = t % 2
    for k in range(TOP_K):
        pltpu.make_async_copy(ys_hbm.at[pl.ds(0, tt)], buf_ref.at[b, k], sems.at[b]).wait()
    meta = meta_ref[...]
    y = x2_ref[...]
    for k in range(TOP_K):
        y = y + meta[:, 2 * TOP_K + k:2 * TOP_K + k + 1] * buf_ref[b, k]
    o_ref[...] = y * lax.rsqrt(jnp.mean(y * y, axis=-1, keepdims=True) + NORM_EPS) * gain_ref[...]


def combine_final(dest_flat, ys, x2, meta, final_gain, tt=128):
    n, d = x2.shape
    grid_spec = pltpu.PrefetchScalarGridSpec(
        num_scalar_prefetch=1,
        grid=(n // tt,),
        in_specs=[pl.BlockSpec(memory_space=pl.ANY),
                  pl.BlockSpec((tt, d), lambda t, dest: (t, 0)),
                  pl.BlockSpec((tt, LANES), lambda t, dest: (t, 0)),
                  pl.BlockSpec((1, d), lambda t, dest: (0, 0))],
        out_specs=pl.BlockSpec((tt, d), lambda t, dest: (t, 0)),
        scratch_shapes=[pltpu.VMEM((2, TOP_K, tt, d), F32), pltpu.SemaphoreType.DMA((2,))],
    )
    return pl.pallas_call(
        _combine_kernel,
        grid_spec=grid_spec,
        out_shape=jax.ShapeDtypeStruct((n, d), F32),
        compiler_params=pltpu.CompilerParams(dimension_semantics=("arbitrary",), vmem_limit_bytes=VMEM_LIMIT,
                                             disable_bounds_checks=True),
        name="moe_combine_final",
    )(dest_flat, ys, x2, meta, final_gain.reshape(1, d))


MOE_TILE = 1152
MOE_ROW_COUNTS = (1024, 1152)
MOE_GATHER_TILE = 384


def moe_ffn_final(x2, xn, meta, meta_t, counts, w1, b1, w2, b2, final_gain, tm=MOE_TILE):
    n, d = x2.shape
    a = n * TOP_K
    n_tiles = -(-a // tm) + N_EXPERTS
    n_slots = n_tiles * tm
    idx = meta_t[0:TOP_K].astype(jnp.int32)
    rank = meta_t[TOP_K:2 * TOP_K].astype(jnp.int32)
    cnt = counts[0, :N_EXPERTS].astype(jnp.int32)
    tiles_e = (cnt + tm - 1) // tm
    tile_end = jnp.cumsum(tiles_e)
    tile_start = tile_end - tiles_e
    used = tile_end[-1]
    experts = jnp.arange(N_EXPERTS, dtype=jnp.int32)[:, None, None]
    first_slot = jnp.sum(jnp.where(idx[None] == experts, (tile_start * tm)[:, None, None], 0), axis=0)
    dest = first_slot + rank
    dest_flat = dest.reshape(a)
    tj = jnp.arange(n_tiles, dtype=jnp.int32)
    te = jnp.minimum(jnp.sum(tj[:, None] >= tile_end[None, :], axis=1), N_EXPERTS - 1).astype(jnp.int32)
    last_e = te[jnp.maximum(used - 1, 0)]
    tile_expert = jnp.where(tj < used, te, last_e).astype(jnp.int32)
    tile_valid = jnp.where(tj < used, jnp.clip(cnt[te] - (tj - tile_start[te]) * tm, 0, tm), 0).astype(jnp.int32)
    tile_block = jnp.maximum(jnp.minimum(tj, used - 1), 0).astype(jnp.int32)

    tr = MOE_GATHER_TILE
    per = tm // tr
    gj = jnp.arange(n_tiles * per, dtype=jnp.int32)
    gather_rows_valid = jnp.clip(tile_valid[gj // per] - (gj % per) * tr, 0, tr).astype(jnp.int32)
    used_gather = jnp.maximum(used * per, 1).astype(jnp.int32).reshape(1)
    xs = gather_rows(dest_flat, gather_rows_valid, used_gather, xn, n_slots, tr)
    ys = expert_ffn(tile_expert, tile_valid, tile_block, xs, w1, b1, w2, b2, tm, MOE_ROW_COUNTS)
    return combine_final(dest_flat, ys, x2, meta, final_gain)


def kernel(x, norm_mix_gain, w_in, hgrn_lb_logits, hgrn_norm_gain, hy_conv_w, hy_conv_b, hy_filt_w1,
           hy_filt_b1, hy_filt_w2, hy_filt_b2, hy_filt_w3, hy_filt_b3, hy_filt_w4, hy_filt_freq, hy_bias,
           hy_norm_gain, w_out, norm_ffn_gain, w_router, b_router, moe_w1, moe_b1, moe_w2, moe_b2,
           final_norm_gain):
    b, l, d = x.shape
    n = b * l
    x2d = x.reshape(n, d)
    proj = in_projection(x2d, norm_mix_gain[0], w_in[0])
    proj3 = proj.reshape(b, l, D_IN)
    hg = hgrn2_mixer(proj3, hgrn_lb_logits, hgrn_norm_gain[0])

    sym, flt = (jnp.asarray(tab).astype(BF16) for tab in _dft_tables(l))
    tap_sum, tap_diff = hyena_filter_taps(l, hy_filt_w1[0], hy_filt_b1[0], hy_filt_w2[0], hy_filt_b2[0],
                                          hy_filt_w3[0], hy_filt_b3[0], hy_filt_w4[0], hy_filt_freq[0])
    spectra = hyena_filter_spectra(flt, jnp.stack([tap_sum, tap_diff]))
    hy = hyena_mixer(proj3, hy_conv_w[0], hy_conv_b[0], sym, spectra, hy_bias[0])

    x2, xn, meta, meta_t, counts = outproj_router(hg.reshape(n, HG_WIDTH), hy.reshape(n, HY_WIDTH), x2d,
                                                  w_out[0], hy_norm_gain[0], norm_ffn_gain[0],
                                                  w_router[0], b_router[0])
    out = moe_ffn_final(x2, xn, meta, meta_t, counts, moe_w1[0], moe_b1[0], moe_w2[0], moe_b2[0],
                        final_norm_gain)
    return out.reshape(b, l, d)
```
